```python
import math
import jax, jax.numpy as jnp
from jax import lax
import numpy as np

D_MODEL = 2048
BATCH = 16
SEQ = 2048
DEPTH = 2

HEAD_DIM = 128
N_HEADS_FOX = 4
N_HEADS_NSA = 4
N_KV_NSA = 2
N_HEADS_SB = 4
N_HEADS_DIFF = 4
DIFF_DIM = HEAD_DIM // 2
BRANCH_WIDTH = 512
N_BRANCHES = 4
Q_BLOCK = 128
ROPE_THETA = 10000.0
NORM_EPS = 1e-6
CMP_BLOCK = 32
CMP_STRIDE = 16
SEL_BLOCK = 64
SEL_TOP_N = 16
WINDOW = 512
FORCE_SCORE = 1e6
N_GROUPS = 4
EXPERTS_PER_GROUP = 8
N_EXPERTS = N_GROUPS * EXPERTS_PER_GROUP
TOP_K_EXPERTS = 2
EXPERT_FF = 1024
MOE_BLOCK = 128

SPLIT_WIDTHS = (
    N_HEADS_FOX * HEAD_DIM, N_HEADS_FOX * HEAD_DIM, N_HEADS_FOX * HEAD_DIM, N_HEADS_FOX,
    N_HEADS_NSA * HEAD_DIM,
    N_KV_NSA * HEAD_DIM, N_KV_NSA * HEAD_DIM,
    N_KV_NSA * HEAD_DIM, N_KV_NSA * HEAD_DIM,
    N_KV_NSA * HEAD_DIM, N_KV_NSA * HEAD_DIM,
    3 * N_HEADS_NSA,
    N_HEADS_SB * HEAD_DIM, N_HEADS_SB * HEAD_DIM, N_HEADS_SB * HEAD_DIM,
    N_HEADS_DIFF * DIFF_DIM, N_HEADS_DIFF * DIFF_DIM,
    N_HEADS_DIFF * DIFF_DIM, N_HEADS_DIFF * DIFF_DIM,
    N_HEADS_DIFF * HEAD_DIM,
)
IN_COLS = sum(SPLIT_WIDTHS)

kernel_name = "hybrid_fox_nsa_stickbreak_diff_hiermoe"


def rms_norm(x, g):
    xf = x.astype(jnp.float32)
    y = xf * lax.rsqrt(jnp.mean(xf * xf, axis=-1, keepdims=True) + NORM_EPS)
    return (y * g.astype(jnp.float32)).astype(x.dtype)


def rope(x):
    S, d = x.shape[1], x.shape[-1]
    inv = jnp.exp(-math.log(ROPE_THETA) * jnp.arange(0, d, 2, dtype=jnp.float32) / d)
    ang = jnp.arange(S, dtype=jnp.float32)[:, None] * inv[None, :]
    cos, sin = jnp.cos(ang)[:, None, :], jnp.sin(ang)[:, None, :]
    xf = x.astype(jnp.float32)
    x1, x2 = xf[..., : d // 2], xf[..., d // 2:]
    return jnp.concatenate([x1 * cos - x2 * sin, x2 * cos + x1 * sin], axis=-1).astype(x.dtype)


def masked_softmax(s, mask):
    s = jnp.where(mask, s.astype(jnp.float32), -jnp.inf)
    m = jnp.max(s, axis=-1, keepdims=True)
    m = jnp.where(jnp.isfinite(m), m, 0.0)
    e = jnp.where(mask, jnp.exp(s - m), 0.0)
    return e / jnp.maximum(jnp.sum(e, axis=-1, keepdims=True), 1e-30)


def causal_sweep(block_fn, seq):
    return jnp.concatenate([block_fn(i * Q_BLOCK, (i + 1) * Q_BLOCK) for i in range(seq // Q_BLOCK)], axis=2)


def fox_attention(q, k, v, f_logit):
    scale = q.shape[-1] ** -0.5
    cum = jnp.cumsum(jax.nn.log_sigmoid(f_logit.astype(jnp.float32)), axis=-1)

    def blk(q0, q1):
        s = jnp.einsum('bhqd,bhkd->bhqk', q[:, :, q0:q1], k[:, :, :q1]).astype(jnp.float32) * scale
        s = s + cum[:, :, q0:q1, None] - cum[:, :, None, :q1]
        mask = jnp.arange(q1)[None, :] <= jnp.arange(q0, q1)[:, None]
        p = masked_softmax(s, mask)
        return jnp.einsum('bhqk,bhkd->bhqd', p.astype(v.dtype), v[:, :, :q1])

    return causal_sweep(blk, q.shape[2])


def stick_breaking_attention(q, k, v):
    scale = q.shape[-1] ** -0.5

    def blk(q0, q1):
        z = jnp.einsum('bhqd,bhkd->bhqk', q[:, :, q0:q1], k[:, :, :q1]).astype(jnp.float32) * scale
        mask = jnp.arange(q1)[None, :] < jnp.arange(q0, q1)[:, None]
        l_neg = jnp.where(mask, jax.nn.log_sigmoid(-z), 0.0)
        after = lax.cumsum(l_neg, axis=3, reverse=True) - l_neg
        a = jnp.where(mask, jnp.exp(jax.nn.log_sigmoid(z) + after), 0.0)
        return jnp.einsum('bhqk,bhkd->bhqd', a.astype(v.dtype), v[:, :, :q1])

    return causal_sweep(blk, q.shape[2])


def diff_attention(q1, q2, k1, k2, v, lam, lam_init, g):
    scale = q1.shape[-1] ** -0.5

    def blk(q0, q1_):
        mask = jnp.arange(q1_)[None, :] <= jnp.arange(q0, q1_)[:, None]
        s1 = jnp.einsum('bhqd,bhkd->bhqk', q1[:, :, q0:q1_], k1[:, :, :q1_]) * scale
        s2 = jnp.einsum('bhqd,bhkd->bhqk', q2[:, :, q0:q1_], k2[:, :, :q1_]) * scale
        p = masked_softmax(s1, mask) - lam * masked_softmax(s2, mask)
        return jnp.einsum('bhqk,bhkd->bhqd', p.astype(v.dtype), v[:, :, :q1_])

    o = causal_sweep(blk, q1.shape[2])
    return rms_norm(o, g) * (1.0 - lam_init)


def nsa_attention(q, kc_tok, vc_tok, ks, vs, kw, vw, gate_logit, pe_k, pe_v, w_ck, w_cv):
    B, S, H, d = q.shape
    G = kc_tok.shape[2]
    HPG = H // G
    scale = d ** -0.5
    qg = q.reshape(B, S, G, HPG, d)
    t = jnp.arange(S)

    n_cmp = (S - CMP_BLOCK) // CMP_STRIDE + 1
    idx = jnp.arange(n_cmp)[:, None] * CMP_STRIDE + jnp.arange(CMP_BLOCK)[None, :]
    kc = jnp.einsum('bnlgd,lde->bnge', kc_tok[:, idx] + pe_k[:, None, :], w_ck)
    vc = jnp.einsum('bnlgd,lde->bnge', vc_tok[:, idx] + pe_v[:, None, :], w_cv)
    s_c = jnp.einsum('bsghd,bngd->bghsn', qg, kc) * scale
    mask_c = (jnp.arange(n_cmp) * CMP_STRIDE + CMP_BLOCK - 1)[None, :] <= t[:, None]
    p_c = masked_softmax(s_c, mask_c)
    o_c = jnp.einsum('bghsn,bngd->bsghd', p_c.astype(vc.dtype), vc)

    n_sel = S // SEL_BLOCK
    ci = jnp.arange(n_cmp)[:, None] * CMP_STRIDE
    sj = jnp.arange(n_sel)[None, :] * SEL_BLOCK
    overlap = ((ci < sj + SEL_BLOCK) & (ci + CMP_BLOCK > sj)).astype(jnp.float32)
    imp = jnp.einsum('bghsn,nj->bgsj', p_c, overlap)
    cur = (t // SEL_BLOCK)[:, None]
    j = jnp.arange(n_sel)[None, :]
    imp = jnp.where((j == 0) | (j == cur) | (j == cur - 1), FORCE_SCORE, imp)
    imp = jnp.where(j <= cur, imp, -jnp.inf)
    k_top = min(SEL_TOP_N, n_sel)
    _, sel = lax.top_k(imp, k_top)

    nqb = S // Q_BLOCK
    q_blocks = qg.reshape(B, nqb, Q_BLOCK, G, HPG, d).transpose(1, 0, 3, 4, 2, 5)
    sel_blocks = sel.reshape(B, G, nqb, Q_BLOCK, k_top).transpose(2, 0, 1, 3, 4)
    kb = ks.reshape(B, n_sel, SEL_BLOCK, G, d).transpose(0, 3, 1, 2, 4)
    vb = vs.reshape(B, n_sel, SEL_BLOCK, G, d).transpose(0, 3, 1, 2, 4)
    gather = jax.vmap(jax.vmap(lambda blocks, ids: blocks[ids]))

    def sel_block(args):
        qb, ids, i = args
        kg = gather(kb, ids)
        vg = gather(vb, ids)
        tq = i * Q_BLOCK + jnp.arange(Q_BLOCK)
        pos = ids[..., None] * SEL_BLOCK + jnp.arange(SEL_BLOCK)
        mask = (pos <= tq[:, None, None]).reshape(B, G, 1, Q_BLOCK, k_top * SEL_BLOCK)
        s = jnp.einsum('bghqd,bgqkld->bghqkl', qb, kg).reshape(B, G, HPG, Q_BLOCK, k_top * SEL_BLOCK) * scale
        p = masked_softmax(s, mask).reshape(B, G, HPG, Q_BLOCK, k_top, SEL_BLOCK)
        return jnp.einsum('bghqkl,bgqkld->bghqd', p.astype(vg.dtype), vg)

    o_s = lax.map(sel_block, (q_blocks, sel_blocks, jnp.arange(nqb)))
    o_s = o_s.transpose(1, 0, 4, 2, 3, 5).reshape(B, S, G, HPG, d)

    kwp = jnp.pad(kw, ((0, 0), (WINDOW, 0), (0, 0), (0, 0)))
    vwp = jnp.pad(vw, ((0, 0), (WINDOW, 0), (0, 0), (0, 0)))

    def win_block(args):
        qb, i = args
        q0 = i * Q_BLOCK
        kk = lax.dynamic_slice_in_dim(kwp, q0, Q_BLOCK + WINDOW, axis=1)
        vv = lax.dynamic_slice_in_dim(vwp, q0, Q_BLOCK + WINDOW, axis=1)
        tq = q0 + jnp.arange(Q_BLOCK)
        ts = q0 - WINDOW + jnp.arange(Q_BLOCK + WINDOW)
        gap = tq[:, None] - ts[None, :]
        mask = (gap >= 0) & (gap < WINDOW) & (ts >= 0)[None, :]
        s = jnp.einsum('bghqd,bkgd->bghqk', qb, kk) * scale
        p = masked_softmax(s, mask)
        return jnp.einsum('bghqk,bkgd->bghqd', p.astype(vv.dtype), vv)

    o_w = lax.map(win_block, (q_blocks, jnp.arange(nqb)))
    o_w = o_w.transpose(1, 0, 4, 2, 3, 5).reshape(B, S, G, HPG, d)

    g = jax.nn.sigmoid(gate_logit.astype(jnp.float32)).astype(q.dtype).reshape(B, S, G, HPG, 3)
    o = g[..., 0:1] * o_c + g[..., 1:2] * o_s + g[..., 2:3] * o_w
    return o.reshape(B, S, H * d)


def hybrid_mixer(h, w_in, fox_bf, pe_k, pe_v, w_ck, w_cv, lq1, lk1, lq2, lk2, diff_g,
                 w_branch, w_mgate, w_out, lam_init):
    B, S, _ = h.shape
    u = h @ w_in
    points = np.cumsum(SPLIT_WIDTHS)[:-1].tolist()
    (fq, fk, fv, ff, nq, nkc, nvc, nks, nvs, nkw, nvw, ngt,
     sq, sk, sv, dq1, dq2, dk1, dk2, dv) = jnp.split(u, points, axis=-1)
    heads = lambda a, n: a.reshape(B, S, n, -1)
    bhsd = lambda a: a.transpose(0, 2, 1, 3)
    flat = lambda o: o.transpose(0, 2, 1, 3).reshape(B, S, -1)

    o_fox = flat(fox_attention(bhsd(heads(fq, N_HEADS_FOX)), bhsd(heads(fk, N_HEADS_FOX)),
                               bhsd(heads(fv, N_HEADS_FOX)), (ff + fox_bf).transpose(0, 2, 1)))
    o_nsa = nsa_attention(rope(heads(nq, N_HEADS_NSA)),
                          rope(heads(nkc, N_KV_NSA)), heads(nvc, N_KV_NSA),
                          rope(heads(nks, N_KV_NSA)), heads(nvs, N_KV_NSA),
                          rope(heads(nkw, N_KV_NSA)), heads(nvw, N_KV_NSA),
                          heads(ngt, N_HEADS_NSA), pe_k, pe_v, w_ck, w_cv)
    o_sb = flat(stick_breaking_attention(bhsd(heads(sq, N_HEADS_SB)), bhsd(heads(sk, N_HEADS_SB)),
                                         bhsd(heads(sv, N_HEADS_SB))))
    lam = (jnp.exp(jnp.sum(lq1.astype(jnp.float32) * lk1.astype(jnp.float32)))
           - jnp.exp(jnp.sum(lq2.astype(jnp.float32) * lk2.astype(jnp.float32))) + lam_init)
    o_diff = flat(diff_attention(bhsd(rope(heads(dq1, N_HEADS_DIFF))), bhsd(rope(heads(dq2, N_HEADS_DIFF))),
                                 bhsd(rope(heads(dk1, N_HEADS_DIFF))), bhsd(rope(heads(dk2, N_HEADS_DIFF))),
                                 bhsd(heads(dv, N_HEADS_DIFF)), lam, lam_init, diff_g))

    branches = (o_fox, o_nsa, o_sb, o_diff)
    mixed = jax.nn.sigmoid(h @ w_mgate[0]) * (branches[0] @ w_branch[0])
    for n in range(1, N_BRANCHES):
        mixed = mixed + jax.nn.sigmoid(h @ w_mgate[n]) * (branches[n] @ w_branch[n])
    return mixed @ w_out


def grouped_experts(xf, eid, wts, w_g, w_u, w_d):
    N, D = xf.shape
    n_exp = w_g.shape[0]
    k = eid.shape[1]
    M = N * k
    slot_e = eid.reshape(M)
    slot_tok = jnp.repeat(jnp.arange(N, dtype=jnp.int32), k)
    slot_w = wts.reshape(M)
    order = jnp.argsort(slot_e)
    se, stok, sw = slot_e[order], slot_tok[order], slot_w[order]
    counts = jnp.bincount(slot_e, length=n_exp)
    start = jnp.cumsum(counts) - counts
    padded = (counts + MOE_BLOCK - 1) // MOE_BLOCK * MOE_BLOCK
    pend = jnp.cumsum(padded)
    pstart = pend - padded
    dest = pstart[se] + jnp.arange(M) - start[se]
    n_rows = -(-(M + n_exp * MOE_BLOCK) // MOE_BLOCK) * MOE_BLOCK
    n_blk = n_rows // MOE_BLOCK
    buf_tok = jnp.full((n_rows,), N, jnp.int32).at[dest].set(stok)
    x_pad = jnp.concatenate([xf, jnp.zeros((1, D), xf.dtype)], axis=0)
    xb = x_pad[buf_tok].reshape(n_blk, MOE_BLOCK, D)
    blk_e = jnp.minimum(jnp.searchsorted(pend, jnp.arange(n_blk) * MOE_BLOCK, side='right'), n_exp - 1)

    def expert_block(args):
        xblk, e = args
        hid = jax.nn.silu(xblk @ w_g[e]) * (xblk @ w_u[e])
        return hid @ w_d[e]

    yb = lax.map(expert_block, (xb, blk_e)).reshape(n_rows, D)
    return jnp.zeros_like(xf).at[stok].add(yb[dest] * sw[:, None])


def hier_moe(h, w_rg, b_rg, w_re, b_re, w_eg, w_eu, w_ed):
    B, S, D = h.shape
    xf = h.reshape(B * S, D)
    lg = (xf @ w_rg).astype(jnp.float32) + b_rg.astype(jnp.float32)
    gi = jnp.argmax(lg, axis=-1)
    pg = jnp.take_along_axis(jax.nn.softmax(lg, axis=-1), gi[:, None], axis=-1)
    le = ((xf @ w_re).astype(jnp.float32) + b_re.astype(jnp.float32)).reshape(-1, N_GROUPS, EXPERTS_PER_GROUP)
    le = jnp.take_along_axis(le, gi[:, None, None], axis=1)[:, 0]
    top_v, top_i = lax.top_k(le, TOP_K_EXPERTS)
    wts = jax.nn.softmax(top_v, axis=-1) * pg
    eid = gi[:, None] * EXPERTS_PER_GROUP + top_i
    y = grouped_experts(xf, eid, wts.astype(xf.dtype), w_eg, w_eu, w_ed)
    return y.reshape(B, S, D)


def setup_inputs(seed: int = 0) -> dict:
    key = jax.random.key(seed)
    ks = jax.random.split(key, 26)
    nrm = lambda k, shape, s: jax.random.normal(k, shape, jnp.float32) * s
    D, L, hd = D_MODEL, CMP_BLOCK, HEAD_DIM
    return {
        "x": nrm(ks[0], (BATCH, SEQ, D), 1.0),
        "norm1_g": 1.0 + nrm(ks[1], (DEPTH, D), 0.02),
        "w_in": nrm(ks[2], (DEPTH, D, IN_COLS), D ** -0.5),
        "fox_bf": jax.random.uniform(ks[3], (DEPTH, N_HEADS_FOX), jnp.float32, 1.0, 4.0),
        "nsa_pe_k": nrm(ks[4], (DEPTH, L, hd), 0.1),
        "nsa_pe_v": nrm(ks[5], (DEPTH, L, hd), 0.1),
        "nsa_w_ck": nrm(ks[6], (DEPTH, L, hd, hd), (L * hd) ** -0.5),
        "nsa_w_cv": nrm(ks[7], (DEPTH, L, hd, hd), (L * hd) ** -0.5),
        "diff_lq1": nrm(ks[8], (DEPTH, DIFF_DIM), 0.1),
        "diff_lk1": nrm(ks[9], (DEPTH, DIFF_DIM), 0.1),
        "diff_lq2": nrm(ks[10], (DEPTH, DIFF_DIM), 0.1),
        "diff_lk2": nrm(ks[11], (DEPTH, DIFF_DIM), 0.1),
        "diff_norm_g": 1.0 + nrm(ks[12], (DEPTH, HEAD_DIM), 0.02),
        "w_branch": nrm(ks[13], (DEPTH, N_BRANCHES, BRANCH_WIDTH, D), BRANCH_WIDTH ** -0.5),
        "w_mgate": nrm(ks[14], (DEPTH, N_BRANCHES, D, D), D ** -0.5),
        "w_out": nrm(ks[15], (DEPTH, D, D), D ** -0.5),
        "norm2_g": 1.0 + nrm(ks[16], (DEPTH, D), 0.02),
        "w_rg": nrm(ks[17], (DEPTH, D, N_GROUPS), D ** -0.5),
        "b_rg": nrm(ks[18], (DEPTH, N_GROUPS), 0.01),
        "w_re": nrm(ks[19], (DEPTH, D, N_EXPERTS), D ** -0.5),
        "b_re": nrm(ks[20], (DEPTH, N_EXPERTS), 0.01),
        "w_eg": nrm(ks[21], (DEPTH, N_EXPERTS, D, EXPERT_FF), D ** -0.5),
        "w_eu": nrm(ks[22], (DEPTH, N_EXPERTS, D, EXPERT_FF), D ** -0.5),
        "w_ed": nrm(ks[23], (DEPTH, N_EXPERTS, EXPERT_FF, D), EXPERT_FF ** -0.5),
        "final_g": 1.0 + nrm(ks[24], (D,), 0.02),
    }


def reference(x, norm1_g, w_in, fox_bf, nsa_pe_k, nsa_pe_v, nsa_w_ck, nsa_w_cv,
              diff_lq1, diff_lk1, diff_lq2, diff_lk2, diff_norm_g, w_branch, w_mgate, w_out,
              norm2_g, w_rg, b_rg, w_re, b_re, w_eg, w_eu, w_ed, final_g):
    for l in range(DEPTH):
        lam_init = 0.8 - 0.6 * math.exp(-0.3 * l)
        h = rms_norm(x, norm1_g[l])
        x = x + hybrid_mixer(h, w_in[l], fox_bf[l], nsa_pe_k[l], nsa_pe_v[l], nsa_w_ck[l], nsa_w_cv[l],
                             diff_lq1[l], diff_lk1[l], diff_lq2[l], diff_lk2[l], diff_norm_g[l],
                             w_branch[l], w_mgate[l], w_out[l], lam_init)
        h = rms_norm(x, norm2_g[l])
        x = x + hier_moe(h, w_rg[l], b_rg[l], w_re[l], b_re[l], w_eg[l], w_eu[l], w_ed[l])
    return rms_norm(x, final_g)
```

```python
import functools
import math

import jax
import jax.numpy as jnp
from jax import lax
from jax.experimental import pallas as pl
from jax.experimental.pallas import tpu as pltpu

F32 = jnp.float32
BF16 = jnp.bfloat16
NEG_INF = float("-inf")

D_MODEL = 2048
HEAD_DIM = 128
DIFF_DIM = 64
N_HEADS = 4
N_KV_NSA = 2
BRANCH_WIDTH = 512
ROPE_THETA = 10000.0
NORM_EPS = 1e-6
CMP_BLOCK = 32
CMP_STRIDE = 16
SEL_BLOCK = 64
SEL_TOP_N = 16
WINDOW = 512
FORCE_SCORE = 1e6
N_GROUPS = 4
EXPERTS_PER_GROUP = 8
N_EXPERTS = 32
EXPERT_FF = 1024

LANES = 128
ROW_TILES = D_MODEL // LANES
MIB = 1024 * 1024

_PIECES = (("fq", 512), ("fk", 512), ("fv", 512), ("nq", 512), ("nkc", 256), ("nvc", 256),
           ("nks", 256), ("nvs", 256), ("nkw", 256), ("nvw", 256), ("sq", 512), ("sk", 512),
           ("sv", 512), ("dq1", 256), ("dq2", 256), ("dk1", 256), ("dk2", 256), ("dv", 512),
           ("small", 256))
COL = {}
_off = 0
for _n, _w in _PIECES:
    COL[_n] = _off // LANES
    _off += _w
C_TOT = _off
SMALL_OFF = COL["small"] * LANES

_ORIG_WIDTHS = (512, 512, 512, 4, 512, 256, 256, 256, 256, 256, 256, 12, 512, 512, 512, 256, 256, 256, 256, 512)
_ORIG_NAMES = ("fq", "fk", "fv", "ff", "nq", "nkc", "nvc", "nks", "nvs", "nkw", "nvw", "ngt",
               "sq", "sk", "sv", "dq1", "dq2", "dk1", "dk2", "dv")


def _cparams(sem, vmem_mib):
    return pltpu.CompilerParams(dimension_semantics=sem, vmem_limit_bytes=vmem_mib * MIB)


def _log_sigmoid(z):
    return jnp.minimum(z, 0.0) - jnp.log1p(jnp.exp(-jnp.abs(z)))


def _rope128(x, c, s):
    return x * c + pltpu.roll(x, 64, axis=1) * s


def _rope64(x, c, sa, sb):
    return x * c + pltpu.roll(x, 96, axis=1) * sa + pltpu.roll(x, 32, axis=1) * sb


def _dot_nt(a, b):
    return lax.dot_general(a, b, (((1,), (1,)), ((), ())), preferred_element_type=F32)


def _dot(a, b):
    return jnp.dot(a, b, preferred_element_type=F32)


def _norm_kernel(x_ref, g_ref, o_ref):
    x = x_ref[...]
    y = x * lax.rsqrt(jnp.mean(x * x, axis=-1, keepdims=True) + NORM_EPS)
    o_ref[...] = (y * g_ref[...]).astype(o_ref.dtype)


def rms_norm_rows(x, g, out_dtype, tm=512):
    n, d = x.shape
    return pl.pallas_call(
        _norm_kernel,
        grid=(n // tm,),
        in_specs=[pl.BlockSpec((tm, d), lambda i: (i, 0)), pl.BlockSpec((1, d), lambda i: (0, 0))],
        out_specs=pl.BlockSpec((tm, d), lambda i: (i, 0)),
        out_shape=jax.ShapeDtypeStruct((n, d), out_dtype),
        compiler_params=_cparams(("parallel",), 40),
        name="rms_norm",
    )(x, g.reshape(1, d))


def _mm_kernel(a_ref, w_ref, o_ref):
    o_ref[...] = _dot(a_ref[...], w_ref[...]).astype(o_ref.dtype)


def matmul(a, w, out_dtype, tm, tn):
    m, k = a.shape
    _, c = w.shape
    return pl.pallas_call(
        _mm_kernel,
        grid=(m // tm, c // tn),
        in_specs=[pl.BlockSpec((tm, k), lambda i, j: (i, 0)), pl.BlockSpec((k, tn), lambda i, j: (0, j))],
        out_specs=pl.BlockSpec((tm, tn), lambda i, j: (i, j)),
        out_shape=jax.ShapeDtypeStruct((m, c), out_dtype),
        compiler_params=_cparams(("parallel", "arbitrary"), 48),
        name="in_proj",
    )(a, w)


def _cum_kernel(f_ref, b_ref, o_ref):
    z = f_ref[0] + b_ref[...]
    ls = _log_sigmoid(z)
    s = ls.shape[1]
    lane = lax.broadcasted_iota(jnp.int32, ls.shape, 1)
    sh = 1
    while sh < s:
        ls = ls + jnp.where(lane >= sh, pltpu.roll(ls, sh, axis=1), 0.0)
        sh *= 2
    o_ref[0] = ls


def forget_cumsum(f_t, bias):
    b, h, s = f_t.shape
    return pl.pallas_call(
        _cum_kernel,
        grid=(b,),
        in_specs=[pl.BlockSpec((1, h, s), lambda i: (i, 0, 0)), pl.BlockSpec((h, 1), lambda i: (0, 0))],
        out_specs=pl.BlockSpec((1, h, s), lambda i: (i, 0, 0)),
        out_shape=jax.ShapeDtypeStruct((b, h, s), F32),
        name="fox_cumsum",
    )(f_t, bias.reshape(h, 1))


def _softmax_step(s, v, m_ref, l_ref, acc_ref):
    m_prev = m_ref[...]
    m_new = jnp.maximum(m_prev, jnp.max(s, axis=1, keepdims=True))
    m_safe = jnp.where(m_new == NEG_INF, 0.0, m_new)
    alpha = jnp.exp(m_prev - m_safe)
    p = jnp.exp(s - m_safe)
    l_ref[...] = alpha * l_ref[...] + jnp.sum(p, axis=1, keepdims=True)
    acc_ref[...] = alpha * acc_ref[...] + _dot(p.astype(BF16), v)
    m_ref[...] = m_new


def _fox_kernel(q_ref, k_ref, v_ref, cum_ref, o_ref, m_ref, l_ref, acc_ref, *, tq, tk):
    h = pl.program_id(1)
    qi = pl.program_id(2)
    q0 = qi * tq
    scale = HEAD_DIM ** -0.5
    q = q_ref[0].astype(BF16)
    m_ref[...] = jnp.full(m_ref.shape, NEG_INF, F32)
    l_ref[...] = jnp.zeros(l_ref.shape, F32)
    acc_ref[...] = jnp.zeros(acc_ref.shape, F32)

    def body(j, carry):
        c0 = pl.multiple_of(j * tk, tk)
        k = k_ref[0, pl.ds(c0, tk), :].astype(BF16)
        v = v_ref[0, pl.ds(c0, tk), :].astype(BF16)
        s = _dot_nt(q, k) * scale - cum_ref[0, pl.ds(h, 1), pl.ds(c0, tk)]
        row = q0 + lax.broadcasted_iota(jnp.int32, (tq, tk), 0)
        col = c0 + lax.broadcasted_iota(jnp.int32, (tq, tk), 1)
        s = jnp.where(col <= row, s, NEG_INF)
        _softmax_step(s, v, m_ref, l_ref, acc_ref)
        return carry

    lax.fori_loop(0, (q0 + tq + tk - 1) // tk, body, 0)
    o_ref[0] = (acc_ref[...] / jnp.maximum(l_ref[...], 1e-30)).astype(o_ref.dtype)


def fox_attention(u, cum, tq, tk):
    b, s, _ = u.shape
    cq, ck, cv = COL["fq"], COL["fk"], COL["fv"]
    return pl.pallas_call(
        functools.partial(_fox_kernel, tq=tq, tk=tk),
        grid=(b, N_HEADS, s // tq),
        in_specs=[
            pl.BlockSpec((1, tq, LANES), lambda b_, h, i: (b_, i, cq + h)),
            pl.BlockSpec((1, s, LANES), lambda b_, h, i: (b_, 0, ck + h)),
            pl.BlockSpec((1, s, LANES), lambda b_, h, i: (b_, 0, cv + h)),
            pl.BlockSpec((1, N_HEADS, s), lambda b_, h, i: (b_, 0, 0)),
        ],
        out_specs=pl.BlockSpec((1, tq, LANES), lambda b_, h, i: (b_, i, h)),
        out_shape=jax.ShapeDtypeStruct((b, s, BRANCH_WIDTH), BF16),
        scratch_shapes=[pltpu.VMEM((tq, 1), F32), pltpu.VMEM((tq, 1), F32), pltpu.VMEM((tq, LANES), F32)],
        compiler_params=_cparams(("parallel", "parallel", "arbitrary"), 40),
        name="fox_attention",
    )(u, u, u, cum)


def _sb_kernel(q_ref, k_ref, v_ref, o_ref, carry_ref, acc_ref, *, tq, tk):
    qi = pl.program_id(2)
    q0 = qi * tq
    scale = HEAD_DIM ** -0.5
    q = q_ref[0].astype(BF16)
    carry_ref[...] = jnp.zeros(carry_ref.shape, F32)
    acc_ref[...] = jnp.zeros(acc_ref.shape, F32)
    upper = (lax.broadcasted_iota(jnp.int32, (tk, tk), 0) > lax.broadcasted_iota(jnp.int32, (tk, tk), 1)).astype(BF16)
    n_chunks = (q0 + tq + tk - 1) // tk

    def body(jj, carry):
        c0 = pl.multiple_of((n_chunks - 1 - jj) * tk, tk)
        k = k_ref[0, pl.ds(c0, tk), :].astype(BF16)
        v = v_ref[0, pl.ds(c0, tk), :].astype(BF16)
        z = _dot_nt(q, k) * scale
        row = q0 + lax.broadcasted_iota(jnp.int32, (tq, tk), 0)
        col = c0 + lax.broadcasted_iota(jnp.int32, (tq, tk), 1)
        mask = col < row
        ls = _log_sigmoid(z)
        l_neg = jnp.where(mask, ls - z, 0.0)
        hi = l_neg.astype(BF16)
        lo = (l_neg - hi.astype(F32)).astype(BF16)
        after = carry_ref[...] + _dot(hi, upper) + _dot(lo, upper)
        a = jnp.where(mask, jnp.exp(ls + after), 0.0)
        acc_ref[...] += _dot(a.astype(BF16), v)
        carry_ref[...] += jnp.sum(l_neg, axis=1, keepdims=True)
        return carry

    lax.fori_loop(0, n_chunks, body, 0)
    o_ref[0] = acc_ref[...].astype(o_ref.dtype)


def sb_attention(u, tq, tk):
    b, s, _ = u.shape
    cq, ck, cv = COL["sq"], COL["sk"], COL["sv"]
    return pl.pallas_call(
        functools.partial(_sb_kernel, tq=tq, tk=tk),
        grid=(b, N_HEADS, s // tq),
        in_specs=[
            pl.BlockSpec((1, tq, LANES), lambda b_, h, i: (b_, i, cq + h)),
            pl.BlockSpec((1, s, LANES), lambda b_, h, i: (b_, 0, ck + h)),
            pl.BlockSpec((1, s, LANES), lambda b_, h, i: (b_, 0, cv + h)),
        ],
        out_specs=pl.BlockSpec((1, tq, LANES), lambda b_, h, i: (b_, i, h)),
        out_shape=jax.ShapeDtypeStruct((b, s, BRANCH_WIDTH), BF16),
        scratch_shapes=[pltpu.VMEM((tq, 1), F32), pltpu.VMEM((tq, LANES), F32)],
        compiler_params=_cparams(("parallel", "parallel", "arbitrary"), 40),
        name="sb_attention",
    )(u, u, u)


def _diff_kernel(q1_ref, q2_ref, k1_ref, k2_ref, v_ref, cq_ref, saq_ref, sbq_ref, ck_ref, sak_ref, sbk_ref,
                 lq1_ref, lk1_ref, lq2_ref, lk2_ref, g_ref, o_ref,
                 m1_ref, l1_ref, a1_ref, m2_ref, l2_ref, a2_ref, *, tq, tk, lam_init):
    h = pl.program_id(1)
    qi = pl.program_id(2)
    q0 = qi * tq
    scale = DIFF_DIM ** -0.5
    lane = lax.broadcasted_iota(jnp.int32, (tq, LANES), 1)
    mine = (lane // DIFF_DIM) == (h % 2)
    cq, saq, sbq = cq_ref[...], saq_ref[...], sbq_ref[...]
    q1 = jnp.where(mine, _rope64(q1_ref[0], cq, saq, sbq), 0.0).astype(BF16)
    q2 = jnp.where(mine, _rope64(q2_ref[0], cq, saq, sbq), 0.0).astype(BF16)
    for m_r, l_r, a_r in ((m1_ref, l1_ref, a1_ref), (m2_ref, l2_ref, a2_ref)):
        m_r[...] = jnp.full(m_r.shape, NEG_INF, F32)
        l_r[...] = jnp.zeros(l_r.shape, F32)
        a_r[...] = jnp.zeros(a_r.shape, F32)

    def body(j, carry):
        c0 = pl.multiple_of(j * tk, tk)
        ck, sak, sbk = ck_ref[pl.ds(c0, tk), :], sak_ref[pl.ds(c0, tk), :], sbk_ref[pl.ds(c0, tk), :]
        k1 = _rope64(k1_ref[0, pl.ds(c0, tk), :], ck, sak, sbk).astype(BF16)
        k2 = _rope64(k2_ref[0, pl.ds(c0, tk), :], ck, sak, sbk).astype(BF16)
        v = v_ref[0, pl.ds(c0, tk), :].astype(BF16)
        row = q0 + lax.broadcasted_iota(jnp.int32, (tq, tk), 0)
        col = c0 + lax.broadcasted_iota(jnp.int32, (tq, tk), 1)
        mask = col <= row
        s1 = jnp.where(mask, _dot_nt(q1, k1) * scale, NEG_INF)
        _softmax_step(s1, v, m1_ref, l1_ref, a1_ref)
        s2 = jnp.where(mask, _dot_nt(q2, k2) * scale, NEG_INF)
        _softmax_step(s2, v, m2_ref, l2_ref, a2_ref)
        return carry

    lax.fori_loop(0, (q0 + tq + tk - 1) // tk, body, 0)
    lam = (jnp.exp(jnp.sum(lq1_ref[...] * lk1_ref[...], axis=1, keepdims=True))
           - jnp.exp(jnp.sum(lq2_ref[...] * lk2_ref[...], axis=1, keepdims=True)) + lam_init)
    o = a1_ref[...] / jnp.maximum(l1_ref[...], 1e-30) - lam * (a2_ref[...] / jnp.maximum(l2_ref[...], 1e-30))
    y = o * lax.rsqrt(jnp.mean(o * o, axis=-1, keepdims=True) + NORM_EPS)
    o_ref[0] = ((y * g_ref[...]) * (1.0 - lam_init)).astype(o_ref.dtype)


def diff_attention(u, rope64_tabs, lq1, lk1, lq2, lk2, g, lam_init, tq, tk):
    b, s, _ = u.shape
    c64, sa64, sb64 = rope64_tabs
    cq1, cq2, ck1, ck2, cv = COL["dq1"], COL["dq2"], COL["dk1"], COL["dk2"], COL["dv"]
    qtab = pl.BlockSpec((tq, LANES), lambda b_, h, i: (i, 0))
    ktab = pl.BlockSpec((s, LANES), lambda b_, h, i: (0, 0))
    vec64 = pl.BlockSpec((1, DIFF_DIM), lambda b_, h, i: (0, 0))
    return pl.pallas_call(
        functools.partial(_diff_kernel, tq=tq, tk=tk, lam_init=lam_init),
        grid=(b, N_HEADS, s // tq),
        in_specs=[
            pl.BlockSpec((1, tq, LANES), lambda b_, h, i: (b_, i, cq1 + h // 2)),
            pl.BlockSpec((1, tq, LANES), lambda b_, h, i: (b_, i, cq2 + h // 2)),
            pl.BlockSpec((1, s, LANES), lambda b_, h, i: (b_, 0, ck1 + h // 2)),
            pl.BlockSpec((1, s, LANES), lambda b_, h, i: (b_, 0, ck2 + h // 2)),
            pl.BlockSpec((1, s, LANES), lambda b_, h, i: (b_, 0, cv + h)),
            qtab, qtab, qtab, ktab, ktab, ktab, vec64, vec64, vec64, vec64,
            pl.BlockSpec((1, HEAD_DIM), lambda b_, h, i: (0, 0)),
        ],
        out_specs=pl.BlockSpec((1, tq, LANES), lambda b_, h, i: (b_, i, h)),
        out_shape=jax.ShapeDtypeStruct((b, s, BRANCH_WIDTH), BF16),
        scratch_shapes=[pltpu.VMEM((tq, 1), F32), pltpu.VMEM((tq, 1), F32), pltpu.VMEM((tq, LANES), F32),
                        pltpu.VMEM((tq, 1), F32), pltpu.VMEM((tq, 1), F32), pltpu.VMEM((tq, LANES), F32)],
        compiler_params=_cparams(("parallel", "parallel", "arbitrary"), 40),
        name="diff_attention",
    )(u, u, u, u, u, c64, sa64, sb64, c64, sa64, sb64,
      lq1.reshape(1, DIFF_DIM), lk1.reshape(1, DIFF_DIM), lq2.reshape(1, DIFF_DIM), lk2.reshape(1, DIFF_DIM),
      g.reshape(1, HEAD_DIM))


def _compress_kernel(kt_ref, vt_ref, c_ref, s_ref, pek_ref, pev_ref, wk_ref, wv_ref, kc_ref, vc_ref,
                     xs_ref, xa_ref, xb_ref, *, n_blk):
    half = CMP_BLOCK // 2

    def compress(x, pe_ref, w_ref):
        xs_ref[...] = x
        for r in range(half):
            piece = xs_ref[pl.ds(r, n_blk, stride=half), :]
            xa_ref[:, r * LANES:(r + 1) * LANES] = (piece + pe_ref[pl.ds(r, 1), :]).astype(BF16)
            xb_ref[:, r * LANES:(r + 1) * LANES] = (piece + pe_ref[pl.ds(half + r, 1), :]).astype(BF16)
        first = _dot(xa_ref[...], w_ref[0])
        second = _dot(xb_ref[...], w_ref[1])
        return first + pltpu.roll(second, n_blk - 1, axis=0)

    for g in range(N_KV_NSA):
        sl = slice(g * LANES, (g + 1) * LANES)
        kc_ref[0, g] = compress(_rope128(kt_ref[0][:, sl], c_ref[...], s_ref[...]), pek_ref, wk_ref)
        vc_ref[0, g] = compress(vt_ref[0][:, sl], pev_ref, wv_ref)


def nsa_compress(u, rope128_tabs, pe_k, pe_v, w_ck, w_cv):
    b, s, _ = u.shape
    n_blk = s // CMP_STRIDE
    c128, s128 = rope128_tabs
    half = CMP_BLOCK // 2
    wk = w_ck.reshape(2, half * HEAD_DIM, HEAD_DIM).astype(BF16)
    wv = w_cv.reshape(2, half * HEAD_DIM, HEAD_DIM).astype(BF16)
    tab = pl.BlockSpec((s, LANES), lambda i: (0, 0))
    pe = pl.BlockSpec((CMP_BLOCK, HEAD_DIM), lambda i: (0, 0))
    wspec = pl.BlockSpec((2, half * HEAD_DIM, HEAD_DIM), lambda i: (0, 0, 0))
    out = pl.BlockSpec((1, N_KV_NSA, n_blk, HEAD_DIM), lambda i: (i, 0, 0, 0))
    return pl.pallas_call(
        functools.partial(_compress_kernel, n_blk=n_blk),
        grid=(b,),
        in_specs=[pl.BlockSpec((1, s, 2 * LANES), lambda i: (i, 0, COL["nkc"] // 2)),
                  pl.BlockSpec((1, s, 2 * LANES), lambda i: (i, 0, COL["nvc"] // 2)),
                  tab, tab, pe, pe, wspec, wspec],
        out_specs=[out, out],
        out_shape=[jax.ShapeDtypeStruct((b, N_KV_NSA, n_blk, HEAD_DIM), F32)] * 2,
        scratch_shapes=[pltpu.VMEM((s, LANES), F32), pltpu.VMEM((n_blk, half * LANES), BF16),
                        pltpu.VMEM((n_blk, half * LANES), BF16)],
        compiler_params=_cparams(("parallel",), 40),
        name="nsa_compress",
    )(u, u, c128, s128, pe_k, pe_v, wk, wv)


def _cmp_attn_kernel(q_ref, kc_ref, vc_ref, c_ref, s_ref, ov_ref, oc_ref, mt_ref, *, tq, n_blk, n_sel):
    qi = pl.program_id(1)
    q0 = qi * tq
    scale = HEAD_DIM ** -0.5
    hpg = N_HEADS // N_KV_NSA
    t_row = q0 + lax.broadcasted_iota(jnp.int32, (tq, n_blk), 0)
    n_col = lax.broadcasted_iota(jnp.int32, (tq, n_blk), 1)
    valid = n_col * CMP_STRIDE + (CMP_BLOCK - 1) <= t_row
    t_lane = q0 + lax.broadcasted_iota(jnp.int32, (n_blk, tq), 1)
    n_sub = lax.broadcasted_iota(jnp.int32, (n_blk, tq), 0)
    valid_t = n_sub * CMP_STRIDE + (CMP_BLOCK - 1) <= t_lane
    j_idx = lax.broadcasted_iota(jnp.int32, (n_sel, tq), 0)
    cur = (q0 + lax.broadcasted_iota(jnp.int32, (n_sel, tq), 1)) // SEL_BLOCK
    for g in range(N_KV_NSA):
        kc = kc_ref[0, g].astype(BF16)
        vc = vc_ref[0, g].astype(BF16)
        p_sum_t = jnp.zeros((n_blk, tq), F32)
        for hh in range(hpg):
            sl = slice((g * hpg + hh) * LANES, (g * hpg + hh + 1) * LANES)
            q = _rope128(q_ref[0][:, sl], c_ref[...], s_ref[...]).astype(BF16)
            s = jnp.where(valid, _dot_nt(q, kc) * scale, NEG_INF)
            m = jnp.max(s, axis=1, keepdims=True)
            m = jnp.where(m == NEG_INF, 0.0, m)
            e = jnp.exp(s - m)
            p = e / jnp.maximum(jnp.sum(e, axis=1, keepdims=True), 1e-30)
            oc_ref[0, :, sl] = _dot(p.astype(BF16), vc)
            st = jnp.where(valid_t, _dot_nt(kc, q) * scale, NEG_INF)
            mt = jnp.max(st, axis=0, keepdims=True)
            mt = jnp.where(mt == NEG_INF, 0.0, mt)
            et = jnp.exp(st - mt)
            p_sum_t = p_sum_t + et / jnp.maximum(jnp.sum(et, axis=0, keepdims=True), 1e-30)
        hi = p_sum_t.astype(BF16)
        lo = (p_sum_t - hi.astype(F32)).astype(BF16)
        imp = _dot(ov_ref[...], hi) + _dot(ov_ref[...], lo)
        imp = jnp.where((j_idx == 0) | (j_idx == cur) | (j_idx == cur - 1), FORCE_SCORE, imp)
        imp = jnp.where(j_idx <= cur, imp, NEG_INF)
        rank = jnp.zeros((n_sel, tq), F32)
        for jp in range(n_sel):
            other = imp[jp:jp + 1, :]
            beats = (other > imp) | ((other == imp) & (j_idx > jp))
            rank = rank + beats.astype(F32)
        mt_ref[0, g] = (rank < float(min(SEL_TOP_N, n_sel))).astype(F32)


def nsa_cmp_attention(u, kc, vc, rope128_tabs, tq):
    b, s, _ = u.shape
    n_blk = s // CMP_STRIDE
    n_sel = s // SEL_BLOCK
    c128, s128 = rope128_tabs
    ci = jnp.arange(n_blk)[None, :] * CMP_STRIDE
    sj = jnp.arange(n_sel)[:, None] * SEL_BLOCK
    overlap_t = ((ci < sj + SEL_BLOCK) & (ci + CMP_BLOCK > sj)).astype(BF16)
    cqc = COL["nq"] // N_HEADS
    tab = pl.BlockSpec((tq, LANES), lambda b_, i: (i, 0))
    cblk = pl.BlockSpec((1, N_KV_NSA, n_blk, HEAD_DIM), lambda b_, i: (b_, 0, 0, 0))
    return pl.pallas_call(
        functools.partial(_cmp_attn_kernel, tq=tq, n_blk=n_blk, n_sel=n_sel),
        grid=(b, s // tq),
        in_specs=[pl.BlockSpec((1, tq, BRANCH_WIDTH), lambda b_, i: (b_, i, cqc)), cblk, cblk, tab, tab,
                  pl.BlockSpec((n_sel, n_blk), lambda b_, i: (0, 0))],
        out_specs=[pl.BlockSpec((1, tq, BRANCH_WIDTH), lambda b_, i: (b_, i, 0)),
                   pl.BlockSpec((1, N_KV_NSA, n_sel, tq), lambda b_, i: (b_, 0, 0, i))],
        out_shape=[jax.ShapeDtypeStruct((b, s, BRANCH_WIDTH), F32),
                   jax.ShapeDtypeStruct((b, N_KV_NSA, n_sel, s), F32)],
        compiler_params=_cparams(("parallel", "parallel"), 40),
        name="nsa_cmp_attention",
    )(u, kc, vc, c128, s128, overlap_t)


def _sel_kernel(q_ref, k_ref, vt_ref, mt_ref, cq_ref, sq_ref, ck_ref, sk_ref, o_ref,
                m_ref, l_ref, acc_ref, *, tq, tk):
    qi = pl.program_id(1)
    q0 = qi * tq
    scale = HEAD_DIM ** -0.5
    hpg = N_HEADS // N_KV_NSA
    per_chunk = tk // SEL_BLOCK
    n_chunks = (q0 + tq + tk - 1) // tk
    for g in range(N_KV_NSA):
        qs = []
        for hh in range(hpg):
            sl = slice((g * hpg + hh) * LANES, (g * hpg + hh + 1) * LANES)
            qs.append(_rope128(q_ref[0][:, sl], cq_ref[...], sq_ref[...]).astype(BF16))
        m_ref[...] = jnp.full(m_ref.shape, NEG_INF, F32)
        l_ref[...] = jnp.zeros(l_ref.shape, F32)
        acc_ref[...] = jnp.zeros(acc_ref.shape, F32)
        gl = slice(g * LANES, (g + 1) * LANES)

        def body(j, carry, g=g, qs=qs, gl=gl):
            c0 = pl.multiple_of(j * tk, tk)
            k = _rope128(k_ref[0, pl.ds(c0, tk), gl], ck_ref[pl.ds(c0, tk), :], sk_ref[pl.ds(c0, tk), :]).astype(BF16)
            vt = vt_ref[0, gl, pl.ds(c0, tk)].astype(BF16)
            rows = [jnp.broadcast_to(mt_ref[0, g, pl.ds(j * per_chunk + i, 1), :], (SEL_BLOCK, tq))
                    for i in range(per_chunk)]
            chosen = jnp.concatenate(rows, axis=0) > 0.5
            key_pos = c0 + lax.broadcasted_iota(jnp.int32, (tk, tq), 0)
            t_pos = q0 + lax.broadcasted_iota(jnp.int32, (tk, tq), 1)
            mask = chosen & (key_pos <= t_pos)
            for hh in range(hpg):
                st = jnp.where(mask, _dot_nt(k, qs[hh]) * scale, NEG_INF)
                m_prev = m_ref[hh]
                m_new = jnp.maximum(m_prev, jnp.max(st, axis=0, keepdims=True))
                m_safe = jnp.where(m_new == NEG_INF, 0.0, m_new)
                alpha = jnp.exp(m_prev - m_safe)
                p = jnp.exp(st - m_safe)
                l_ref[hh] = alpha * l_ref[hh] + jnp.sum(p, axis=0, keepdims=True)
                acc_ref[hh] = alpha * acc_ref[hh] + _dot(vt, p.astype(BF16))
                m_ref[hh] = m_new
            return carry

        lax.fori_loop(0, n_chunks, body, 0)
        for hh in range(hpg):
            sl = slice((g * hpg + hh) * LANES, (g * hpg + hh + 1) * LANES)
            o_t = acc_ref[hh] / jnp.maximum(l_ref[hh], 1e-30)
            o_ref[0, :, sl] = o_t.T


def nsa_sel_attention(u, vs_t, mt, rope128_tabs, tq, tk):
    b, s, _ = u.shape
    n_sel = s // SEL_BLOCK
    c128, s128 = rope128_tabs
    hpg = N_HEADS // N_KV_NSA
    qtab = pl.BlockSpec((tq, LANES), lambda b_, i: (i, 0))
    ktab = pl.BlockSpec((s, LANES), lambda b_, i: (0, 0))
    return pl.pallas_call(
        functools.partial(_sel_kernel, tq=tq, tk=tk),
        grid=(b, s // tq),
        in_specs=[pl.BlockSpec((1, tq, BRANCH_WIDTH), lambda b_, i: (b_, i, COL["nq"] // N_HEADS)),
                  pl.BlockSpec((1, s, 2 * LANES), lambda b_, i: (b_, 0, COL["nks"] // 2)),
                  pl.BlockSpec((1, 2 * LANES, s), lambda b_, i: (b_, 0, 0)),
                  pl.BlockSpec((1, N_KV_NSA, n_sel, tq), lambda b_, i: (b_, 0, 0, i)),
                  qtab, qtab, ktab, ktab],
        out_specs=pl.BlockSpec((1, tq, BRANCH_WIDTH), lambda b_, i: (b_, i, 0)),
        out_shape=jax.ShapeDtypeStruct((b, s, BRANCH_WIDTH), F32),
        scratch_shapes=[pltpu.VMEM((hpg, 1, tq), F32), pltpu.VMEM((hpg, 1, tq), F32),
                        pltpu.VMEM((hpg, HEAD_DIM, tq), F32)],
        compiler_params=_cparams(("parallel", "arbitrary"), 40),
        name="nsa_sel_attention",
    )(u, u, vs_t, mt, c128, s128, c128, s128)


def _win_kernel(q_ref, k_ref, v_ref, oc_ref, os_ref, small_ref, cq_ref, sq_ref, ck_ref, sk_ref, o_ref,
                m_ref, l_ref, acc_ref, *, tq, tk):
    qi = pl.program_id(1)
    q0 = qi * tq
    scale = HEAD_DIM ** -0.5
    hpg = N_HEADS // N_KV_NSA
    first = jnp.maximum(q0 - (WINDOW - 1), 0) // tk
    last = (q0 + tq - 1) // tk
    gates = jax.nn.sigmoid(small_ref[0][:, 0:LANES])
    for g in range(N_KV_NSA):
        qs = []
        for hh in range(hpg):
            sl = slice((g * hpg + hh) * LANES, (g * hpg + hh + 1) * LANES)
            qs.append(_rope128(q_ref[0][:, sl], cq_ref[...], sq_ref[...]).astype(BF16))
        m_ref[...] = jnp.full(m_ref.shape, NEG_INF, F32)
        l_ref[...] = jnp.zeros(l_ref.shape, F32)
        acc_ref[...] = jnp.zeros(acc_ref.shape, F32)
        gl = slice(g * LANES, (g + 1) * LANES)

        def body(j, carry, qs=qs, gl=gl):
            c0 = pl.multiple_of(j * tk, tk)
            k = _rope128(k_ref[0, pl.ds(c0, tk), gl], ck_ref[pl.ds(c0, tk), :], sk_ref[pl.ds(c0, tk), :]).astype(BF16)
            v = v_ref[0, pl.ds(c0, tk), gl].astype(BF16)
            gap = (q0 + lax.broadcasted_iota(jnp.int32, (tq, tk), 0)) - (c0 + lax.broadcasted_iota(jnp.int32, (tq, tk), 1))
            mask = (gap >= 0) & (gap < WINDOW)
            for hh in range(hpg):
                s = jnp.where(mask, _dot_nt(qs[hh], k) * scale, NEG_INF)
                _softmax_step(s, v, m_ref.at[hh], l_ref.at[hh], acc_ref.at[hh])
            return carry

        lax.fori_loop(first, last + 1, body, 0)
        for hh in range(hpg):
            head = g * hpg + hh
            sl = slice(head * LANES, (head + 1) * LANES)
            o_w = acc_ref[hh] / jnp.maximum(l_ref[hh], 1e-30)
            c = 4 + 3 * head
            o = gates[:, c:c + 1] * oc_ref[0][:, sl] + gates[:, c + 1:c + 2] * os_ref[0][:, sl] + gates[:, c + 2:c + 3] * o_w
            o_ref[0, :, sl] = o.astype(o_ref.dtype)


def nsa_win_attention(u, o_c, o_s, rope128_tabs, tq, tk):
    b, s, _ = u.shape
    c128, s128 = rope128_tabs
    hpg = N_HEADS // N_KV_NSA
    qtab = pl.BlockSpec((tq, LANES), lambda b_, i: (i, 0))
    ktab = pl.BlockSpec((s, LANES), lambda b_, i: (0, 0))
    wide = pl.BlockSpec((1, tq, BRANCH_WIDTH), lambda b_, i: (b_, i, 0))
    return pl.pallas_call(
        functools.partial(_win_kernel, tq=tq, tk=tk),
        grid=(b, s // tq),
        in_specs=[pl.BlockSpec((1, tq, BRANCH_WIDTH), lambda b_, i: (b_, i, COL["nq"] // N_HEADS)),
                  pl.BlockSpec((1, s, 2 * LANES), lambda b_, i: (b_, 0, COL["nkw"] // 2)),
                  pl.BlockSpec((1, s, 2 * LANES), lambda b_, i: (b_, 0, COL["nvw"] // 2)),
                  wide, wide,
                  pl.BlockSpec((1, tq, 2 * LANES), lambda b_, i: (b_, i, COL["small"] // 2)),
                  qtab, qtab, ktab, ktab],
        out_specs=wide,
        out_shape=jax.ShapeDtypeStruct((b, s, BRANCH_WIDTH), BF16),
        scratch_shapes=[pltpu.VMEM((hpg, tq, 1), F32), pltpu.VMEM((hpg, tq, 1), F32),
                        pltpu.VMEM((hpg, tq, HEAD_DIM), F32)],
        compiler_params=_cparams(("parallel", "arbitrary"), 48),
        name="nsa_win_attention",
    )(u, u, u, o_c, o_s, u, c128, s128, c128, s128)


def _merge_kernel(h_ref, o0_ref, o1_ref, o2_ref, o3_ref, wg_ref, wb_ref, out_ref):
    h = h_ref[...]
    acc = None
    for n, o_ref in enumerate((o0_ref, o1_ref, o2_ref, o3_ref)):
        term = jax.nn.sigmoid(_dot(h, wg_ref[n])) * _dot(o_ref[...], wb_ref[n])
        acc = term if acc is None else acc + term
    out_ref[...] = acc.astype(out_ref.dtype)


def merge_branches(h, branches, wg, wb, tm, tn):
    n, d = h.shape
    bspec = pl.BlockSpec((tm, BRANCH_WIDTH), lambda j, i: (i, 0))
    return pl.pallas_call(
        _merge_kernel,
        grid=(d // tn, n // tm),
        in_specs=[pl.BlockSpec((tm, d), lambda j, i: (i, 0)), bspec, bspec, bspec, bspec,
                  pl.BlockSpec((4, d, tn), lambda j, i: (0, 0, j)),
                  pl.BlockSpec((4, BRANCH_WIDTH, tn), lambda j, i: (0, 0, j))],
        out_specs=pl.BlockSpec((tm, tn), lambda j, i: (i, j)),
        out_shape=jax.ShapeDtypeStruct((n, d), BF16),
        compiler_params=_cparams(("parallel", "arbitrary"), 56),
        name="merge_branches",
    )(h, *branches, wg, wb)


def _out_kernel(mg_ref, x_ref, w_ref, g_ref, wr_ref, br_ref, x1_ref, h2_ref, route_ref, *, tm):
    x1 = x_ref[...] + _dot(mg_ref[...], w_ref[...])
    x1_ref[...] = x1
    h2 = (x1 * lax.rsqrt(jnp.mean(x1 * x1, axis=-1, keepdims=True) + NORM_EPS)) * g_ref[...]
    for s in range(ROW_TILES):
        h2_ref[pl.ds(s, tm, stride=ROW_TILES), :] = h2[:, s * LANES:(s + 1) * LANES]
    logits = _dot(h2.astype(BF16), wr_ref[...]) + br_ref[...]
    lane = lax.broadcasted_iota(jnp.int32, logits.shape, 1)
    lane_f = lane.astype(F32)
    big = float(LANES)
    is_g = lane < N_GROUPS
    lg = jnp.where(is_g, logits, NEG_INF)
    mx = jnp.max(lg, axis=1, keepdims=True)
    gi = jnp.min(jnp.where(lg == mx, lane_f, big), axis=1, keepdims=True)
    pg = 1.0 / jnp.sum(jnp.where(is_g, jnp.exp(lg - mx), 0.0), axis=1, keepdims=True)
    e_idx = lane - N_GROUPS
    in_grp = (lane >= N_GROUPS) & (lane < N_GROUPS + N_EXPERTS) & ((e_idx // EXPERTS_PER_GROUP).astype(F32) == gi)
    le = jnp.where(in_grp, logits, NEG_INF)
    v1 = jnp.max(le, axis=1, keepdims=True)
    i1 = jnp.min(jnp.where(le == v1, lane_f, big), axis=1, keepdims=True)
    le2 = jnp.where(lane_f == i1, NEG_INF, le)
    v2 = jnp.max(le2, axis=1, keepdims=True)
    i2 = jnp.min(jnp.where(le2 == v2, lane_f, big), axis=1, keepdims=True)
    e2 = jnp.exp(v2 - v1)
    w1 = pg / (1.0 + e2)
    w2 = pg * e2 / (1.0 + e2)
    route = jnp.where(lane == 0, i1 - N_GROUPS, 0.0)
    route = jnp.where(lane == 1, i2 - N_GROUPS, route)
    route = jnp.where(lane == 2, w1, route)
    route = jnp.where(lane == 3, w2, route)
    route_ref[...] = route


def out_proj_router(merged, x, w_out, g2, w_router, b_router, tm):
    n, d = x.shape
    row = pl.BlockSpec((tm, d), lambda i: (i, 0))
    return pl.pallas_call(
        functools.partial(_out_kernel, tm=tm),
        grid=(n // tm,),
        in_specs=[row, row, pl.BlockSpec((d, d), lambda i: (0, 0)), pl.BlockSpec((1, d), lambda i: (0, 0)),
                  pl.BlockSpec((d, LANES), lambda i: (0, 0)), pl.BlockSpec((1, LANES), lambda i: (0, 0))],
        out_specs=[row, pl.BlockSpec((tm * ROW_TILES, LANES), lambda i: (i, 0)),
                   pl.BlockSpec((tm, LANES), lambda i: (i, 0))],
        out_shape=[jax.ShapeDtypeStruct((n, d), F32), jax.ShapeDtypeStruct((n * ROW_TILES, LANES), F32),
                   jax.ShapeDtypeStruct((n, LANES), F32)],
        compiler_params=_cparams(("parallel",), 56),
        name="out_proj_router",
    )(merged, x, w_out, g2.reshape(1, d), w_router, b_router)


def _row_copy(src_hbm, src_row, dst_vmem, dst_row, sem):
    return pltpu.make_async_copy(
        src_hbm.at[pl.ds(pl.multiple_of(src_row * ROW_TILES, ROW_TILES), ROW_TILES), :],
        dst_vmem.at[pl.ds(pl.multiple_of(dst_row * ROW_TILES, ROW_TILES), ROW_TILES), :],
        sem)


def _ffn_kernel(blk_e_ref, n_used_ref, tok_ref, sw_ref, h2_hbm, wg_ref, wu_ref, wd_ref, y_ref, xg_ref, sem, *, tm):
    blk = pl.program_id(0)

    @pl.when(blk < n_used_ref[0])
    def _():
        def issue(r, carry):
            _row_copy(h2_hbm, tok_ref[0, 0, r], xg_ref, r, sem).start()
            return carry

        lax.fori_loop(0, tm, issue, 0)
        pltpu.make_async_copy(h2_hbm.at[pl.ds(0, tm * ROW_TILES), :], xg_ref, sem).wait()
        x = jnp.concatenate([xg_ref[pl.ds(s, tm, stride=ROW_TILES), :] for s in range(ROW_TILES)], axis=1).astype(BF16)
        gate = _dot(x, wg_ref[0])
        up = _dot(x, wu_ref[0])
        hid = (gate * jax.nn.sigmoid(gate) * up).astype(BF16)
        y = _dot(hid, wd_ref[0]) * sw_ref[...]
        for s in range(ROW_TILES):
            y_ref[pl.ds(s, tm, stride=ROW_TILES), :] = y[:, s * LANES:(s + 1) * LANES]

    @pl.when(blk >= n_used_ref[0])
    def _():
        y_ref[...] = jnp.zeros(y_ref.shape, F32)


def moe_experts(blk_e, n_used, buf_tok, buf_w, h2_rows, w_g, w_u, w_d, tm):
    n_blk = blk_e.shape[0]
    d, ff = w_g.shape[1], w_g.shape[2]
    grid_spec = pltpu.PrefetchScalarGridSpec(
        num_scalar_prefetch=2,
        grid=(n_blk,),
        in_specs=[
            pl.BlockSpec((1, 1, tm), lambda i, be, nu: (i, 0, 0), memory_space=pltpu.SMEM),
            pl.BlockSpec((tm, 1), lambda i, be, nu: (i, 0)),
            pl.BlockSpec(memory_space=pl.ANY),
            pl.BlockSpec((1, d, ff), lambda i, be, nu: (be[i], 0, 0)),
            pl.BlockSpec((1, d, ff), lambda i, be, nu: (be[i], 0, 0)),
            pl.BlockSpec((1, ff, d), lambda i, be, nu: (be[i], 0, 0)),
        ],
        out_specs=pl.BlockSpec((tm * ROW_TILES, LANES), lambda i, be, nu: (i, 0)),
        scratch_shapes=[pltpu.VMEM((tm * ROW_TILES, LANES), F32), pltpu.SemaphoreType.DMA(())],
    )
    return pl.pallas_call(
        functools.partial(_ffn_kernel, tm=tm),
        grid_spec=grid_spec,
        out_shape=jax.ShapeDtypeStruct((n_blk * tm * ROW_TILES, LANES), F32),
        compiler_params=_cparams(("arbitrary",), 56),
        name="moe_experts",
    )(blk_e, n_used, buf_tok.reshape(n_blk, 1, tm), buf_w.reshape(n_blk * tm, 1), h2_rows, w_g, w_u, w_d)


def _combine_kernel(pos_ref, x1_ref, yb_hbm, g_ref, x2_ref, hn_ref, buf_ref, sem, *, tm):
    def issue(r, carry):
        _row_copy(yb_hbm, pos_ref[0, 0, r], buf_ref, r, sem).start()
        return carry

    lax.fori_loop(0, 2 * tm, issue, 0)
    pltpu.make_async_copy(yb_hbm.at[pl.ds(0, 2 * tm * ROW_TILES), :], buf_ref, sem).wait()
    pieces = []
    ssq = jnp.zeros((tm, 1), F32)
    for s in range(ROW_TILES):
        piece = (x1_ref[:, s * LANES:(s + 1) * LANES]
                 + buf_ref[pl.ds(s, tm, stride=2 * ROW_TILES), :]
                 + buf_ref[pl.ds(ROW_TILES + s, tm, stride=2 * ROW_TILES), :])
        x2_ref[:, s * LANES:(s + 1) * LANES] = piece
        ssq = ssq + jnp.sum(piece * piece, axis=1, keepdims=True)
        pieces.append(piece)
    inv = lax.rsqrt(ssq / D_MODEL + NORM_EPS)
    for s in range(ROW_TILES):
        sl = slice(s * LANES, (s + 1) * LANES)
        hn_ref[:, sl] = ((pieces[s] * inv) * g_ref[:, sl]).astype(hn_ref.dtype)


def moe_combine(pos, x1, yb_rows, g, hn_dtype, tm):
    n, d = x1.shape
    row = pl.BlockSpec((tm, d), lambda i: (i, 0))
    return pl.pallas_call(
        functools.partial(_combine_kernel, tm=tm),
        grid=(n // tm,),
        in_specs=[pl.BlockSpec((1, 1, 2 * tm), lambda i: (i, 0, 0), memory_space=pltpu.SMEM), row,
                  pl.BlockSpec(memory_space=pl.ANY), pl.BlockSpec((1, d), lambda i: (0, 0))],
        out_specs=[row, row],
        out_shape=[jax.ShapeDtypeStruct((n, d), F32), jax.ShapeDtypeStruct((n, d), hn_dtype)],
        scratch_shapes=[pltpu.VMEM((2 * tm * ROW_TILES, LANES), F32), pltpu.SemaphoreType.DMA(())],
        compiler_params=_cparams(("arbitrary",), 48),
        name="moe_combine",
    )(pos.reshape(n // tm, 1, 2 * tm), x1, yb_rows, g.reshape(1, d))


def _dispatch_plan(route, tm):
    n = route.shape[0]
    m = 2 * n
    slot_e = route[:, 0:2].astype(jnp.int32).reshape(m)
    slot_w = route[:, 2:4].reshape(m)
    slot_tok = jnp.repeat(jnp.arange(n, dtype=jnp.int32), 2)
    order = jnp.argsort(slot_e)
    se, stok, sw = slot_e[order], slot_tok[order], slot_w[order]
    counts = jnp.zeros((N_EXPERTS,), jnp.int32).at[slot_e].add(1)
    start = jnp.cumsum(counts) - counts
    padded = (counts + tm - 1) // tm * tm
    pend = jnp.cumsum(padded)
    pstart = pend - padded
    dest = pstart[se] + jnp.arange(m, dtype=jnp.int32) - start[se]
    n_rows = m + N_EXPERTS * tm
    n_blk = n_rows // tm
    buf_tok = jnp.zeros((n_rows,), jnp.int32).at[dest].set(stok)
    buf_w = jnp.zeros((n_rows,), F32).at[dest].set(sw)
    blk_e = jnp.minimum(jnp.searchsorted(pend, jnp.arange(n_blk, dtype=jnp.int32) * tm, side="right"),
                        N_EXPERTS - 1).astype(jnp.int32)
    n_used = (pend[-1] // tm).astype(jnp.int32).reshape(1)
    pos = jnp.zeros((m,), jnp.int32).at[order].set(dest)
    return blk_e, n_used, buf_tok, buf_w, pos


def _rope_tables(s):
    pos = jnp.arange(s, dtype=F32)[:, None]
    inv128 = jnp.exp(-math.log(ROPE_THETA) * jnp.arange(0, HEAD_DIM, 2, dtype=F32) / HEAD_DIM)
    a = pos * inv128[None, :]
    c128 = jnp.concatenate([jnp.cos(a), jnp.cos(a)], axis=1)
    s128 = jnp.concatenate([-jnp.sin(a), jnp.sin(a)], axis=1)
    inv64 = jnp.exp(-math.log(ROPE_THETA) * jnp.arange(0, DIFF_DIM, 2, dtype=F32) / DIFF_DIM)
    a = pos * inv64[None, :]
    co, si, z = jnp.cos(a), jnp.sin(a), jnp.zeros_like(a)
    c64 = jnp.concatenate([co, co, co, co], axis=1)
    sa64 = jnp.concatenate([-si, z, -si, z], axis=1)
    sb64 = jnp.concatenate([z, si, z, si], axis=1)
    return (c128, s128), (c64, sa64, sb64)


def _permute_w_in(w_in):
    offs = {}
    o = 0
    for name, w in zip(_ORIG_NAMES, _ORIG_WIDTHS):
        offs[name] = (o, w)
        o += w
    cols = []
    for name, w in _PIECES:
        if name == "small":
            f0, fw = offs["ff"]
            g0, gw = offs["ngt"]
            cols += [w_in[:, f0:f0 + fw], w_in[:, g0:g0 + gw], jnp.zeros((w_in.shape[0], w - fw - gw), w_in.dtype)]
        else:
            o0, ow = offs[name]
            cols.append(w_in[:, o0:o0 + ow])
    return jnp.concatenate(cols, axis=1).astype(BF16)


def _tiles(s):
    return dict(tq=min(512, s), tk=min(512, s), tq_nsa=min(256, s), tk_sb=min(256, s))


def kernel(x, norm1_g, w_in, fox_bf, nsa_pe_k, nsa_pe_v, nsa_w_ck, nsa_w_cv, diff_lq1, diff_lk1, diff_lq2, diff_lk2, diff_norm_g, w_branch, w_mgate, w_out, norm2_g, w_rg, b_rg, w_re, b_re, w_eg, w_eu, w_ed, final_g):
    b, s, d = x.shape
    n = b * s
    depth = w_in.shape[0]
    t = _tiles(s)
    tm_rows = min(256, n)
    tm_moe = 256
    rope128_tabs, rope64_tabs = _rope_tables(s)
    xf = x.reshape(n, d)
    h = rms_norm_rows(xf, norm1_g[0], BF16, tm=min(512, n))
    out = None
    for l in range(depth):
        lam_init = 0.8 - 0.6 * math.exp(-0.3 * l)
        u = matmul(h, _permute_w_in(w_in[l]), F32, tm=min(1024, n), tn=768).reshape(b, s, C_TOT)
        f_t = u[:, :, SMALL_OFF:SMALL_OFF + N_HEADS].transpose(0, 2, 1)
        cum = forget_cumsum(f_t, fox_bf[l])
        o_fox = fox_attention(u, cum, t["tq"], t["tk"])
        o_sb = sb_attention(u, t["tq"], t["tk_sb"])
        o_diff = diff_attention(u, rope64_tabs, diff_lq1[l], diff_lk1[l], diff_lq2[l], diff_lk2[l],
                                diff_norm_g[l], lam_init, t["tq"], t["tk"])
        kc, vc = nsa_compress(u, rope128_tabs, nsa_pe_k[l], nsa_pe_v[l], nsa_w_ck[l], nsa_w_cv[l])
        o_c, mt = nsa_cmp_attention(u, kc, vc, rope128_tabs, t["tq_nsa"])
        nvs0 = COL["nvs"] * LANES
        vs_t = u[:, :, nvs0:nvs0 + 2 * LANES].transpose(0, 2, 1)
        o_s = nsa_sel_attention(u, vs_t, mt, rope128_tabs, t["tq_nsa"], t["tq_nsa"])
        o_nsa = nsa_win_attention(u, o_c, o_s, rope128_tabs, t["tq"], t["tk"])
        branches = [o.reshape(n, BRANCH_WIDTH) for o in (o_fox, o_nsa, o_sb, o_diff)]
        merged = merge_branches(h, branches, w_mgate[l].astype(BF16), w_branch[l].astype(BF16), tm=min(512, n), tn=512)
        w_router = jnp.concatenate([w_rg[l], w_re[l], jnp.zeros((d, LANES - N_GROUPS - N_EXPERTS), F32)], axis=1).astype(BF16)
        b_router = jnp.concatenate([b_rg[l], b_re[l], jnp.zeros((LANES - N_GROUPS - N_EXPERTS,), F32)]).reshape(1, LANES)
        x1, h2_rows, route = out_proj_router(merged, xf, w_out[l].astype(BF16), norm2_g[l], w_router, b_router, tm_rows)
        blk_e, n_used, buf_tok, buf_w, pos = _dispatch_plan(route, tm_moe)
        yb_rows = moe_experts(blk_e, n_used, buf_tok, buf_w, h2_rows,
                              w_eg[l].astype(BF16), w_eu[l].astype(BF16), w_ed[l].astype(BF16), tm_moe)
        last = l == depth - 1
        g_next = final_g if last else norm1_g[l + 1]
        xf, hn = moe_combine(pos, x1, yb_rows, g_next, F32 if last else BF16, tm_rows)
        h = hn
        out = hn
    return out.reshape(b, s, d)
```

```python
import functools
import math

import jax
import jax.numpy as jnp
from jax import lax
from jax.experimental import pallas as pl
from jax.experimental.pallas import tpu as pltpu

F32 = jnp.float32
BF16 = jnp.bfloat16
NEG_INF = float("-inf")
LOG2E = 1.4426950408889634

D_MODEL = 2048
HEAD_DIM = 128
DIFF_DIM = 64
N_HEADS = 4
N_KV_NSA = 2
BRANCH_WIDTH = 512
ROPE_THETA = 10000.0
NORM_EPS = 1e-6
CMP_BLOCK = 32
CMP_STRIDE = 16
SEL_BLOCK = 64
SEL_TOP_N = 16
WINDOW = 512
FORCE_SCORE = 1e6
N_GROUPS = 4
EXPERTS_PER_GROUP = 8
N_EXPERTS = 32
EXPERT_FF = 1024

LANES = 128
ROW_TILES = D_MODEL // LANES
MIB = 1024 * 1024

_PLAIN = (("fq", 512), ("fk", 512), ("fv", 512), ("nvc", 256), ("nvs", 256), ("nvw", 256),
          ("sq", 512), ("sk", 512), ("sv", 512), ("dv", 512))
_ROPE128 = (("nq", 512), ("nkc", 256), ("nks", 256), ("nkw", 256))
_ROPE64 = (("dq1", 256), ("dq2", 256), ("dk1", 256), ("dk2", 256))


def _layout(pieces):
    col, off = {}, 0
    for name, w in pieces:
        col[name] = off // LANES
        off += w
    return col, off


COL_P, C_PLAIN = _layout(_PLAIN)
COL_R, C_ROPE128 = _layout(_ROPE128)
COL_D, C_ROPE64 = _layout(_ROPE64)

_ORIG_WIDTHS = (512, 512, 512, 4, 512, 256, 256, 256, 256, 256, 256, 12, 512, 512, 512, 256, 256, 256, 256, 512)
_ORIG_NAMES = ("fq", "fk", "fv", "ff", "nq", "nkc", "nvc", "nks", "nvs", "nkw", "nvw", "ngt",
               "sq", "sk", "sv", "dq1", "dq2", "dk1", "dk2", "dv")


def _cparams(sem, vmem_mib):
    return pltpu.CompilerParams(dimension_semantics=sem, vmem_limit_bytes=vmem_mib * MIB)


def _log_sigmoid(z):
    return jnp.minimum(z, 0.0) - jnp.log1p(jnp.exp(-jnp.abs(z)))


def _dot_nt(a, b):
    return lax.dot_general(a, b, (((1,), (1,)), ((), ())), preferred_element_type=F32)


def _dot(a, b):
    return jnp.dot(a, b, preferred_element_type=F32)


def _lanes(x, width):
    return x if width == LANES else jnp.concatenate([x] * (width // LANES), axis=1)


def _norm_kernel(x_ref, g_ref, o_ref):
    x = x_ref[...]
    y = x * lax.rsqrt(jnp.mean(x * x, axis=-1, keepdims=True) + NORM_EPS)
    o_ref[...] = (y * g_ref[...]).astype(o_ref.dtype)


def rms_norm_rows(x, g, out_dtype, tm):
    n, d = x.shape
    return pl.pallas_call(
        _norm_kernel,
        grid=(n // tm,),
        in_specs=[pl.BlockSpec((tm, d), lambda i: (i, 0)), pl.BlockSpec((1, d), lambda i: (0, 0))],
        out_specs=pl.BlockSpec((tm, d), lambda i: (i, 0)),
        out_shape=jax.ShapeDtypeStruct((n, d), out_dtype),
        compiler_params=_cparams(("parallel",), 40),
        name="rms_norm",
    )(x, g.reshape(1, d))


def _mm_kernel(a_ref, w_ref, o_ref):
    o_ref[...] = _dot(a_ref[...], w_ref[...]).astype(o_ref.dtype)


def in_proj_plain(h, w, tm, tn):
    m, k = h.shape
    c = w.shape[1]
    return pl.pallas_call(
        _mm_kernel,
        grid=(c // tn, m // tm),
        in_specs=[pl.BlockSpec((tm, k), lambda j, i: (i, 0)), pl.BlockSpec((k, tn), lambda j, i: (0, j))],
        out_specs=pl.BlockSpec((tm, tn), lambda j, i: (i, j)),
        out_shape=jax.ShapeDtypeStruct((m, c), BF16),
        compiler_params=_cparams(("parallel", "arbitrary"), 48),
        name="in_proj_plain",
    )(h, w)


def _proj_rope_kernel(h_ref, wr_ref, wd_ref, ws_ref, c128_ref, s128_ref, c64_ref, sa64_ref, sb64_ref,
                      r_ref, d_ref, small_ref):
    h = h_ref[...]
    acc = _dot(h, wr_ref[...])
    c, s = c128_ref[...], s128_ref[...]
    for blk in range(C_ROPE128 // LANES):
        x = acc[:, blk * LANES:(blk + 1) * LANES]
        r_ref[:, blk * LANES:(blk + 1) * LANES] = (x * c + pltpu.roll(x, 64, axis=1) * s).astype(BF16)
    acc = _dot(h, wd_ref[...])
    c, sa, sb = c64_ref[...], sa64_ref[...], sb64_ref[...]
    for blk in range(C_ROPE64 // LANES):
        x = acc[:, blk * LANES:(blk + 1) * LANES]
        d_ref[:, blk * LANES:(blk + 1) * LANES] = (
            x * c + pltpu.roll(x, 96, axis=1) * sa + pltpu.roll(x, 32, axis=1) * sb).astype(BF16)
    small_ref[...] = _dot(h, ws_ref[...])


def in_proj_rope(h, w_r, w_d, w_s, rope128_tabs, rope64_tabs, s, tm):
    m, k = h.shape
    per_seq = s // tm
    row = lambda w: pl.BlockSpec((tm, w), lambda i: (i, 0))
    full = lambda w: pl.BlockSpec((k, w), lambda i: (0, 0))
    tab = pl.BlockSpec((tm, LANES), lambda i: (i % per_seq, 0))
    return pl.pallas_call(
        _proj_rope_kernel,
        grid=(m // tm,),
        in_specs=[row(k), full(C_ROPE128), full(C_ROPE64), full(LANES), tab, tab, tab, tab, tab],
        out_specs=[row(C_ROPE128), row(C_ROPE64), row(LANES)],
        out_shape=[jax.ShapeDtypeStruct((m, C_ROPE128), BF16), jax.ShapeDtypeStruct((m, C_ROPE64), BF16),
                   jax.ShapeDtypeStruct((m, LANES), F32)],
        compiler_params=_cparams(("parallel",), 48),
        name="in_proj_rope",
    )(h, w_r, w_d, w_s, *rope128_tabs, *rope64_tabs)


def _cum_kernel(f_ref, b_ref, o_ref):
    z = f_ref[0] + b_ref[...]
    ls = _log_sigmoid(z)
    s = ls.shape[1]
    lane = lax.broadcasted_iota(jnp.int32, ls.shape, 1)
    sh = 1
    while sh < s:
        ls = ls + jnp.where(lane >= sh, pltpu.roll(ls, sh, axis=1), 0.0)
        sh *= 2
    o_ref[0] = ls * LOG2E


def forget_cumsum(f_t, bias):
    b, h, s = f_t.shape
    return pl.pallas_call(
        _cum_kernel,
        grid=(b,),
        in_specs=[pl.BlockSpec((1, h, s), lambda i: (i, 0, 0)), pl.BlockSpec((h, 1), lambda i: (0, 0))],
        out_specs=pl.BlockSpec((1, h, s), lambda i: (i, 0, 0)),
        out_shape=jax.ShapeDtypeStruct((b, h, s), F32),
        name="fox_cumsum",
    )(f_t, bias.reshape(h, 1))


def _flash_init(m_ref, l_ref, acc_ref):
    m_ref[...] = jnp.full(m_ref.shape, NEG_INF, F32)
    l_ref[...] = jnp.zeros(l_ref.shape, F32)
    acc_ref[...] = jnp.zeros(acc_ref.shape, F32)


def _flash_step(a, v, m_ref, l_ref, acc_ref):
    m_prev = m_ref[...]
    m_new = jnp.maximum(m_prev, jnp.max(a, axis=1, keepdims=True))
    alpha = jnp.exp2(m_prev - m_new)
    p = jnp.exp2(a - _lanes(m_new, a.shape[1]))
    l_ref[...] = alpha * l_ref[...] + jnp.sum(p, axis=1, keepdims=True)
    acc_ref[...] = alpha * acc_ref[...] + _dot(p.astype(BF16), v)
    m_ref[...] = m_new


def _causal_diag(t):
    return lax.broadcasted_iota(jnp.int32, (t, t), 1) <= lax.broadcasted_iota(jnp.int32, (t, t), 0)


def _fox_kernel(q_ref, k_ref, v_ref, cum_ref, o_ref, m_ref, l_ref, acc_ref, *, t):
    h = pl.program_id(1)
    qi = pl.program_id(2)
    c = HEAD_DIM ** -0.5 * LOG2E
    q = q_ref[0]
    _flash_init(m_ref, l_ref, acc_ref)

    def logits(c0):
        a = _dot_nt(q, k_ref[0, pl.ds(c0, t), :]) * c - cum_ref[0, pl.ds(h, 1), pl.ds(c0, t)]
        return a, v_ref[0, pl.ds(c0, t), :]

    def body(j, carry):
        a, v = logits(pl.multiple_of(j * t, t))
        _flash_step(a, v, m_ref, l_ref, acc_ref)
        return carry

    lax.fori_loop(0, qi, body, 0)
    a, v = logits(pl.multiple_of(qi * t, t))
    _flash_step(jnp.where(_causal_diag(t), a, NEG_INF), v, m_ref, l_ref, acc_ref)
    o_ref[0] = (acc_ref[...] / l_ref[...]).astype(o_ref.dtype)


def fox_attention(u, cum, t):
    b, s, _ = u.shape
    cq, ck, cv = COL_P["fq"], COL_P["fk"], COL_P["fv"]
    stat = pltpu.VMEM((t, LANES), F32)
    return pl.pallas_call(
        functools.partial(_fox_kernel, t=t),
        grid=(b, N_HEADS, s // t),
        in_specs=[
            pl.BlockSpec((1, t, LANES), lambda b_, h, i: (b_, i, cq + h)),
            pl.BlockSpec((1, s, LANES), lambda b_, h, i: (b_, 0, ck + h)),
            pl.BlockSpec((1, s, LANES), lambda b_, h, i: (b_, 0, cv + h)),
            pl.BlockSpec((1, N_HEADS, s), lambda b_, h, i: (b_, 0, 0)),
        ],
        out_specs=pl.BlockSpec((1, t, LANES), lambda b_, h, i: (b_, i, h)),
        out_shape=jax.ShapeDtypeStruct((b, s, BRANCH_WIDTH), BF16),
        scratch_shapes=[stat, stat, stat],
        compiler_params=_cparams(("parallel", "parallel", "arbitrary"), 40),
        name="fox_attention",
    )(u, u, u, cum)


def _sb_kernel(q_ref, k_ref, v_ref, o_ref, carry_ref, acc_ref, *, tq, tk):
    qi = pl.program_id(2)
    q0 = qi * tq
    c = HEAD_DIM ** -0.5 * LOG2E
    q = q_ref[0]
    carry_ref[...] = jnp.zeros(carry_ref.shape, F32)
    acc_ref[...] = jnp.zeros(acc_ref.shape, F32)
    upper = (lax.broadcasted_iota(jnp.int32, (tk, tk), 0) > lax.broadcasted_iota(jnp.int32, (tk, tk), 1)).astype(BF16)
    per_tile = tq // tk

    def chunk(c0, mask):
        z2 = _dot_nt(q, k_ref[0, pl.ds(c0, tk), :]) * c
        ls = jnp.minimum(z2, 0.0) - jnp.log2(1.0 + jnp.exp2(-jnp.abs(z2)))
        l_neg = ls - z2
        if mask is not None:
            l_neg = jnp.where(mask, l_neg, 0.0)
        hi = l_neg.astype(BF16)
        lo = (l_neg - hi.astype(F32)).astype(BF16)
        after = _lanes(carry_ref[...], tk) + _dot(hi, upper) + _dot(lo, upper)
        a = jnp.exp2(ls + after)
        if mask is not None:
            a = jnp.where(mask, a, 0.0)
        acc_ref[...] += _dot(a.astype(BF16), v_ref[0, pl.ds(c0, tk), :])
        carry_ref[...] += jnp.sum(l_neg, axis=1, keepdims=True)

    for d in reversed(range(per_tile)):
        row = lax.broadcasted_iota(jnp.int32, (tq, tk), 0)
        col = d * tk + lax.broadcasted_iota(jnp.int32, (tq, tk), 1)
        chunk(pl.multiple_of(q0 + d * tk, tk), col < row)

    def body(jj, carry):
        chunk(pl.multiple_of((qi * per_tile - 1 - jj) * tk, tk), None)
        return carry

    lax.fori_loop(0, qi * per_tile, body, 0)
    o_ref[0] = acc_ref[...].astype(o_ref.dtype)


def sb_attention(u, tq, tk):
    b, s, _ = u.shape
    cq, ck, cv = COL_P["sq"], COL_P["sk"], COL_P["sv"]
    return pl.pallas_call(
        functools.partial(_sb_kernel, tq=tq, tk=tk),
        grid=(b, N_HEADS, s // tq),
        in_specs=[
            pl.BlockSpec((1, tq, LANES), lambda b_, h, i: (b_, i, cq + h)),
            pl.BlockSpec((1, s, LANES), lambda b_, h, i: (b_, 0, ck + h)),
            pl.BlockSpec((1, s, LANES), lambda b_, h, i: (b_, 0, cv + h)),
        ],
        out_specs=pl.BlockSpec((1, tq, LANES), lambda b_, h, i: (b_, i, h)),
        out_shape=jax.ShapeDtypeStruct((b, s, BRANCH_WIDTH), BF16),
        scratch_shapes=[pltpu.VMEM((tq, LANES), F32), pltpu.VMEM((tq, LANES), F32)],
        compiler_params=_cparams(("parallel", "parallel", "arbitrary"), 40),
        name="sb_attention",
    )(u, u, u)


def _diff_kernel(q1_ref, q2_ref, k1_ref, k2_ref, v_ref, lq1_ref, lk1_ref, lq2_ref, lk2_ref, g_ref, o_ref,
                 m1_ref, l1_ref, a1_ref, m2_ref, l2_ref, a2_ref, *, t, lam_init):
    h = pl.program_id(1)
    qi = pl.program_id(2)
    c = DIFF_DIM ** -0.5 * LOG2E
    mine = (lax.broadcasted_iota(jnp.int32, (t, LANES), 1) // DIFF_DIM) == (h % 2)
    q1 = jnp.where(mine, q1_ref[0], 0.0).astype(BF16)
    q2 = jnp.where(mine, q2_ref[0], 0.0).astype(BF16)
    _flash_init(m1_ref, l1_ref, a1_ref)
    _flash_init(m2_ref, l2_ref, a2_ref)

    def chunk(c0, mask):
        v = v_ref[0, pl.ds(c0, t), :]
        for q, k_ref, m_r, l_r, a_r in ((q1, k1_ref, m1_ref, l1_ref, a1_ref), (q2, k2_ref, m2_ref, l2_ref, a2_ref)):
            a = _dot_nt(q, k_ref[0, pl.ds(c0, t), :]) * c
            if mask is not None:
                a = jnp.where(mask, a, NEG_INF)
            _flash_step(a, v, m_r, l_r, a_r)

    def body(j, carry):
        chunk(pl.multiple_of(j * t, t), None)
        return carry

    lax.fori_loop(0, qi, body, 0)
    chunk(pl.multiple_of(qi * t, t), _causal_diag(t))
    lam = (jnp.exp(jnp.sum(lq1_ref[...] * lk1_ref[...], axis=1, keepdims=True))
           - jnp.exp(jnp.sum(lq2_ref[...] * lk2_ref[...], axis=1, keepdims=True)) + lam_init)
    o = a1_ref[...] / l1_ref[...] - lam * (a2_ref[...] / l2_ref[...])
    y = o * lax.rsqrt(jnp.mean(o * o, axis=-1, keepdims=True) + NORM_EPS)
    o_ref[0] = ((y * g_ref[...]) * (1.0 - lam_init)).astype(o_ref.dtype)


def diff_attention(u_d, u_p, lq1, lk1, lq2, lk2, g, lam_init, t):
    b, s, _ = u_d.shape
    cq1, cq2, ck1, ck2, cv = COL_D["dq1"], COL_D["dq2"], COL_D["dk1"], COL_D["dk2"], COL_P["dv"]
    vec64 = pl.BlockSpec((1, DIFF_DIM), lambda b_, h, i: (0, 0))
    stat = pltpu.VMEM((t, LANES), F32)
    return pl.pallas_call(
        functools.partial(_diff_kernel, t=t, lam_init=lam_init),
        grid=(b, N_HEADS, s // t),
        in_specs=[
            pl.BlockSpec((1, t, LANES), lambda b_, h, i: (b_, i, cq1 + h // 2)),
            pl.BlockSpec((1, t, LANES), lambda b_, h, i: (b_, i, cq2 + h // 2)),
            pl.BlockSpec((1, s, LANES), lambda b_, h, i: (b_, 0, ck1 + h // 2)),
            pl.BlockSpec((1, s, LANES), lambda b_, h, i: (b_, 0, ck2 + h // 2)),
            pl.BlockSpec((1, s, LANES), lambda b_, h, i: (b_, 0, cv + h)),
            vec64, vec64, vec64, vec64,
            pl.BlockSpec((1, HEAD_DIM), lambda b_, h, i: (0, 0)),
        ],
        out_specs=pl.BlockSpec((1, t, LANES), lambda b_, h, i: (b_, i, h)),
        out_shape=jax.ShapeDtypeStruct((b, s, BRANCH_WIDTH), BF16),
        scratch_shapes=[stat] * 6,
        compiler_params=_cparams(("parallel", "parallel", "arbitrary"), 40),
        name="diff_attention",
    )(u_d, u_d, u_d, u_d, u_p,
      lq1.reshape(1, DIFF_DIM), lk1.reshape(1, DIFF_DIM), lq2.reshape(1, DIFF_DIM), lk2.reshape(1, DIFF_DIM),
      g.reshape(1, HEAD_DIM))


def _compress_kernel(kt_ref, vt_ref, pek_ref, pev_ref, wk_ref, wv_ref, kc_ref, vc_ref,
                     xs_ref, xa_ref, xb_ref, *, n_blk):
    half = CMP_BLOCK // 2

    def compress(x, pe_ref, w_ref):
        xs_ref[...] = x.astype(F32)
        for r in range(half):
            piece = xs_ref[pl.ds(r, n_blk, stride=half), :]
            xa_ref[:, r * LANES:(r + 1) * LANES] = (piece + pe_ref[pl.ds(r, 1), :]).astype(BF16)
            xb_ref[:, r * LANES:(r + 1) * LANES] = (piece + pe_ref[pl.ds(half + r, 1), :]).astype(BF16)
        first = _dot(xa_ref[...], w_ref[0])
        second = _dot(xb_ref[...], w_ref[1])
        return first + pltpu.roll(second, n_blk - 1, axis=0)

    for g in range(N_KV_NSA):
        sl = slice(g * LANES, (g + 1) * LANES)
        kc_ref[0, g] = compress(kt_ref[0][:, sl], pek_ref, wk_ref).astype(BF16)
        vc_ref[0, g] = compress(vt_ref[0][:, sl], pev_ref, wv_ref).astype(BF16)


def nsa_compress(u_r, u_p, pe_k, pe_v, w_ck, w_cv):
    b, s, _ = u_r.shape
    n_blk = s // CMP_STRIDE
    half = CMP_BLOCK // 2
    wk = w_ck.reshape(2, half * HEAD_DIM, HEAD_DIM).astype(BF16)
    wv = w_cv.reshape(2, half * HEAD_DIM, HEAD_DIM).astype(BF16)
    pe = pl.BlockSpec((CMP_BLOCK, HEAD_DIM), lambda i: (0, 0))
    wspec = pl.BlockSpec((2, half * HEAD_DIM, HEAD_DIM), lambda i: (0, 0, 0))
    out = pl.BlockSpec((1, N_KV_NSA, n_blk, HEAD_DIM), lambda i: (i, 0, 0, 0))
    return pl.pallas_call(
        functools.partial(_compress_kernel, n_blk=n_blk),
        grid=(b,),
        in_specs=[pl.BlockSpec((1, s, 2 * LANES), lambda i: (i, 0, COL_R["nkc"] // 2)),
                  pl.BlockSpec((1, s, 2 * LANES), lambda i: (i, 0, COL_P["nvc"] // 2)),
                  pe, pe, wspec, wspec],
        out_specs=[out, out],
        out_shape=[jax.ShapeDtypeStruct((b, N_KV_NSA, n_blk, HEAD_DIM), BF16)] * 2,
        scratch_shapes=[pltpu.VMEM((s, LANES), F32), pltpu.VMEM((n_blk, half * LANES), BF16),
                        pltpu.VMEM((n_blk, half * LANES), BF16)],
        compiler_params=_cparams(("parallel",), 40),
        name="nsa_compress",
    )(u_r, u_p, pe_k, pe_v, wk, wv)


def _cmp_attn_kernel(q_ref, kc_ref, vc_ref, ov_ref, oc_ref, mt_ref, *, tq, n_blk, n_sel):
    qi = pl.program_id(1)
    q0 = qi * tq
    scale = HEAD_DIM ** -0.5
    hpg = N_HEADS // N_KV_NSA
    t_row = q0 + lax.broadcasted_iota(jnp.int32, (tq, n_blk), 0)
    n_col = lax.broadcasted_iota(jnp.int32, (tq, n_blk), 1)
    valid = n_col * CMP_STRIDE + (CMP_BLOCK - 1) <= t_row
    t_lane = q0 + lax.broadcasted_iota(jnp.int32, (n_blk, tq), 1)
    n_sub = lax.broadcasted_iota(jnp.int32, (n_blk, tq), 0)
    valid_t = n_sub * CMP_STRIDE + (CMP_BLOCK - 1) <= t_lane
    j_idx = lax.broadcasted_iota(jnp.int32, (n_sel, tq), 0)
    cur = (q0 + lax.broadcasted_iota(jnp.int32, (n_sel, tq), 1)) // SEL_BLOCK
    for g in range(N_KV_NSA):
        kc = kc_ref[0, g]
        vc = vc_ref[0, g]
        p_sum_t = jnp.zeros((n_blk, tq), F32)
        for hh in range(hpg):
            sl = slice((g * hpg + hh) * LANES, (g * hpg + hh + 1) * LANES)
            q = q_ref[0][:, sl]
            s = jnp.where(valid, _dot_nt(q, kc) * scale, NEG_INF)
            m = jnp.max(s, axis=1, keepdims=True)
            m = jnp.where(m == NEG_INF, 0.0, m)
            e = jnp.exp(s - m)
            p = e / jnp.maximum(jnp.sum(e, axis=1, keepdims=True), 1e-30)
            oc_ref[0, :, sl] = _dot(p.astype(BF16), vc)
            st = jnp.where(valid_t, _dot_nt(kc, q) * scale, NEG_INF)
            mt = jnp.max(st, axis=0, keepdims=True)
            mt = jnp.where(mt == NEG_INF, 0.0, mt)
            et = jnp.exp(st - mt)
            p_sum_t = p_sum_t + et / jnp.maximum(jnp.sum(et, axis=0, keepdims=True), 1e-30)
        hi = p_sum_t.astype(BF16)
        lo = (p_sum_t - hi.astype(F32)).astype(BF16)
        imp = _dot(ov_ref[...], hi) + _dot(ov_ref[...], lo)
        imp = jnp.where((j_idx == 0) | (j_idx == cur) | (j_idx == cur - 1), FORCE_SCORE, imp)
        imp = jnp.where(j_idx <= cur, imp, NEG_INF)
        rank = jnp.zeros((n_sel, tq), F32)
        for jp in range(n_sel):
            other = imp[jp:jp + 1, :]
            beats = (other > imp) | ((other == imp) & (j_idx > jp))
            rank = rank + beats.astype(F32)
        mt_ref[0, g] = (rank < float(min(SEL_TOP_N, n_sel))).astype(F32)


def nsa_cmp_attention(u_r, kc, vc, tq):
    b, s, _ = u_r.shape
    n_blk = s // CMP_STRIDE
    n_sel = s // SEL_BLOCK
    ci = jnp.arange(n_blk)[None, :] * CMP_STRIDE
    sj = jnp.arange(n_sel)[:, None] * SEL_BLOCK
    overlap_t = ((ci < sj + SEL_BLOCK) & (ci + CMP_BLOCK > sj)).astype(BF16)
    cblk = pl.BlockSpec((1, N_KV_NSA, n_blk, HEAD_DIM), lambda b_, i: (b_, 0, 0, 0))
    return pl.pallas_call(
        functools.partial(_cmp_attn_kernel, tq=tq, n_blk=n_blk, n_sel=n_sel),
        grid=(b, s // tq),
        in_specs=[pl.BlockSpec((1, tq, BRANCH_WIDTH), lambda b_, i: (b_, i, COL_R["nq"] // N_HEADS)), cblk, cblk,
                  pl.BlockSpec((n_sel, n_blk), lambda b_, i: (0, 0))],
        out_specs=[pl.BlockSpec((1, tq, BRANCH_WIDTH), lambda b_, i: (b_, i, 0)),
                   pl.BlockSpec((1, N_KV_NSA, n_sel, tq), lambda b_, i: (b_, 0, 0, i))],
        out_shape=[jax.ShapeDtypeStruct((b, s, BRANCH_WIDTH), F32),
                   jax.ShapeDtypeStruct((b, N_KV_NSA, n_sel, s), F32)],
        compiler_params=_cparams(("parallel", "parallel"), 40),
        name="nsa_cmp_attention",
    )(u_r, kc, vc, overlap_t)


def _sel_kernel(q_ref, k_ref, vt_ref, mt_ref, o_ref, m_ref, l_ref, acc_ref, *, t):
    qi = pl.program_id(1)
    c = HEAD_DIM ** -0.5 * LOG2E
    hpg = N_HEADS // N_KV_NSA
    per_tile = t // SEL_BLOCK
    causal_t = lax.broadcasted_iota(jnp.int32, (t, t), 0) <= lax.broadcasted_iota(jnp.int32, (t, t), 1)
    for g in range(N_KV_NSA):
        qs = [q_ref[0][:, (g * hpg + hh) * LANES:(g * hpg + hh + 1) * LANES] for hh in range(hpg)]
        _flash_init(m_ref, l_ref, acc_ref)
        gl = slice(g * LANES, (g + 1) * LANES)

        def tile(j, diag, g=g, qs=qs, gl=gl):
            c0 = pl.multiple_of(j * t, t)
            k = k_ref[0, pl.ds(c0, t), gl]
            vt = vt_ref[0, gl, pl.ds(c0, t)]
            rows = [jnp.broadcast_to(mt_ref[0, g, pl.ds(j * per_tile + i, 1), :], (SEL_BLOCK, t))
                    for i in range(per_tile)]
            mask = jnp.concatenate(rows, axis=0) > 0.5
            if diag:
                mask = mask & causal_t
            for hh in range(hpg):
                at = jnp.where(mask, _dot_nt(k, qs[hh]) * c, NEG_INF)
                m_prev = m_ref[hh]
                m_new = jnp.maximum(m_prev, jnp.max(at, axis=0, keepdims=True))
                alpha = jnp.exp2(m_prev - m_new)
                p = jnp.exp2(at - m_new)
                l_ref[hh] = alpha * l_ref[hh] + jnp.sum(p, axis=0, keepdims=True)
                acc_ref[hh] = alpha * acc_ref[hh] + _dot(vt, p.astype(BF16))
                m_ref[hh] = m_new

        def body(j, carry):
            tile(j, False)
            return carry

        lax.fori_loop(0, qi, body, 0)
        tile(qi, True)
        for hh in range(hpg):
            sl = slice((g * hpg + hh) * LANES, (g * hpg + hh + 1) * LANES)
            o_ref[0, :, sl] = (acc_ref[hh] / l_ref[hh]).T


def nsa_sel_attention(u_r, vs_t, mt, t):
    b, s, _ = u_r.shape
    n_sel = s // SEL_BLOCK
    hpg = N_HEADS // N_KV_NSA
    return pl.pallas_call(
        functools.partial(_sel_kernel, t=t),
        grid=(b, s // t),
        in_specs=[pl.BlockSpec((1, t, BRANCH_WIDTH), lambda b_, i: (b_, i, COL_R["nq"] // N_HEADS)),
                  pl.BlockSpec((1, s, 2 * LANES), lambda b_, i: (b_, 0, COL_R["nks"] // 2)),
                  pl.BlockSpec((1, 2 * LANES, s), lambda b_, i: (b_, 0, 0)),
                  pl.BlockSpec((1, N_KV_NSA, n_sel, t), lambda b_, i: (b_, 0, 0, i))],
        out_specs=pl.BlockSpec((1, t, BRANCH_WIDTH), lambda b_, i: (b_, i, 0)),
        out_shape=jax.ShapeDtypeStruct((b, s, BRANCH_WIDTH), F32),
        scratch_shapes=[pltpu.VMEM((hpg, 1, t), F32), pltpu.VMEM((hpg, 1, t), F32),
                        pltpu.VMEM((hpg, HEAD_DIM, t), F32)],
        compiler_params=_cparams(("parallel", "arbitrary"), 40),
        name="nsa_sel_attention",
    )(u_r, u_r, vs_t, mt)


def _win_kernel(q_ref, k_ref, kp_ref, v_ref, vp_ref, oc_ref, os_ref, small_ref, o_ref, m_ref, l_ref, acc_ref, *, t):
    qi = pl.program_id(1)
    c = HEAD_DIM ** -0.5 * LOG2E
    hpg = N_HEADS // N_KV_NSA
    row = lax.broadcasted_iota(jnp.int32, (t, t), 0)
    col = lax.broadcasted_iota(jnp.int32, (t, t), 1)
    gates = jax.nn.sigmoid(small_ref[0])
    for g in range(N_KV_NSA):
        gl = slice(g * LANES, (g + 1) * LANES)
        for hh in range(hpg):
            head = g * hpg + hh
            sl = slice(head * LANES, (head + 1) * LANES)
            q = q_ref[0][:, sl]
            _flash_init(m_ref, l_ref, acc_ref)
            a = jnp.where(col <= row, _dot_nt(q, k_ref[0][:, gl]) * c, NEG_INF)
            _flash_step(a, v_ref[0][:, gl], m_ref, l_ref, acc_ref)

            @pl.when(qi > 0)
            def _():
                ap = jnp.where(col > row, _dot_nt(q, kp_ref[0][:, gl]) * c, NEG_INF)
                _flash_step(ap, vp_ref[0][:, gl], m_ref, l_ref, acc_ref)

            o_w = acc_ref[...] / l_ref[...]
            cg = 4 + 3 * head
            o = (gates[:, cg:cg + 1] * oc_ref[0][:, sl] + gates[:, cg + 1:cg + 2] * os_ref[0][:, sl]
                 + gates[:, cg + 2:cg + 3] * o_w)
            o_ref[0, :, sl] = o.astype(o_ref.dtype)


def nsa_win_attention(u_r, u_p, o_c, o_s, small, t):
    b, s, _ = u_r.shape
    assert t == WINDOW
    wide = pl.BlockSpec((1, t, BRANCH_WIDTH), lambda b_, i: (b_, i, 0))
    ck, cv = COL_R["nkw"] // 2, COL_P["nvw"] // 2
    stat = pltpu.VMEM((t, LANES), F32)
    return pl.pallas_call(
        functools.partial(_win_kernel, t=t),
        grid=(b, s // t),
        in_specs=[pl.BlockSpec((1, t, BRANCH_WIDTH), lambda b_, i: (b_, i, COL_R["nq"] // N_HEADS)),
                  pl.BlockSpec((1, t, 2 * LANES), lambda b_, i: (b_, i, ck)),
                  pl.BlockSpec((1, t, 2 * LANES), lambda b_, i: (b_, jnp.maximum(i - 1, 0), ck)),
                  pl.BlockSpec((1, t, 2 * LANES), lambda b_, i: (b_, i, cv)),
                  pl.BlockSpec((1, t, 2 * LANES), lambda b_, i: (b_, jnp.maximum(i - 1, 0), cv)),
                  wide, wide,
                  pl.BlockSpec((1, t, LANES), lambda b_, i: (b_, i, 0))],
        out_specs=wide,
        out_shape=jax.ShapeDtypeStruct((b, s, BRANCH_WIDTH), BF16),
        scratch_shapes=[stat, stat, stat],
        compiler_params=_cparams(("parallel", "arbitrary"), 48),
        name="nsa_win_attention",
    )(u_r, u_r, u_r, u_p, u_p, o_c, o_s, small)


def _merge_kernel(h_ref, o0_ref, o1_ref, o2_ref, o3_ref, wg_ref, wb_ref, out_ref):
    h = h_ref[...]
    acc = None
    for n, o_ref in enumerate((o0_ref, o1_ref, o2_ref, o3_ref)):
        term = jax.nn.sigmoid(_dot(h, wg_ref[n])) * _dot(o_ref[...], wb_ref[n])
        acc = term if acc is None else acc + term
    out_ref[...] = acc.astype(out_ref.dtype)


def merge_branches(h, branches, wg, wb, tm, tn):
    n, d = h.shape
    bspec = pl.BlockSpec((tm, BRANCH_WIDTH), lambda j, i: (i, 0))
    return pl.pallas_call(
        _merge_kernel,
        grid=(d // tn, n // tm),
        in_specs=[pl.BlockSpec((tm, d), lambda j, i: (i, 0)), bspec, bspec, bspec, bspec,
                  pl.BlockSpec((4, d, tn), lambda j, i: (0, 0, j)),
                  pl.BlockSpec((4, BRANCH_WIDTH, tn), lambda j, i: (0, 0, j))],
        out_specs=pl.BlockSpec((tm, tn), lambda j, i: (i, j)),
        out_shape=jax.ShapeDtypeStruct((n, d), BF16),
        compiler_params=_cparams(("parallel", "arbitrary"), 56),
        name="merge_branches",
    )(h, *branches, wg, wb)


def _out_kernel(mg_ref, x_ref, w_ref, g_ref, wr_ref, br_ref, x1_ref, h2_ref, route_ref, *, tm):
    x1 = x_ref[...] + _dot(mg_ref[...], w_ref[...])
    x1_ref[...] = x1
    h2 = (x1 * lax.rsqrt(jnp.mean(x1 * x1, axis=-1, keepdims=True) + NORM_EPS)) * g_ref[...]
    for s in range(ROW_TILES):
        h2_ref[pl.ds(s, tm, stride=ROW_TILES), :] = h2[:, s * LANES:(s + 1) * LANES]
    logits = _dot(h2.astype(BF16), wr_ref[...]) + br_ref[...]
    lane = lax.broadcasted_iota(jnp.int32, logits.shape, 1)
    lane_f = lane.astype(F32)
    big = float(LANES)
    is_g = lane < N_GROUPS
    lg = jnp.where(is_g, logits, NEG_INF)
    mx = jnp.max(lg, axis=1, keepdims=True)
    gi = jnp.min(jnp.where(lg == mx, lane_f, big), axis=1, keepdims=True)
    pg = 1.0 / jnp.sum(jnp.where(is_g, jnp.exp(lg - mx), 0.0), axis=1, keepdims=True)
    e_idx = lane - N_GROUPS
    in_grp = (lane >= N_GROUPS) & (lane < N_GROUPS + N_EXPERTS) & ((e_idx // EXPERTS_PER_GROUP).astype(F32) == gi)
    le = jnp.where(in_grp, logits, NEG_INF)
    v1 = jnp.max(le, axis=1, keepdims=True)
    i1 = jnp.min(jnp.where(le == v1, lane_f, big), axis=1, keepdims=True)
    le2 = jnp.where(lane_f == i1, NEG_INF, le)
    v2 = jnp.max(le2, axis=1, keepdims=True)
    i2 = jnp.min(jnp.where(le2 == v2, lane_f, big), axis=1, keepdims=True)
    e2 = jnp.exp(v2 - v1)
    w1 = pg / (1.0 + e2)
    w2 = pg * e2 / (1.0 + e2)
    route = jnp.where(lane == 0, i1 - N_GROUPS, 0.0)
    route = jnp.where(lane == 1, i2 - N_GROUPS, route)
    route = jnp.where(lane == 2, w1, route)
    route = jnp.where(lane == 3, w2, route)
    route_ref[...] = route


def out_proj_router(merged, x, w_out, g2, w_router, b_router, tm):
    n, d = x.shape
    row = pl.BlockSpec((tm, d), lambda i: (i, 0))
    return pl.pallas_call(
        functools.partial(_out_kernel, tm=tm),
        grid=(n // tm,),
        in_specs=[row, row, pl.BlockSpec((d, d), lambda i: (0, 0)), pl.BlockSpec((1, d), lambda i: (0, 0)),
                  pl.BlockSpec((d, LANES), lambda i: (0, 0)), pl.BlockSpec((1, LANES), lambda i: (0, 0))],
        out_specs=[row, pl.BlockSpec((tm * ROW_TILES, LANES), lambda i: (i, 0)),
                   pl.BlockSpec((tm, LANES), lambda i: (i, 0))],
        out_shape=[jax.ShapeDtypeStruct((n, d), F32), jax.ShapeDtypeStruct((n * ROW_TILES, LANES), F32),
                   jax.ShapeDtypeStruct((n, LANES), F32)],
        compiler_params=_cparams(("parallel",), 56),
        name="out_proj_router",
    )(merged, x, w_out, g2.reshape(1, d), w_router, b_router)


def _gather_rows(src_hbm, idx_ref, n_rows, buf_ref, slot, sem):
    base = slot * n_rows

    def issue(r, carry):
        src = pl.multiple_of(idx_ref[0, 0, r] * ROW_TILES, ROW_TILES)
        dst = pl.multiple_of((base + r) * ROW_TILES, ROW_TILES)
        pltpu.make_async_copy(src_hbm.at[pl.ds(src, ROW_TILES), :], buf_ref.at[pl.ds(dst, ROW_TILES), :],
                              sem.at[slot]).start()
        return carry

    lax.fori_loop(0, n_rows, issue, 0)


def _wait_rows(src_hbm, n_rows, buf_ref, slot, sem):
    dst = pl.multiple_of(slot * n_rows * ROW_TILES, ROW_TILES)
    pltpu.make_async_copy(src_hbm.at[pl.ds(0, n_rows * ROW_TILES), :],
                          buf_ref.at[pl.ds(dst, n_rows * ROW_TILES), :], sem.at[slot]).wait()


def _ffn_kernel(blk_e_ref, n_used_ref, tok_ref, tok_next_ref, sw_ref, h2_hbm, wg_ref, wu_ref, wd_ref, y_ref,
                xg_ref, sem, *, tm):
    blk = pl.program_id(0)
    n_used = n_used_ref[0]
    slot = blk % 2

    @pl.when(blk == 0)
    def _():
        _gather_rows(h2_hbm, tok_ref, tm, xg_ref, 0, sem)

    @pl.when(blk + 1 < n_used)
    def _():
        _gather_rows(h2_hbm, tok_next_ref, tm, xg_ref, 1 - slot, sem)

    @pl.when(blk < n_used)
    def _():
        _wait_rows(h2_hbm, tm, xg_ref, slot, sem)
        base = slot * tm * ROW_TILES
        x = jnp.concatenate([xg_ref[pl.ds(base + s, tm, stride=ROW_TILES), :] for s in range(ROW_TILES)],
                            axis=1).astype(BF16)
        gate = _dot(x, wg_ref[0])
        up = _dot(x, wu_ref[0])
        hid = (gate * jax.nn.sigmoid(gate) * up).astype(BF16)
        y = _dot(hid, wd_ref[0]) * sw_ref[...]
        for s in range(ROW_TILES):
            y_ref[pl.ds(s, tm, stride=ROW_TILES), :] = y[:, s * LANES:(s + 1) * LANES]

    @pl.when(blk >= n_used)
    def _():
        y_ref[...] = jnp.zeros(y_ref.shape, F32)


def moe_experts(blk_e, n_used, buf_tok, buf_w, h2_rows, w_g, w_u, w_d, tm):
    n_blk = blk_e.shape[0]
    d, ff = w_g.shape[1], w_g.shape[2]
    tok = buf_tok.reshape(n_blk, 1, tm)
    grid_spec = pltpu.PrefetchScalarGridSpec(
        num_scalar_prefetch=2,
        grid=(n_blk,),
        in_specs=[
            pl.BlockSpec((1, 1, tm), lambda i, be, nu: (i, 0, 0), memory_space=pltpu.SMEM),
            pl.BlockSpec((1, 1, tm), lambda i, be, nu: (jnp.minimum(i + 1, n_blk - 1), 0, 0), memory_space=pltpu.SMEM),
            pl.BlockSpec((tm, 1), lambda i, be, nu: (i, 0)),
            pl.BlockSpec(memory_space=pl.ANY),
            pl.BlockSpec((1, d, ff), lambda i, be, nu: (be[i], 0, 0)),
            pl.BlockSpec((1, d, ff), lambda i, be, nu: (be[i], 0, 0)),
            pl.BlockSpec((1, ff, d), lambda i, be, nu: (be[i], 0, 0)),
        ],
        out_specs=pl.BlockSpec((tm * ROW_TILES, LANES), lambda i, be, nu: (i, 0)),
        scratch_shapes=[pltpu.VMEM((2 * tm * ROW_TILES, LANES), F32), pltpu.SemaphoreType.DMA((2,))],
    )
    return pl.pallas_call(
        functools.partial(_ffn_kernel, tm=tm),
        grid_spec=grid_spec,
        out_shape=jax.ShapeDtypeStruct((n_blk * tm * ROW_TILES, LANES), F32),
        compiler_params=_cparams(("arbitrary",), 56),
        name="moe_experts",
    )(blk_e, n_used, tok, tok, buf_w.reshape(n_blk * tm, 1), h2_rows, w_g, w_u, w_d)


def _combine_kernel(pos_ref, pos_next_ref, x1_ref, yb_hbm, g_ref, x2_ref, hn_ref, buf_ref, sem, *, tm):
    i = pl.program_id(0)
    n_tiles = pl.num_programs(0)
    slot = i % 2

    @pl.when(i == 0)
    def _():
        _gather_rows(yb_hbm, pos_ref, 2 * tm, buf_ref, 0, sem)

    @pl.when(i + 1 < n_tiles)
    def _():
        _gather_rows(yb_hbm, pos_next_ref, 2 * tm, buf_ref, 1 - slot, sem)

    _wait_rows(yb_hbm, 2 * tm, buf_ref, slot, sem)
    base = slot * 2 * tm * ROW_TILES
    pieces = []
    ssq = jnp.zeros((tm, 1), F32)
    for s in range(ROW_TILES):
        piece = (x1_ref[:, s * LANES:(s + 1) * LANES]
                 + buf_ref[pl.ds(base + s, tm, stride=2 * ROW_TILES), :]
                 + buf_ref[pl.ds(base + ROW_TILES + s, tm, stride=2 * ROW_TILES), :])
        x2_ref[:, s * LANES:(s + 1) * LANES] = piece
        ssq = ssq + jnp.sum(piece * piece, axis=1, keepdims=True)
        pieces.append(piece)
    inv = lax.rsqrt(ssq / D_MODEL + NORM_EPS)
    for s in range(ROW_TILES):
        sl = slice(s * LANES, (s + 1) * LANES)
        hn_ref[:, sl] = ((pieces[s] * inv) * g_ref[:, sl]).astype(hn_ref.dtype)


def moe_combine(pos, x1, yb_rows, g, hn_dtype, tm):
    n, d = x1.shape
    n_tiles = n // tm
    row = pl.BlockSpec((tm, d), lambda i: (i, 0))
    pos3 = pos.reshape(n_tiles, 1, 2 * tm)
    return pl.pallas_call(
        functools.partial(_combine_kernel, tm=tm),
        grid=(n_tiles,),
        in_specs=[pl.BlockSpec((1, 1, 2 * tm), lambda i: (i, 0, 0), memory_space=pltpu.SMEM),
                  pl.BlockSpec((1, 1, 2 * tm), lambda i: (jnp.minimum(i + 1, n_tiles - 1), 0, 0), memory_space=pltpu.SMEM),
                  row, pl.BlockSpec(memory_space=pl.ANY), pl.BlockSpec((1, d), lambda i: (0, 0))],
        out_specs=[row, row],
        out_shape=[jax.ShapeDtypeStruct((n, d), F32), jax.ShapeDtypeStruct((n, d), hn_dtype)],
        scratch_shapes=[pltpu.VMEM((2 * 2 * tm * ROW_TILES, LANES), F32), pltpu.SemaphoreType.DMA((2,))],
        compiler_params=_cparams(("arbitrary",), 48),
        name="moe_combine",
    )(pos3, pos3, x1, yb_rows, g.reshape(1, d))


def _dispatch_plan(route, tm):
    n = route.shape[0]
    m = 2 * n
    slot_e = route[:, 0:2].astype(jnp.int32).reshape(m)
    slot_w = route[:, 2:4].reshape(m)
    iota = jnp.arange(m, dtype=jnp.int32)
    se, order, sw = lax.sort((slot_e, iota, slot_w), num_keys=1)
    e_ids = jnp.arange(N_EXPERTS, dtype=jnp.int32)
    counts = jnp.sum((slot_e[None, :] == e_ids[:, None]).astype(jnp.int32), axis=1)
    start = jnp.cumsum(counts) - counts
    padded = (counts + tm - 1) // tm * tm
    pend = jnp.cumsum(padded)
    pstart = pend - padded
    delta = pstart - start
    dest = iota + jnp.sum(jnp.where(se[:, None] == e_ids[None, :], delta[None, :], 0), axis=1)
    _, pos = lax.sort((order, dest), num_keys=1)
    n_blk = (m + N_EXPERTS * tm) // tm
    blk_row0 = jnp.arange(n_blk, dtype=jnp.int32) * tm
    blk_e = jnp.minimum(jnp.sum((pend[None, :] <= blk_row0[:, None]).astype(jnp.int32), axis=1), N_EXPERTS - 1)
    onehot = blk_e[:, None] == e_ids[None, :]
    pick = lambda tab: jnp.sum(jnp.where(onehot, tab[None, :], 0), axis=1)
    local0 = blk_row0 - pick(pstart)
    n_valid = jnp.clip(pick(counts) - local0, 0, tm)
    src0 = jnp.clip(pick(start) + local0, 0, m)
    window = lambda arr: jax.vmap(lambda s0: lax.dynamic_slice(arr, (s0,), (tm,)))(src0)
    pad_i = jnp.zeros((tm,), jnp.int32)
    valid = jnp.arange(tm, dtype=jnp.int32)[None, :] < n_valid[:, None]
    buf_tok = jnp.where(valid, window(jnp.concatenate([order, pad_i])) // 2, 0)
    buf_w = jnp.where(valid, window(jnp.concatenate([sw, jnp.zeros((tm,), F32)])), 0.0)
    n_used = (pend[-1] // tm).astype(jnp.int32).reshape(1)
    return blk_e.astype(jnp.int32), n_used, buf_tok, buf_w, pos


def _rope_tables(s):
    pos = jnp.arange(s, dtype=F32)[:, None]
    inv128 = jnp.exp(-math.log(ROPE_THETA) * jnp.arange(0, HEAD_DIM, 2, dtype=F32) / HEAD_DIM)
    a = pos * inv128[None, :]
    c128 = jnp.concatenate([jnp.cos(a), jnp.cos(a)], axis=1)
    s128 = jnp.concatenate([-jnp.sin(a), jnp.sin(a)], axis=1)
    inv64 = jnp.exp(-math.log(ROPE_THETA) * jnp.arange(0, DIFF_DIM, 2, dtype=F32) / DIFF_DIM)
    a = pos * inv64[None, :]
    co, si, z = jnp.cos(a), jnp.sin(a), jnp.zeros_like(a)
    c64 = jnp.concatenate([co, co, co, co], axis=1)
    sa64 = jnp.concatenate([-si, z, -si, z], axis=1)
    sb64 = jnp.concatenate([z, si, z, si], axis=1)
    return (c128, s128), (c64, sa64, sb64)


def _split_w_in(w_in):
    offs = {}
    o = 0
    for name, w in zip(_ORIG_NAMES, _ORIG_WIDTHS):
        offs[name] = (o, w)
        o += w
    take = lambda pieces: jnp.concatenate([w_in[:, offs[nm][0]:offs[nm][0] + offs[nm][1]] for nm, _ in pieces],
                                          axis=1).astype(BF16)
    f0, fw = offs["ff"]
    g0, gw = offs["ngt"]
    w_small = jnp.concatenate([w_in[:, f0:f0 + fw], w_in[:, g0:g0 + gw],
                               jnp.zeros((w_in.shape[0], LANES - fw - gw), w_in.dtype)], axis=1).astype(BF16)
    return take(_PLAIN), take(_ROPE128), take(_ROPE64), w_small


def _tiles(s, n):
    return dict(t_attn=min(512, s), tq_nsa=min(256, s), tk_sb=min(256, s), tm_norm=min(512, n), tm_proj=min(512, n),
                tm_rows=min(256, n), tm_moe=256)


def kernel(x, norm1_g, w_in, fox_bf, nsa_pe_k, nsa_pe_v, nsa_w_ck, nsa_w_cv, diff_lq1, diff_lk1, diff_lq2, diff_lk2, diff_norm_g, w_branch, w_mgate, w_out, norm2_g, w_rg, b_rg, w_re, b_re, w_eg, w_eu, w_ed, final_g):
    b, s, d = x.shape
    n = b * s
    depth = w_in.shape[0]
    t = _tiles(s, n)
    rope128_tabs, rope64_tabs = _rope_tables(s)
    xf = x.reshape(n, d)
    h = rms_norm_rows(xf, norm1_g[0], BF16, t["tm_norm"])
    out = None
    for l in range(depth):
        lam_init = 0.8 - 0.6 * math.exp(-0.3 * l)
        w_p, w_r, w_d, w_s = _split_w_in(w_in[l])
        u_p = in_proj_plain(h, w_p, t["tm_proj"], C_PLAIN // 2).reshape(b, s, C_PLAIN)
        u_r, u_d, small = in_proj_rope(h, w_r, w_d, w_s, rope128_tabs, rope64_tabs, s, t["tm_proj"])
        u_r, u_d, small = u_r.reshape(b, s, C_ROPE128), u_d.reshape(b, s, C_ROPE64), small.reshape(b, s, LANES)
        cum = forget_cumsum(small[:, :, 0:N_HEADS].transpose(0, 2, 1), fox_bf[l])
        o_fox = fox_attention(u_p, cum, t["t_attn"])
        o_sb = sb_attention(u_p, t["t_attn"], t["tk_sb"])
        o_diff = diff_attention(u_d, u_p, diff_lq1[l], diff_lk1[l], diff_lq2[l], diff_lk2[l], diff_norm_g[l],
                                lam_init, t["t_attn"])
        kc, vc = nsa_compress(u_r, u_p, nsa_pe_k[l], nsa_pe_v[l], nsa_w_ck[l], nsa_w_cv[l])
        o_c, mt = nsa_cmp_attention(u_r, kc, vc, t["tq_nsa"])
        nvs0 = COL_P["nvs"] * LANES
        vs_t = u_p[:, :, nvs0:nvs0 + 2 * LANES].transpose(0, 2, 1)
        o_s = nsa_sel_attention(u_r, vs_t, mt, t["t_attn"])
        o_nsa = nsa_win_attention(u_r, u_p, o_c, o_s, small, t["t_attn"])
        branches = [o.reshape(n, BRANCH_WIDTH) for o in (o_fox, o_nsa, o_sb, o_diff)]
        merged = merge_branches(h, branches, w_mgate[l].astype(BF16), w_branch[l].astype(BF16), t["tm_proj"], 512)
        w_router = jnp.concatenate([w_rg[l], w_re[l], jnp.zeros((d, LANES - N_GROUPS - N_EXPERTS), F32)], axis=1).astype(BF16)
        b_router = jnp.concatenate([b_rg[l], b_re[l], jnp.zeros((LANES - N_GROUPS - N_EXPERTS,), F32)]).reshape(1, LANES)
        x1, h2_rows, route = out_proj_router(merged, xf, w_out[l].astype(BF16), norm2_g[l], w_router, b_router, t["tm_rows"])
        blk_e, n_used, buf_tok, buf_w, pos = _dispatch_plan(route, t["tm_moe"])
        yb_rows = moe_experts(blk_e, n_used, buf_tok, buf_w, h2_rows,
                              w_eg[l].astype(BF16), w_eu[l].astype(BF16), w_ed[l].astype(BF16), t["tm_moe"])
        last = l == depth - 1
        g_next = final_g if last else norm1_g[l + 1]
        xf, hn = moe_combine(pos, x1, yb_rows, g_next, F32 if last else BF16, t["tm_rows"])
        h = hn
        out = hn
    return out.reshape(b, s, d)
```

```python
import functools
import math

import jax
import jax.numpy as jnp
from jax import lax
from jax.experimental import pallas as pl
from jax.experimental.pallas import tpu as pltpu

F32 = jnp.float32
BF16 = jnp.bfloat16
NEG_INF = float("-inf")
LOG2E = 1.4426950408889634

D_MODEL = 2048
HEAD_DIM = 128
DIFF_DIM = 64
N_HEADS = 4
N_KV_NSA = 2
BRANCH_WIDTH = 512
ROPE_THETA = 10000.0
NORM_EPS = 1e-6
CMP_BLOCK = 32
CMP_STRIDE = 16
SEL_BLOCK = 64
SEL_TOP_N = 16
WINDOW = 512
FORCE_SCORE = 1e6
N_GROUPS = 4
EXPERTS_PER_GROUP = 8
N_EXPERTS = 32
EXPERT_FF = 1024

LANES = 128
ROW_TILES = D_MODEL // LANES
MIB = 1024 * 1024

_PLAIN = (("fq", 512), ("fk", 512), ("fv", 512), ("nvc", 256), ("nvs", 256), ("nvw", 256),
          ("sq", 512), ("sk", 512), ("sv", 512), ("dv", 512))
_ROPE128 = (("nq", 512), ("nkc", 256), ("nks", 256), ("nkw", 256))
_ROPE64 = (("dq1", 256), ("dq2", 256), ("dk1", 256), ("dk2", 256))


def _layout(pieces):
    col, off = {}, 0
    for name, w in pieces:
        col[name] = off // LANES
        off += w
    return col, off


COL_P, C_PLAIN = _layout(_PLAIN)
COL_R, C_ROPE128 = _layout(_ROPE128)
COL_D, C_ROPE64 = _layout(_ROPE64)

_ORIG_WIDTHS = (512, 512, 512, 4, 512, 256, 256, 256, 256, 256, 256, 12, 512, 512, 512, 256, 256, 256, 256, 512)
_ORIG_NAMES = ("fq", "fk", "fv", "ff", "nq", "nkc", "nvc", "nks", "nvs", "nkw", "nvw", "ngt",
               "sq", "sk", "sv", "dq1", "dq2", "dk1", "dk2", "dv")


def _cparams(sem, vmem_mib):
    return pltpu.CompilerParams(dimension_semantics=sem, vmem_limit_bytes=vmem_mib * MIB)


def _log_sigmoid(z):
    return jnp.minimum(z, 0.0) - jnp.log1p(jnp.exp(-jnp.abs(z)))


def _dot_nt(a, b):
    return lax.dot_general(a, b, (((1,), (1,)), ((), ())), preferred_element_type=F32)


def _dot(a, b):
    return jnp.dot(a, b, preferred_element_type=F32)


def _lanes(x, width):
    return x if width == LANES else jnp.concatenate([x] * (width // LANES), axis=1)


def _norm_kernel(x_ref, g_ref, o_ref):
    x = x_ref[...]
    y = x * lax.rsqrt(jnp.mean(x * x, axis=-1, keepdims=True) + NORM_EPS)
    o_ref[...] = (y * g_ref[...]).astype(o_ref.dtype)


def rms_norm_rows(x, g, out_dtype, tm):
    n, d = x.shape
    return pl.pallas_call(
        _norm_kernel,
        grid=(n // tm,),
        in_specs=[pl.BlockSpec((tm, d), lambda i: (i, 0)), pl.BlockSpec((1, d), lambda i: (0, 0))],
        out_specs=pl.BlockSpec((tm, d), lambda i: (i, 0)),
        out_shape=jax.ShapeDtypeStruct((n, d), out_dtype),
        compiler_params=_cparams(("parallel",), 40),
        name="rms_norm",
    )(x, g.reshape(1, d))


def _mm_kernel(a_ref, w_ref, o_ref):
    o_ref[...] = _dot(a_ref[...], w_ref[...]).astype(o_ref.dtype)


def in_proj_plain(h, w, tm, tn):
    m, k = h.shape
    c = w.shape[1]
    return pl.pallas_call(
        _mm_kernel,
        grid=(c // tn, m // tm),
        in_specs=[pl.BlockSpec((tm, k), lambda j, i: (i, 0)), pl.BlockSpec((k, tn), lambda j, i: (0, j))],
        out_specs=pl.BlockSpec((tm, tn), lambda j, i: (i, j)),
        out_shape=jax.ShapeDtypeStruct((m, c), BF16),
        compiler_params=_cparams(("parallel", "arbitrary"), 48),
        name="in_proj_plain",
    )(h, w)


def _proj_rope_kernel(h_ref, wr_ref, wd_ref, ws_ref, c128_ref, s128_ref, c64_ref, sa64_ref, sb64_ref,
                      r_ref, d_ref, small_ref):
    h = h_ref[...]
    acc = _dot(h, wr_ref[...])
    c, s = c128_ref[...], s128_ref[...]
    for blk in range(C_ROPE128 // LANES):
        x = acc[:, blk * LANES:(blk + 1) * LANES]
        r_ref[:, blk * LANES:(blk + 1) * LANES] = (x * c + pltpu.roll(x, 64, axis=1) * s).astype(BF16)
    acc = _dot(h, wd_ref[...])
    c, sa, sb = c64_ref[...], sa64_ref[...], sb64_ref[...]
    for blk in range(C_ROPE64 // LANES):
        x = acc[:, blk * LANES:(blk + 1) * LANES]
        d_ref[:, blk * LANES:(blk + 1) * LANES] = (
            x * c + pltpu.roll(x, 96, axis=1) * sa + pltpu.roll(x, 32, axis=1) * sb).astype(BF16)
    small_ref[...] = _dot(h, ws_ref[...])


def in_proj_rope(h, w_r, w_d, w_s, rope128_tabs, rope64_tabs, s, tm):
    m, k = h.shape
    per_seq = s // tm
    row = lambda w: pl.BlockSpec((tm, w), lambda i: (i, 0))
    full = lambda w: pl.BlockSpec((k, w), lambda i: (0, 0))
    tab = pl.BlockSpec((tm, LANES), lambda i: (i % per_seq, 0))
    return pl.pallas_call(
        _proj_rope_kernel,
        grid=(m // tm,),
        in_specs=[row(k), full(C_ROPE128), full(C_ROPE64), full(LANES), tab, tab, tab, tab, tab],
        out_specs=[row(C_ROPE128), row(C_ROPE64), row(LANES)],
        out_shape=[jax.ShapeDtypeStruct((m, C_ROPE128), BF16), jax.ShapeDtypeStruct((m, C_ROPE64), BF16),
                   jax.ShapeDtypeStruct((m, LANES), F32)],
        compiler_params=_cparams(("parallel",), 48),
        name="in_proj_rope",
    )(h, w_r, w_d, w_s, *rope128_tabs, *rope64_tabs)


def _cum_kernel(f_ref, b_ref, o_ref):
    z = f_ref[0] + b_ref[...]
    ls = _log_sigmoid(z)
    s = ls.shape[1]
    lane = lax.broadcasted_iota(jnp.int32, ls.shape, 1)
    sh = 1
    while sh < s:
        ls = ls + jnp.where(lane >= sh, pltpu.roll(ls, sh, axis=1), 0.0)
        sh *= 2
    o_ref[0] = ls * LOG2E


def forget_cumsum(f_t, bias):
    b, h, s = f_t.shape
    return pl.pallas_call(
        _cum_kernel,
        grid=(b,),
        in_specs=[pl.BlockSpec((1, h, s), lambda i: (i, 0, 0)), pl.BlockSpec((h, 1), lambda i: (0, 0))],
        out_specs=pl.BlockSpec((1, h, s), lambda i: (i, 0, 0)),
        out_shape=jax.ShapeDtypeStruct((b, h, s), F32),
        name="fox_cumsum",
    )(f_t, bias.reshape(h, 1))


def _flash_init(m_ref, l_ref, acc_ref):
    m_ref[...] = jnp.full(m_ref.shape, NEG_INF, F32)
    l_ref[...] = jnp.zeros(l_ref.shape, F32)
    acc_ref[...] = jnp.zeros(acc_ref.shape, F32)


def _flash_step(a, v, m_ref, l_ref, acc_ref):
    m_prev = m_ref[...]
    m_new = jnp.maximum(m_prev, jnp.max(a, axis=1, keepdims=True))
    alpha = jnp.exp2(m_prev - m_new)
    p = jnp.exp2(a - _lanes(m_new, a.shape[1]))
    l_ref[...] = alpha * l_ref[...] + jnp.sum(p, axis=1, keepdims=True)
    acc_ref[...] = alpha * acc_ref[...] + _dot(p.astype(BF16), v)
    m_ref[...] = m_new


def _causal_diag(t):
    return lax.broadcasted_iota(jnp.int32, (t, t), 1) <= lax.broadcasted_iota(jnp.int32, (t, t), 0)


def _fox_kernel(q_ref, k_ref, v_ref, cum_ref, o_ref, m_ref, l_ref, acc_ref, *, t):
    h = pl.program_id(1)
    qi = pl.program_id(2)
    c = HEAD_DIM ** -0.5 * LOG2E
    q = q_ref[0]
    _flash_init(m_ref, l_ref, acc_ref)

    def logits(c0):
        a = _dot_nt(q, k_ref[0, pl.ds(c0, t), :]) * c - cum_ref[0, pl.ds(h, 1), pl.ds(c0, t)]
        return a, v_ref[0, pl.ds(c0, t), :]

    def body(j, carry):
        a, v = logits(pl.multiple_of(j * t, t))
        _flash_step(a, v, m_ref, l_ref, acc_ref)
        return carry

    lax.fori_loop(0, qi, body, 0)
    a, v = logits(pl.multiple_of(qi * t, t))
    _flash_step(jnp.where(_causal_diag(t), a, NEG_INF), v, m_ref, l_ref, acc_ref)
    o_ref[0] = (acc_ref[...] / l_ref[...]).astype(o_ref.dtype)


def fox_attention(u, cum, t):
    b, s, _ = u.shape
    cq, ck, cv = COL_P["fq"], COL_P["fk"], COL_P["fv"]
    stat = pltpu.VMEM((t, LANES), F32)
    return pl.pallas_call(
        functools.partial(_fox_kernel, t=t),
        grid=(b, N_HEADS, s // t),
        in_specs=[
            pl.BlockSpec((1, t, LANES), lambda b_, h, i: (b_, i, cq + h)),
            pl.BlockSpec((1, s, LANES), lambda b_, h, i: (b_, 0, ck + h)),
            pl.BlockSpec((1, s, LANES), lambda b_, h, i: (b_, 0, cv + h)),
            pl.BlockSpec((1, N_HEADS, s), lambda b_, h, i: (b_, 0, 0)),
        ],
        out_specs=pl.BlockSpec((1, t, LANES), lambda b_, h, i: (b_, i, h)),
        out_shape=jax.ShapeDtypeStruct((b, s, BRANCH_WIDTH), BF16),
        scratch_shapes=[stat, stat, stat],
        compiler_params=_cparams(("parallel", "parallel", "arbitrary"), 40),
        name="fox_attention",
    )(u, u, u, cum)


def _sb_kernel(q_ref, k_ref, v_ref, o_ref, carry_ref, acc_ref, *, tq, tk):
    qi = pl.program_id(2)
    q0 = qi * tq
    c = HEAD_DIM ** -0.5 * LOG2E
    q = q_ref[0]
    carry_ref[...] = jnp.zeros(carry_ref.shape, F32)
    acc_ref[...] = jnp.zeros(acc_ref.shape, F32)
    upper = (lax.broadcasted_iota(jnp.int32, (tk, tk), 0) > lax.broadcasted_iota(jnp.int32, (tk, tk), 1)).astype(BF16)
    per_tile = tq // tk

    def chunk(c0, mask):
        z2 = _dot_nt(q, k_ref[0, pl.ds(c0, tk), :]) * c
        ls = jnp.minimum(z2, 0.0) - jnp.log2(1.0 + jnp.exp2(-jnp.abs(z2)))
        l_neg = ls - z2
        if mask is not None:
            l_neg = jnp.where(mask, l_neg, 0.0)
        hi = l_neg.astype(BF16)
        lo = (l_neg - hi.astype(F32)).astype(BF16)
        after = _lanes(carry_ref[...], tk) + _dot(hi, upper) + _dot(lo, upper)
        a = jnp.exp2(ls + after)
        if mask is not None:
            a = jnp.where(mask, a, 0.0)
        acc_ref[...] += _dot(a.astype(BF16), v_ref[0, pl.ds(c0, tk), :])
        carry_ref[...] += jnp.sum(l_neg, axis=1, keepdims=True)

    for d in reversed(range(per_tile)):
        row = lax.broadcasted_iota(jnp.int32, (tq, tk), 0)
        col = d * tk + lax.broadcasted_iota(jnp.int32, (tq, tk), 1)
        chunk(pl.multiple_of(q0 + d * tk, tk), col < row)

    def body(jj, carry):
        chunk(pl.multiple_of((qi * per_tile - 1 - jj) * tk, tk), None)
        return carry

    lax.fori_loop(0, qi * per_tile, body, 0)
    o_ref[0] = acc_ref[...].astype(o_ref.dtype)


def sb_attention(u, tq, tk):
    b, s, _ = u.shape
    cq, ck, cv = COL_P["sq"], COL_P["sk"], COL_P["sv"]
    return pl.pallas_call(
        functools.partial(_sb_kernel, tq=tq, tk=tk),
        grid=(b, N_HEADS, s // tq),
        in_specs=[
            pl.BlockSpec((1, tq, LANES), lambda b_, h, i: (b_, i, cq + h)),
            pl.BlockSpec((1, s, LANES), lambda b_, h, i: (b_, 0, ck + h)),
            pl.BlockSpec((1, s, LANES), lambda b_, h, i: (b_, 0, cv + h)),
        ],
        out_specs=pl.BlockSpec((1, tq, LANES), lambda b_, h, i: (b_, i, h)),
        out_shape=jax.ShapeDtypeStruct((b, s, BRANCH_WIDTH), BF16),
        scratch_shapes=[pltpu.VMEM((tq, LANES), F32), pltpu.VMEM((tq, LANES), F32)],
        compiler_params=_cparams(("parallel", "parallel", "arbitrary"), 40),
        name="sb_attention",
    )(u, u, u)


def _diff_kernel(q1_ref, q2_ref, k1_ref, k2_ref, v_ref, lq1_ref, lk1_ref, lq2_ref, lk2_ref, g_ref, o_ref,
                 m1_ref, l1_ref, a1_ref, m2_ref, l2_ref, a2_ref, *, t, lam_init):
    h = pl.program_id(1)
    qi = pl.program_id(2)
    c = DIFF_DIM ** -0.5 * LOG2E
    mine = (lax.broadcasted_iota(jnp.int32, (t, LANES), 1) // DIFF_DIM) == (h % 2)
    q1 = jnp.where(mine, q1_ref[0], 0.0).astype(BF16)
    q2 = jnp.where(mine, q2_ref[0], 0.0).astype(BF16)
    _flash_init(m1_ref, l1_ref, a1_ref)
    _flash_init(m2_ref, l2_ref, a2_ref)

    def chunk(c0, mask):
        v = v_ref[0, pl.ds(c0, t), :]
        for q, k_ref, m_r, l_r, a_r in ((q1, k1_ref, m1_ref, l1_ref, a1_ref), (q2, k2_ref, m2_ref, l2_ref, a2_ref)):
            a = _dot_nt(q, k_ref[0, pl.ds(c0, t), :]) * c
            if mask is not None:
                a = jnp.where(mask, a, NEG_INF)
            _flash_step(a, v, m_r, l_r, a_r)

    def body(j, carry):
        chunk(pl.multiple_of(j * t, t), None)
        return carry

    lax.fori_loop(0, qi, body, 0)
    chunk(pl.multiple_of(qi * t, t), _causal_diag(t))
    lam = (jnp.exp(jnp.sum(lq1_ref[...] * lk1_ref[...], axis=1, keepdims=True))
           - jnp.exp(jnp.sum(lq2_ref[...] * lk2_ref[...], axis=1, keepdims=True)) + lam_init)
    o = a1_ref[...] / l1_ref[...] - lam * (a2_ref[...] / l2_ref[...])
    y = o * lax.rsqrt(jnp.mean(o * o, axis=-1, keepdims=True) + NORM_EPS)
    o_ref[0] = ((y * g_ref[...]) * (1.0 - lam_init)).astype(o_ref.dtype)


def diff_attention(u_d, u_p, lq1, lk1, lq2, lk2, g, lam_init, t):
    b, s, _ = u_d.shape
    cq1, cq2, ck1, ck2, cv = COL_D["dq1"], COL_D["dq2"], COL_D["dk1"], COL_D["dk2"], COL_P["dv"]
    vec64 = pl.BlockSpec((1, DIFF_DIM), lambda b_, h, i: (0, 0))
    stat = pltpu.VMEM((t, LANES), F32)
    return pl.pallas_call(
        functools.partial(_diff_kernel, t=t, lam_init=lam_init),
        grid=(b, N_HEADS, s // t),
        in_specs=[
            pl.BlockSpec((1, t, LANES), lambda b_, h, i: (b_, i, cq1 + h // 2)),
            pl.BlockSpec((1, t, LANES), lambda b_, h, i: (b_, i, cq2 + h // 2)),
            pl.BlockSpec((1, s, LANES), lambda b_, h, i: (b_, 0, ck1 + h // 2)),
            pl.BlockSpec((1, s, LANES), lambda b_, h, i: (b_, 0, ck2 + h // 2)),
            pl.BlockSpec((1, s, LANES), lambda b_, h, i: (b_, 0, cv + h)),
            vec64, vec64, vec64, vec64,
            pl.BlockSpec((1, HEAD_DIM), lambda b_, h, i: (0, 0)),
        ],
        out_specs=pl.BlockSpec((1, t, LANES), lambda b_, h, i: (b_, i, h)),
        out_shape=jax.ShapeDtypeStruct((b, s, BRANCH_WIDTH), BF16),
        scratch_shapes=[stat] * 6,
        compiler_params=_cparams(("parallel", "parallel", "arbitrary"), 40),
        name="diff_attention",
    )(u_d, u_d, u_d, u_d, u_p,
      lq1.reshape(1, DIFF_DIM), lk1.reshape(1, DIFF_DIM), lq2.reshape(1, DIFF_DIM), lk2.reshape(1, DIFF_DIM),
      g.reshape(1, HEAD_DIM))


def _compress_kernel(kt_ref, vt_ref, pek_ref, pev_ref, wk_ref, wv_ref, kc_ref, vc_ref,
                     xs_ref, xa_ref, xb_ref, *, n_blk):
    half = CMP_BLOCK // 2

    def compress(x, pe_ref, w_ref):
        xs_ref[...] = x.astype(F32)
        for r in range(half):
            piece = xs_ref[pl.ds(r, n_blk, stride=half), :]
            xa_ref[:, r * LANES:(r + 1) * LANES] = (piece + pe_ref[pl.ds(r, 1), :]).astype(BF16)
            xb_ref[:, r * LANES:(r + 1) * LANES] = (piece + pe_ref[pl.ds(half + r, 1), :]).astype(BF16)
        first = _dot(xa_ref[...], w_ref[0])
        second = _dot(xb_ref[...], w_ref[1])
        return first + pltpu.roll(second, n_blk - 1, axis=0)

    for g in range(N_KV_NSA):
        sl = slice(g * LANES, (g + 1) * LANES)
        kc_ref[0, g] = compress(kt_ref[0][:, sl], pek_ref, wk_ref).astype(BF16)
        vc_ref[0, g] = compress(vt_ref[0][:, sl], pev_ref, wv_ref).astype(BF16)


def nsa_compress(u_r, u_p, pe_k, pe_v, w_ck, w_cv):
    b, s, _ = u_r.shape
    n_blk = s // CMP_STRIDE
    half = CMP_BLOCK // 2
    wk = w_ck.reshape(2, half * HEAD_DIM, HEAD_DIM).astype(BF16)
    wv = w_cv.reshape(2, half * HEAD_DIM, HEAD_DIM).astype(BF16)
    pe = pl.BlockSpec((CMP_BLOCK, HEAD_DIM), lambda i: (0, 0))
    wspec = pl.BlockSpec((2, half * HEAD_DIM, HEAD_DIM), lambda i: (0, 0, 0))
    out = pl.BlockSpec((1, N_KV_NSA, n_blk, HEAD_DIM), lambda i: (i, 0, 0, 0))
    return pl.pallas_call(
        functools.partial(_compress_kernel, n_blk=n_blk),
        grid=(b,),
        in_specs=[pl.BlockSpec((1, s, 2 * LANES), lambda i: (i, 0, COL_R["nkc"] // 2)),
                  pl.BlockSpec((1, s, 2 * LANES), lambda i: (i, 0, COL_P["nvc"] // 2)),
                  pe, pe, wspec, wspec],
        out_specs=[out, out],
        out_shape=[jax.ShapeDtypeStruct((b, N_KV_NSA, n_blk, HEAD_DIM), BF16)] * 2,
        scratch_shapes=[pltpu.VMEM((s, LANES), F32), pltpu.VMEM((n_blk, half * LANES), BF16),
                        pltpu.VMEM((n_blk, half * LANES), BF16)],
        compiler_params=_cparams(("parallel",), 40),
        name="nsa_compress",
    )(u_r, u_p, pe_k, pe_v, wk, wv)


def _cmp_attn_kernel(q_ref, kc_ref, vc_ref, ov_ref, oc_ref, mt_ref, *, tq, n_blk, n_sel):
    qi = pl.program_id(1)
    q0 = qi * tq
    scale = HEAD_DIM ** -0.5
    hpg = N_HEADS // N_KV_NSA
    t_row = q0 + lax.broadcasted_iota(jnp.int32, (tq, n_blk), 0)
    n_col = lax.broadcasted_iota(jnp.int32, (tq, n_blk), 1)
    valid = n_col * CMP_STRIDE + (CMP_BLOCK - 1) <= t_row
    t_lane = q0 + lax.broadcasted_iota(jnp.int32, (n_blk, tq), 1)
    n_sub = lax.broadcasted_iota(jnp.int32, (n_blk, tq), 0)
    valid_t = n_sub * CMP_STRIDE + (CMP_BLOCK - 1) <= t_lane
    j_idx = lax.broadcasted_iota(jnp.int32, (n_sel, tq), 0)
    cur = (q0 + lax.broadcasted_iota(jnp.int32, (n_sel, tq), 1)) // SEL_BLOCK
    for g in range(N_KV_NSA):
        kc = kc_ref[0, g]
        vc = vc_ref[0, g]
        p_sum_t = jnp.zeros((n_blk, tq), F32)
        for hh in range(hpg):
            sl = slice((g * hpg + hh) * LANES, (g * hpg + hh + 1) * LANES)
            q = q_ref[0][:, sl]
            s = jnp.where(valid, _dot_nt(q, kc) * scale, NEG_INF)
            m = jnp.max(s, axis=1, keepdims=True)
            m = jnp.where(m == NEG_INF, 0.0, m)
            e = jnp.exp(s - m)
            p = e / jnp.maximum(jnp.sum(e, axis=1, keepdims=True), 1e-30)
            oc_ref[0, :, sl] = _dot(p.astype(BF16), vc)
            st = jnp.where(valid_t, _dot_nt(kc, q) * scale, NEG_INF)
            mt = jnp.max(st, axis=0, keepdims=True)
            mt = jnp.where(mt == NEG_INF, 0.0, mt)
            et = jnp.exp(st - mt)
            p_sum_t = p_sum_t + et / jnp.maximum(jnp.sum(et, axis=0, keepdims=True), 1e-30)
        hi = p_sum_t.astype(BF16)
        lo = (p_sum_t - hi.astype(F32)).astype(BF16)
        imp = _dot(ov_ref[...], hi) + _dot(ov_ref[...], lo)
        imp = jnp.where((j_idx == 0) | (j_idx == cur) | (j_idx == cur - 1), FORCE_SCORE, imp)
        imp = jnp.where(j_idx <= cur, imp, NEG_INF)
        rank = jnp.zeros((n_sel, tq), F32)
        for jp in range(n_sel):
            other = imp[jp:jp + 1, :]
            beats = (other > imp) | ((other == imp) & (j_idx > jp))
            rank = rank + beats.astype(F32)
        mt_ref[0, g] = (rank < float(min(SEL_TOP_N, n_sel))).astype(F32)


def nsa_cmp_attention(u_r, kc, vc, tq):
    b, s, _ = u_r.shape
    n_blk = s // CMP_STRIDE
    n_sel = s // SEL_BLOCK
    ci = jnp.arange(n_blk)[None, :] * CMP_STRIDE
    sj = jnp.arange(n_sel)[:, None] * SEL_BLOCK
    overlap_t = ((ci < sj + SEL_BLOCK) & (ci + CMP_BLOCK > sj)).astype(BF16)
    cblk = pl.BlockSpec((1, N_KV_NSA, n_blk, HEAD_DIM), lambda b_, i: (b_, 0, 0, 0))
    return pl.pallas_call(
        functools.partial(_cmp_attn_kernel, tq=tq, n_blk=n_blk, n_sel=n_sel),
        grid=(b, s // tq),
        in_specs=[pl.BlockSpec((1, tq, BRANCH_WIDTH), lambda b_, i: (b_, i, COL_R["nq"] // N_HEADS)), cblk, cblk,
                  pl.BlockSpec((n_sel, n_blk), lambda b_, i: (0, 0))],
        out_specs=[pl.BlockSpec((1, tq, BRANCH_WIDTH), lambda b_, i: (b_, i, 0)),
                   pl.BlockSpec((1, N_KV_NSA, n_sel, tq), lambda b_, i: (b_, 0, 0, i))],
        out_shape=[jax.ShapeDtypeStruct((b, s, BRANCH_WIDTH), F32),
                   jax.ShapeDtypeStruct((b, N_KV_NSA, n_sel, s), F32)],
        compiler_params=_cparams(("parallel", "parallel"), 40),
        name="nsa_cmp_attention",
    )(u_r, kc, vc, overlap_t)


def _sel_kernel(q_ref, k_ref, vt_ref, mt_ref, o_ref, m_ref, l_ref, acc_ref, *, t):
    qi = pl.program_id(1)
    c = HEAD_DIM ** -0.5 * LOG2E
    hpg = N_HEADS // N_KV_NSA
    per_tile = t // SEL_BLOCK
    causal_t = lax.broadcasted_iota(jnp.int32, (t, t), 0) <= lax.broadcasted_iota(jnp.int32, (t, t), 1)
    for g in range(N_KV_NSA):
        qs = [q_ref[0][:, (g * hpg + hh) * LANES:(g * hpg + hh + 1) * LANES] for hh in range(hpg)]
        _flash_init(m_ref, l_ref, acc_ref)
        gl = slice(g * LANES, (g + 1) * LANES)

        def tile(j, diag, g=g, qs=qs, gl=gl):
            c0 = pl.multiple_of(j * t, t)
            k = k_ref[0, pl.ds(c0, t), gl]
            vt = vt_ref[0, gl, pl.ds(c0, t)]
            rows = [jnp.broadcast_to(mt_ref[0, g, pl.ds(j * per_tile + i, 1), :], (SEL_BLOCK, t))
                    for i in range(per_tile)]
            mask = jnp.concatenate(rows, axis=0) > 0.5
            if diag:
                mask = mask & causal_t
            for hh in range(hpg):
                at = jnp.where(mask, _dot_nt(k, qs[hh]) * c, NEG_INF)
                m_prev = m_ref[hh]
                m_new = jnp.maximum(m_prev, jnp.max(at, axis=0, keepdims=True))
                alpha = jnp.exp2(m_prev - m_new)
                p = jnp.exp2(at - m_new)
                l_ref[hh] = alpha * l_ref[hh] + jnp.sum(p, axis=0, keepdims=True)
                acc_ref[hh] = alpha * acc_ref[hh] + _dot(vt, p.astype(BF16))
                m_ref[hh] = m_new

        def body(j, carry):
            tile(j, False)
            return carry

        lax.fori_loop(0, qi, body, 0)
        tile(qi, True)
        for hh in range(hpg):
            sl = slice((g * hpg + hh) * LANES, (g * hpg + hh + 1) * LANES)
            o_ref[0, :, sl] = (acc_ref[hh] / l_ref[hh]).T


def nsa_sel_attention(u_r, vs_t, mt, t):
    b, s, _ = u_r.shape
    n_sel = s // SEL_BLOCK
    hpg = N_HEADS // N_KV_NSA
    return pl.pallas_call(
        functools.partial(_sel_kernel, t=t),
        grid=(b, s // t),
        in_specs=[pl.BlockSpec((1, t, BRANCH_WIDTH), lambda b_, i: (b_, i, COL_R["nq"] // N_HEADS)),
                  pl.BlockSpec((1, s, 2 * LANES), lambda b_, i: (b_, 0, COL_R["nks"] // 2)),
                  pl.BlockSpec((1, 2 * LANES, s), lambda b_, i: (b_, 0, 0)),
                  pl.BlockSpec((1, N_KV_NSA, n_sel, t), lambda b_, i: (b_, 0, 0, i))],
        out_specs=pl.BlockSpec((1, t, BRANCH_WIDTH), lambda b_, i: (b_, i, 0)),
        out_shape=jax.ShapeDtypeStruct((b, s, BRANCH_WIDTH), F32),
        scratch_shapes=[pltpu.VMEM((hpg, 1, t), F32), pltpu.VMEM((hpg, 1, t), F32),
                        pltpu.VMEM((hpg, HEAD_DIM, t), F32)],
        compiler_params=_cparams(("parallel", "arbitrary"), 40),
        name="nsa_sel_attention",
    )(u_r, u_r, vs_t, mt)


def _win_kernel(q_ref, k_ref, kp_ref, v_ref, vp_ref, oc_ref, os_ref, small_ref, o_ref, m_ref, l_ref, acc_ref, *, t):
    qi = pl.program_id(1)
    c = HEAD_DIM ** -0.5 * LOG2E
    hpg = N_HEADS // N_KV_NSA
    row = lax.broadcasted_iota(jnp.int32, (t, t), 0)
    col = lax.broadcasted_iota(jnp.int32, (t, t), 1)
    gates = jax.nn.sigmoid(small_ref[0])
    for g in range(N_KV_NSA):
        gl = slice(g * LANES, (g + 1) * LANES)
        for hh in range(hpg):
            head = g * hpg + hh
            sl = slice(head * LANES, (head + 1) * LANES)
            q = q_ref[0][:, sl]
            _flash_init(m_ref, l_ref, acc_ref)
            a = jnp.where(col <= row, _dot_nt(q, k_ref[0][:, gl]) * c, NEG_INF)
            _flash_step(a, v_ref[0][:, gl], m_ref, l_ref, acc_ref)

            @pl.when(qi > 0)
            def _():
                ap = jnp.where(col > row, _dot_nt(q, kp_ref[0][:, gl]) * c, NEG_INF)
                _flash_step(ap, vp_ref[0][:, gl], m_ref, l_ref, acc_ref)

            o_w = acc_ref[...] / l_ref[...]
            cg = 4 + 3 * head
            o = (gates[:, cg:cg + 1] * oc_ref[0][:, sl] + gates[:, cg + 1:cg + 2] * os_ref[0][:, sl]
                 + gates[:, cg + 2:cg + 3] * o_w)
            o_ref[0, :, sl] = o.astype(o_ref.dtype)


def nsa_win_attention(u_r, u_p, o_c, o_s, small, t):
    b, s, _ = u_r.shape
    assert t == WINDOW
    wide = pl.BlockSpec((1, t, BRANCH_WIDTH), lambda b_, i: (b_, i, 0))
    ck, cv = COL_R["nkw"] // 2, COL_P["nvw"] // 2
    stat = pltpu.VMEM((t, LANES), F32)
    return pl.pallas_call(
        functools.partial(_win_kernel, t=t),
        grid=(b, s // t),
        in_specs=[pl.BlockSpec((1, t, BRANCH_WIDTH), lambda b_, i: (b_, i, COL_R["nq"] // N_HEADS)),
                  pl.BlockSpec((1, t, 2 * LANES), lambda b_, i: (b_, i, ck)),
                  pl.BlockSpec((1, t, 2 * LANES), lambda b_, i: (b_, jnp.maximum(i - 1, 0), ck)),
                  pl.BlockSpec((1, t, 2 * LANES), lambda b_, i: (b_, i, cv)),
                  pl.BlockSpec((1, t, 2 * LANES), lambda b_, i: (b_, jnp.maximum(i - 1, 0), cv)),
                  wide, wide,
                  pl.BlockSpec((1, t, LANES), lambda b_, i: (b_, i, 0))],
        out_specs=wide,
        out_shape=jax.ShapeDtypeStruct((b, s, BRANCH_WIDTH), BF16),
        scratch_shapes=[stat, stat, stat],
        compiler_params=_cparams(("parallel", "arbitrary"), 48),
        name="nsa_win_attention",
    )(u_r, u_r, u_r, u_p, u_p, o_c, o_s, small)


def _merge_kernel(h_ref, o0_ref, o1_ref, o2_ref, o3_ref, wg_ref, wb_ref, out_ref):
    h = h_ref[...]
    acc = None
    for n, o_ref in enumerate((o0_ref, o1_ref, o2_ref, o3_ref)):
        term = jax.nn.sigmoid(_dot(h, wg_ref[n])) * _dot(o_ref[...], wb_ref[n])
        acc = term if acc is None else acc + term
    out_ref[...] = acc.astype(out_ref.dtype)


def merge_branches(h, branches, wg, wb, tm, tn):
    n, d = h.shape
    bspec = pl.BlockSpec((tm, BRANCH_WIDTH), lambda j, i: (i, 0))
    return pl.pallas_call(
        _merge_kernel,
        grid=(d // tn, n // tm),
        in_specs=[pl.BlockSpec((tm, d), lambda j, i: (i, 0)), bspec, bspec, bspec, bspec,
                  pl.BlockSpec((4, d, tn), lambda j, i: (0, 0, j)),
                  pl.BlockSpec((4, BRANCH_WIDTH, tn), lambda j, i: (0, 0, j))],
        out_specs=pl.BlockSpec((tm, tn), lambda j, i: (i, j)),
        out_shape=jax.ShapeDtypeStruct((n, d), BF16),
        compiler_params=_cparams(("parallel", "arbitrary"), 56),
        name="merge_branches",
    )(h, *branches, wg, wb)


def _out_kernel(mg_ref, x_ref, w_ref, g_ref, wr_ref, br_ref, x1_ref, h2_ref, route_ref, *, tm):
    x1 = x_ref[...] + _dot(mg_ref[...], w_ref[...])
    x1_ref[...] = x1
    h2 = (x1 * lax.rsqrt(jnp.mean(x1 * x1, axis=-1, keepdims=True) + NORM_EPS)) * g_ref[...]
    for s in range(ROW_TILES):
        h2_ref[pl.ds(s, tm, stride=ROW_TILES), :] = h2[:, s * LANES:(s + 1) * LANES]
    logits = _dot(h2.astype(BF16), wr_ref[...]) + br_ref[...]
    lane = lax.broadcasted_iota(jnp.int32, logits.shape, 1)
    lane_f = lane.astype(F32)
    big = float(LANES)
    is_g = lane < N_GROUPS
    lg = jnp.where(is_g, logits, NEG_INF)
    mx = jnp.max(lg, axis=1, keepdims=True)
    gi = jnp.min(jnp.where(lg == mx, lane_f, big), axis=1, keepdims=True)
    pg = 1.0 / jnp.sum(jnp.where(is_g, jnp.exp(lg - mx), 0.0), axis=1, keepdims=True)
    e_idx = lane - N_GROUPS
    in_grp = (lane >= N_GROUPS) & (lane < N_GROUPS + N_EXPERTS) & ((e_idx // EXPERTS_PER_GROUP).astype(F32) == gi)
    le = jnp.where(in_grp, logits, NEG_INF)
    v1 = jnp.max(le, axis=1, keepdims=True)
    i1 = jnp.min(jnp.where(le == v1, lane_f, big), axis=1, keepdims=True)
    le2 = jnp.where(lane_f == i1, NEG_INF, le)
    v2 = jnp.max(le2, axis=1, keepdims=True)
    i2 = jnp.min(jnp.where(le2 == v2, lane_f, big), axis=1, keepdims=True)
    e2 = jnp.exp(v2 - v1)
    w1 = pg / (1.0 + e2)
    w2 = pg * e2 / (1.0 + e2)
    route = jnp.where(lane == 0, i1 - N_GROUPS, 0.0)
    route = jnp.where(lane == 1, i2 - N_GROUPS, route)
    route = jnp.where(lane == 2, w1, route)
    route = jnp.where(lane == 3, w2, route)
    route_ref[...] = route


def out_proj_router(merged, x, w_out, g2, w_router, b_router, tm):
    n, d = x.shape
    row = pl.BlockSpec((tm, d), lambda i: (i, 0))
    return pl.pallas_call(
        functools.partial(_out_kernel, tm=tm),
        grid=(n // tm,),
        in_specs=[row, row, pl.BlockSpec((d, d), lambda i: (0, 0)), pl.BlockSpec((1, d), lambda i: (0, 0)),
                  pl.BlockSpec((d, LANES), lambda i: (0, 0)), pl.BlockSpec((1, LANES), lambda i: (0, 0))],
        out_specs=[row, pl.BlockSpec((tm * ROW_TILES, LANES), lambda i: (i, 0)),
                   pl.BlockSpec((tm, LANES), lambda i: (i, 0))],
        out_shape=[jax.ShapeDtypeStruct((n, d), F32), jax.ShapeDtypeStruct((n * ROW_TILES, LANES), F32),
                   jax.ShapeDtypeStruct((n, LANES), F32)],
        compiler_params=_cparams(("parallel",), 56),
        name="out_proj_router",
    )(merged, x, w_out, g2.reshape(1, d), w_router, b_router)


def _row_copy(src_hbm, src_row, buf_ref, r, sem):
    src = pl.multiple_of(src_row * ROW_TILES, ROW_TILES)
    return pltpu.make_async_copy(src_hbm.at[pl.ds(src, ROW_TILES), :],
                                 buf_ref.at[pl.ds(r * ROW_TILES, ROW_TILES), :], sem)


def _start_rows(src_hbm, idx_ref, n_rows, buf_ref, sem):
    for r in range(n_rows):
        _row_copy(src_hbm, idx_ref[0, 0, r], buf_ref, r, sem).start(priority=r % 2)


def _wait_rows(src_hbm, buf_ref, sem):
    pltpu.make_async_copy(src_hbm.at[pl.ds(0, buf_ref.shape[0]), :], buf_ref, sem).wait()


def _ffn_kernel(blk_e_ref, n_used_ref, tok_ref, tok_next_ref, sw_ref, h2_hbm, wg32_ref, wu32_ref, wd32_ref, y_ref,
                xa_ref, xb_ref, wg_ref, wu_ref, wd_ref, sem, *, tm):
    blk = pl.program_id(0)
    n_used = n_used_ref[0]
    even = blk % 2 == 0

    @pl.when(blk == 0)
    def _():
        _start_rows(h2_hbm, tok_ref, tm, xa_ref, sem.at[0])

    @pl.when((blk == 0) | (blk_e_ref[blk] != blk_e_ref[jnp.maximum(blk - 1, 0)]))
    def _():
        rows = 256
        for src, dst in ((wg32_ref, wg_ref), (wu32_ref, wu_ref), (wd32_ref, wd_ref)):
            for r0 in range(0, dst.shape[1], rows):
                dst[0, r0:r0 + rows, :] = src[0, 0, r0:r0 + rows, :].astype(BF16)

    def work(cur_ref, cur_sem, nxt_ref, nxt_sem):
        _wait_rows(h2_hbm, cur_ref, cur_sem)
        _start_rows(h2_hbm, tok_next_ref, tm, nxt_ref, nxt_sem)
        x = jnp.concatenate([cur_ref[pl.ds(s, tm, stride=ROW_TILES), :] for s in range(ROW_TILES)],
                            axis=1).astype(BF16)
        gate = _dot(x, wg_ref[0])
        up = _dot(x, wu_ref[0])
        hid = (gate * jax.nn.sigmoid(gate) * up).astype(BF16)
        y = _dot(hid, wd_ref[0]) * sw_ref[...]
        for s in range(ROW_TILES):
            y_ref[pl.ds(s, tm, stride=ROW_TILES), :] = y[:, s * LANES:(s + 1) * LANES]

    @pl.when((blk < n_used) & even)
    def _():
        work(xa_ref, sem.at[0], xb_ref, sem.at[1])

    @pl.when((blk < n_used) & jnp.logical_not(even))
    def _():
        work(xb_ref, sem.at[1], xa_ref, sem.at[0])

    @pl.when(blk >= n_used)
    def _():
        y_ref[...] = jnp.zeros(y_ref.shape, F32)

    @pl.when((blk == n_used) & even)
    def _():
        _wait_rows(h2_hbm, xa_ref, sem.at[0])

    @pl.when((blk == n_used) & jnp.logical_not(even))
    def _():
        _wait_rows(h2_hbm, xb_ref, sem.at[1])


def moe_experts(blk_e, n_used, buf_tok, buf_w, h2_rows, w_g, w_u, w_d, layer, tm):
    n_blk = blk_e.shape[0]
    d, ff = w_g.shape[2], w_g.shape[3]
    tok = buf_tok.reshape(n_blk, 1, tm)
    once = pl.Buffered(1)
    grid_spec = pltpu.PrefetchScalarGridSpec(
        num_scalar_prefetch=2,
        grid=(n_blk,),
        in_specs=[
            pl.BlockSpec((1, 1, tm), lambda i, be, nu: (i, 0, 0), memory_space=pltpu.SMEM),
            pl.BlockSpec((1, 1, tm), lambda i, be, nu: (jnp.minimum(i + 1, n_blk - 1), 0, 0), memory_space=pltpu.SMEM),
            pl.BlockSpec((tm, 1), lambda i, be, nu: (i, 0)),
            pl.BlockSpec(memory_space=pl.ANY),
            pl.BlockSpec((1, 1, d, ff), lambda i, be, nu: (layer, be[i], 0, 0), pipeline_mode=once),
            pl.BlockSpec((1, 1, d, ff), lambda i, be, nu: (layer, be[i], 0, 0), pipeline_mode=once),
            pl.BlockSpec((1, 1, ff, d), lambda i, be, nu: (layer, be[i], 0, 0), pipeline_mode=once),
        ],
        out_specs=pl.BlockSpec((tm * ROW_TILES, LANES), lambda i, be, nu: (i, 0)),
        scratch_shapes=[pltpu.VMEM((tm * ROW_TILES, LANES), F32), pltpu.VMEM((tm * ROW_TILES, LANES), F32),
                        pltpu.VMEM((1, d, ff), BF16), pltpu.VMEM((1, d, ff), BF16), pltpu.VMEM((1, ff, d), BF16),
                        pltpu.SemaphoreType.DMA((2,))],
    )
    return pl.pallas_call(
        functools.partial(_ffn_kernel, tm=tm),
        grid_spec=grid_spec,
        out_shape=jax.ShapeDtypeStruct((n_blk * tm * ROW_TILES, LANES), F32),
        compiler_params=_cparams(("arbitrary",), 56),
        name="moe_experts",
    )(blk_e, n_used, tok, tok, buf_w.reshape(n_blk * tm, 1), h2_rows, w_g, w_u, w_d)


def _combine_kernel(pos_ref, pos_next_ref, x1_ref, yb_hbm, g_ref, x2_ref, hn_ref, ba_ref, bb_ref, sem, *, tm, n_tiles):
    i = pl.program_id(0)
    even = i % 2 == 0

    @pl.when(i == 0)
    def _():
        _start_rows(yb_hbm, pos_ref, 2 * tm, ba_ref, sem.at[0])

    def work(cur_ref, cur_sem, nxt_ref, nxt_sem):
        _wait_rows(yb_hbm, cur_ref, cur_sem)
        if nxt_ref is not None:
            _start_rows(yb_hbm, pos_next_ref, 2 * tm, nxt_ref, nxt_sem)
        pieces = []
        ssq = jnp.zeros((tm, 1), F32)
        for s in range(ROW_TILES):
            piece = (x1_ref[:, s * LANES:(s + 1) * LANES]
                     + cur_ref[pl.ds(s, tm, stride=2 * ROW_TILES), :]
                     + cur_ref[pl.ds(ROW_TILES + s, tm, stride=2 * ROW_TILES), :])
            x2_ref[:, s * LANES:(s + 1) * LANES] = piece
            ssq = ssq + jnp.sum(piece * piece, axis=1, keepdims=True)
            pieces.append(piece)
        inv = lax.rsqrt(ssq / D_MODEL + NORM_EPS)
        for s in range(ROW_TILES):
            sl = slice(s * LANES, (s + 1) * LANES)
            hn_ref[:, sl] = ((pieces[s] * inv) * g_ref[:, sl]).astype(hn_ref.dtype)

    @pl.when((i < n_tiles - 1) & even)
    def _():
        work(ba_ref, sem.at[0], bb_ref, sem.at[1])

    @pl.when((i < n_tiles - 1) & jnp.logical_not(even))
    def _():
        work(bb_ref, sem.at[1], ba_ref, sem.at[0])

    @pl.when(i == n_tiles - 1)
    def _():
        work(bb_ref, sem.at[1], None, None)


def moe_combine(pos, x1, yb_rows, g, hn_dtype, tm):
    n, d = x1.shape
    n_tiles = n // tm
    assert n_tiles % 2 == 0
    row = pl.BlockSpec((tm, d), lambda i: (i, 0))
    pos3 = pos.reshape(n_tiles, 1, 2 * tm)
    return pl.pallas_call(
        functools.partial(_combine_kernel, tm=tm, n_tiles=n_tiles),
        grid=(n_tiles,),
        in_specs=[pl.BlockSpec((1, 1, 2 * tm), lambda i: (i, 0, 0), memory_space=pltpu.SMEM),
                  pl.BlockSpec((1, 1, 2 * tm), lambda i: (jnp.minimum(i + 1, n_tiles - 1), 0, 0), memory_space=pltpu.SMEM),
                  row, pl.BlockSpec(memory_space=pl.ANY), pl.BlockSpec((1, d), lambda i: (0, 0))],
        out_specs=[row, row],
        out_shape=[jax.ShapeDtypeStruct((n, d), F32), jax.ShapeDtypeStruct((n, d), hn_dtype)],
        scratch_shapes=[pltpu.VMEM((2 * tm * ROW_TILES, LANES), F32), pltpu.VMEM((2 * tm * ROW_TILES, LANES), F32),
                        pltpu.SemaphoreType.DMA((2,))],
        compiler_params=_cparams(("arbitrary",), 48),
        name="moe_combine",
    )(pos3, pos3, x1, yb_rows, g.reshape(1, d))


def _dispatch_plan(route, tm):
    n = route.shape[0]
    m = 2 * n
    slot_e = route[:, 0:2].astype(jnp.int32).reshape(m)
    slot_w = route[:, 2:4].reshape(m)
    iota = jnp.arange(m, dtype=jnp.int32)
    se, order, sw = lax.sort((slot_e, iota, slot_w), num_keys=1)
    e_ids = jnp.arange(N_EXPERTS, dtype=jnp.int32)
    counts = jnp.sum((slot_e[None, :] == e_ids[:, None]).astype(jnp.int32), axis=1)
    start = jnp.cumsum(counts) - counts
    padded = (counts + tm - 1) // tm * tm
    pend = jnp.cumsum(padded)
    pstart = pend - padded
    delta = pstart - start
    dest = iota + jnp.sum(jnp.where(se[:, None] == e_ids[None, :], delta[None, :], 0), axis=1)
    _, pos = lax.sort((order, dest), num_keys=1)
    n_blk = (m + N_EXPERTS * tm) // tm
    blk_row0 = jnp.arange(n_blk, dtype=jnp.int32) * tm
    blk_e = jnp.minimum(jnp.sum((pend[None, :] <= blk_row0[:, None]).astype(jnp.int32), axis=1), N_EXPERTS - 1)
    onehot = blk_e[:, None] == e_ids[None, :]
    pick = lambda tab: jnp.sum(jnp.where(onehot, tab[None, :], 0), axis=1)
    local0 = blk_row0 - pick(pstart)
    n_valid = jnp.clip(pick(counts) - local0, 0, tm)
    within = jnp.arange(tm, dtype=jnp.int32)[None, :]
    src = jnp.clip(pick(start)[:, None] + local0[:, None] + within, 0, m - 1)
    valid = within < n_valid[:, None]
    buf_tok = jnp.where(valid, jnp.take(order, src) // 2, 0)
    buf_w = jnp.where(valid, jnp.take(sw, src), 0.0)
    n_used = (pend[-1] // tm).astype(jnp.int32).reshape(1)
    return blk_e.astype(jnp.int32), n_used, buf_tok, buf_w, pos


def _rope_tables(s):
    pos = jnp.arange(s, dtype=F32)[:, None]
    inv128 = jnp.exp(-math.log(ROPE_THETA) * jnp.arange(0, HEAD_DIM, 2, dtype=F32) / HEAD_DIM)
    a = pos * inv128[None, :]
    c128 = jnp.concatenate([jnp.cos(a), jnp.cos(a)], axis=1)
    s128 = jnp.concatenate([-jnp.sin(a), jnp.sin(a)], axis=1)
    inv64 = jnp.exp(-math.log(ROPE_THETA) * jnp.arange(0, DIFF_DIM, 2, dtype=F32) / DIFF_DIM)
    a = pos * inv64[None, :]
    co, si, z = jnp.cos(a), jnp.sin(a), jnp.zeros_like(a)
    c64 = jnp.concatenate([co, co, co, co], axis=1)
    sa64 = jnp.concatenate([-si, z, -si, z], axis=1)
    sb64 = jnp.concatenate([z, si, z, si], axis=1)
    return (c128, s128), (c64, sa64, sb64)


def _orig_offsets():
    offs, o = {}, 0
    for name, w in zip(_ORIG_NAMES, _ORIG_WIDTHS):
        offs[name] = (o, w)
        o += w
    return offs


def _w_in_prep_kernel(w_ref, p_ref, r_ref, d_ref, s_ref):
    offs = _orig_offsets()
    for pieces, dst_ref in ((_PLAIN, p_ref), (_ROPE128, r_ref), (_ROPE64, d_ref)):
        dst = 0
        for name, width in pieces:
            src = offs[name][0]
            dst_ref[:, dst:dst + width] = w_ref[0, :, src:src + width].astype(BF16)
            dst += width
    (f0, fw), (g0, gw) = offs["ff"], offs["ngt"]
    s_ref[...] = jnp.zeros(s_ref.shape, BF16)
    s_ref[:, 0:fw] = w_ref[0, :, f0:f0 + fw].astype(BF16)
    s_ref[:, fw:fw + gw] = w_ref[0, :, g0:g0 + gw].astype(BF16)


def split_w_in(w_in, layer, tr=256):
    _, k, c = w_in.shape
    row = lambda w: pl.BlockSpec((tr, w), lambda i: (i, 0))
    shape = lambda w: jax.ShapeDtypeStruct((k, w), BF16)
    return pl.pallas_call(
        _w_in_prep_kernel,
        grid=(k // tr,),
        in_specs=[pl.BlockSpec((1, tr, c), lambda i: (layer, i, 0))],
        out_specs=[row(C_PLAIN), row(C_ROPE128), row(C_ROPE64), row(LANES)],
        out_shape=[shape(C_PLAIN), shape(C_ROPE128), shape(C_ROPE64), shape(LANES)],
        compiler_params=_cparams(("parallel",), 48),
        name="w_in_prep",
    )(w_in)


def _tiles(s, n):
    return dict(t_attn=min(512, s), tq_nsa=min(256, s), tk_sb=min(256, s), tm_norm=min(512, n), tm_proj=min(512, n),
                tm_out=min(512, n), tm_comb=min(256, n), tm_moe=256)


def kernel(x, norm1_g, w_in, fox_bf, nsa_pe_k, nsa_pe_v, nsa_w_ck, nsa_w_cv, diff_lq1, diff_lk1, diff_lq2, diff_lk2, diff_norm_g, w_branch, w_mgate, w_out, norm2_g, w_rg, b_rg, w_re, b_re, w_eg, w_eu, w_ed, final_g):
    b, s, d = x.shape
    n = b * s
    depth = w_in.shape[0]
    t = _tiles(s, n)
    rope128_tabs, rope64_tabs = _rope_tables(s)
    xf = x.reshape(n, d)
    h = rms_norm_rows(xf, norm1_g[0], BF16, t["tm_norm"])
    out = None
    for l in range(depth):
        lam_init = 0.8 - 0.6 * math.exp(-0.3 * l)
        w_p, w_r, w_d, w_s = split_w_in(w_in, l)
        u_p = in_proj_plain(h, w_p, t["tm_proj"], C_PLAIN // 2).reshape(b, s, C_PLAIN)
        u_r, u_d, small = in_proj_rope(h, w_r, w_d, w_s, rope128_tabs, rope64_tabs, s, t["tm_proj"])
        u_r, u_d, small = u_r.reshape(b, s, C_ROPE128), u_d.reshape(b, s, C_ROPE64), small.reshape(b, s, LANES)
        cum = forget_cumsum(small[:, :, 0:N_HEADS].transpose(0, 2, 1), fox_bf[l])
        o_fox = fox_attention(u_p, cum, t["t_attn"])
        o_sb = sb_attention(u_p, t["t_attn"], t["tk_sb"])
        o_diff = diff_attention(u_d, u_p, diff_lq1[l], diff_lk1[l], diff_lq2[l], diff_lk2[l], diff_norm_g[l],
                                lam_init, t["t_attn"])
        kc, vc = nsa_compress(u_r, u_p, nsa_pe_k[l], nsa_pe_v[l], nsa_w_ck[l], nsa_w_cv[l])
        o_c, mt = nsa_cmp_attention(u_r, kc, vc, t["tq_nsa"])
        nvs0 = COL_P["nvs"] * LANES
        vs_t = u_p[:, :, nvs0:nvs0 + 2 * LANES].transpose(0, 2, 1)
        o_s = nsa_sel_attention(u_r, vs_t, mt, t["t_attn"])
        o_nsa = nsa_win_attention(u_r, u_p, o_c, o_s, small, t["t_attn"])
        branches = [o.reshape(n, BRANCH_WIDTH) for o in (o_fox, o_nsa, o_sb, o_diff)]
        merged = merge_branches(h, branches, w_mgate[l].astype(BF16), w_branch[l].astype(BF16), t["tm_proj"], 512)
        w_router = jnp.concatenate([w_rg[l], w_re[l], jnp.zeros((d, LANES - N_GROUPS - N_EXPERTS), F32)], axis=1).astype(BF16)
        b_router = jnp.concatenate([b_rg[l], b_re[l], jnp.zeros((LANES - N_GROUPS - N_EXPERTS,), F32)]).reshape(1, LANES)
        x1, h2_rows, route = out_proj_router(merged, xf, w_out[l].astype(BF16), norm2_g[l], w_router, b_router, t["tm_out"])
        blk_e, n_used, buf_tok, buf_w, pos = _dispatch_plan(route, t["tm_moe"])
        yb_rows = moe_experts(blk_e, n_used, buf_tok, buf_w, h2_rows, w_eg, w_eu, w_ed, l, t["tm_moe"])
        last = l == depth - 1
        g_next = final_g if last else norm1_g[l + 1]
        xf, hn = moe_combine(pos, x1, yb_rows, g_next, F32 if last else BF16, t["tm_comb"])
        h = hn
        out = hn
    return out.reshape(b, s, d)
```

```python
import functools
import math

import jax
import jax.numpy as jnp
from jax import lax
from jax.experimental import pallas as pl
from jax.experimental.pallas import tpu as pltpu

F32 = jnp.float32
BF16 = jnp.bfloat16
NEG_INF = float("-inf")
LOG2E = 1.4426950408889634

D_MODEL = 2048
HEAD_DIM = 128
DIFF_DIM = 64
N_HEADS = 4
N_KV_NSA = 2
BRANCH_WIDTH = 512
ROPE_THETA = 10000.0
NORM_EPS = 1e-6
CMP_BLOCK = 32
CMP_STRIDE = 16
SEL_BLOCK = 64
SEL_TOP_N = 16
WINDOW = 512
FORCE_SCORE = 1e6
N_GROUPS = 4
EXPERTS_PER_GROUP = 8
N_EXPERTS = 32
EXPERT_FF = 1024

LANES = 128
ROW_TILES = D_MODEL // LANES
MIB = 1024 * 1024

_PLAIN = (("fq", 512), ("fk", 512), ("fv", 512), ("nvc", 256), ("nvs", 256), ("nvw", 256),
          ("sq", 512), ("sk", 512), ("sv", 512), ("dv", 512))
_ROPE128 = (("nq", 512), ("nkc", 256), ("nks", 256), ("nkw", 256))
_ROPE64 = (("dq1", 256), ("dq2", 256), ("dk1", 256), ("dk2", 256))


def _layout(pieces):
    col, off = {}, 0
    for name, w in pieces:
        col[name] = off // LANES
        off += w
    return col, off


COL_P, C_PLAIN = _layout(_PLAIN)
COL_R, C_ROPE128 = _layout(_ROPE128)
COL_D, C_ROPE64 = _layout(_ROPE64)

_ORIG_WIDTHS = (512, 512, 512, 4, 512, 256, 256, 256, 256, 256, 256, 12, 512, 512, 512, 256, 256, 256, 256, 512)
_ORIG_NAMES = ("fq", "fk", "fv", "ff", "nq", "nkc", "nvc", "nks", "nvs", "nkw", "nvw", "ngt",
               "sq", "sk", "sv", "dq1", "dq2", "dk1", "dk2", "dv")


def _cparams(sem, vmem_mib):
    return pltpu.CompilerParams(dimension_semantics=sem, vmem_limit_bytes=vmem_mib * MIB)


def _log_sigmoid(z):
    return jnp.minimum(z, 0.0) - jnp.log1p(jnp.exp(-jnp.abs(z)))


def _dot_nt(a, b):
    return lax.dot_general(a, b, (((1,), (1,)), ((), ())), preferred_element_type=F32)


def _dot(a, b):
    return jnp.dot(a, b, preferred_element_type=F32)


def _lanes(x, width):
    return x if width == LANES else jnp.concatenate([x] * (width // LANES), axis=1)


def _norm_kernel(x_ref, g_ref, o_ref):
    x = x_ref[...]
    y = x * lax.rsqrt(jnp.mean(x * x, axis=-1, keepdims=True) + NORM_EPS)
    o_ref[...] = (y * g_ref[...]).astype(o_ref.dtype)


def rms_norm_rows(x, g, out_dtype, tm):
    n, d = x.shape
    return pl.pallas_call(
        _norm_kernel,
        grid=(n // tm,),
        in_specs=[pl.BlockSpec((tm, d), lambda i: (i, 0)), pl.BlockSpec((1, d), lambda i: (0, 0))],
        out_specs=pl.BlockSpec((tm, d), lambda i: (i, 0)),
        out_shape=jax.ShapeDtypeStruct((n, d), out_dtype),
        compiler_params=_cparams(("parallel",), 40),
        name="rms_norm",
    )(x, g.reshape(1, d))


def _mm_kernel(a_ref, w_ref, o_ref):
    o_ref[...] = _dot(a_ref[...], w_ref[...]).astype(o_ref.dtype)


def in_proj_plain(h, w, tm, tn):
    m, k = h.shape
    c = w.shape[1]
    return pl.pallas_call(
        _mm_kernel,
        grid=(c // tn, m // tm),
        in_specs=[pl.BlockSpec((tm, k), lambda j, i: (i, 0)), pl.BlockSpec((k, tn), lambda j, i: (0, j))],
        out_specs=pl.BlockSpec((tm, tn), lambda j, i: (i, j)),
        out_shape=jax.ShapeDtypeStruct((m, c), BF16),
        compiler_params=_cparams(("parallel", "arbitrary"), 48),
        name="in_proj_plain",
    )(h, w)


def _proj_rope_kernel(h_ref, wr_ref, wd_ref, ws_ref, c128_ref, s128_ref, c64_ref, sa64_ref, sb64_ref,
                      r_ref, d_ref, small_ref):
    h = h_ref[...]
    acc = _dot(h, wr_ref[...])
    c, s = c128_ref[...], s128_ref[...]
    for blk in range(C_ROPE128 // LANES):
        x = acc[:, blk * LANES:(blk + 1) * LANES]
        r_ref[:, blk * LANES:(blk + 1) * LANES] = (x * c + pltpu.roll(x, 64, axis=1) * s).astype(BF16)
    acc = _dot(h, wd_ref[...])
    c, sa, sb = c64_ref[...], sa64_ref[...], sb64_ref[...]
    for blk in range(C_ROPE64 // LANES):
        x = acc[:, blk * LANES:(blk + 1) * LANES]
        d_ref[:, blk * LANES:(blk + 1) * LANES] = (
            x * c + pltpu.roll(x, 96, axis=1) * sa + pltpu.roll(x, 32, axis=1) * sb).astype(BF16)
    small_ref[...] = _dot(h, ws_ref[...])


def in_proj_rope(h, w_r, w_d, w_s, rope128_tabs, rope64_tabs, s, tm):
    m, k = h.shape
    per_seq = s // tm
    row = lambda w: pl.BlockSpec((tm, w), lambda i: (i, 0))
    full = lambda w: pl.BlockSpec((k, w), lambda i: (0, 0))
    tab = pl.BlockSpec((tm, LANES), lambda i: (i % per_seq, 0))
    return pl.pallas_call(
        _proj_rope_kernel,
        grid=(m // tm,),
        in_specs=[row(k), full(C_ROPE128), full(C_ROPE64), full(LANES), tab, tab, tab, tab, tab],
        out_specs=[row(C_ROPE128), row(C_ROPE64), row(LANES)],
        out_shape=[jax.ShapeDtypeStruct((m, C_ROPE128), BF16), jax.ShapeDtypeStruct((m, C_ROPE64), BF16),
                   jax.ShapeDtypeStruct((m, LANES), F32)],
        compiler_params=_cparams(("parallel",), 48),
        name="in_proj_rope",
    )(h, w_r, w_d, w_s, *rope128_tabs, *rope64_tabs)


def _cum_kernel(f_ref, b_ref, o_ref):
    z = f_ref[0] + b_ref[...]
    ls = _log_sigmoid(z)
    s = ls.shape[1]
    lane = lax.broadcasted_iota(jnp.int32, ls.shape, 1)
    sh = 1
    while sh < s:
        ls = ls + jnp.where(lane >= sh, pltpu.roll(ls, sh, axis=1), 0.0)
        sh *= 2
    o_ref[0] = ls * LOG2E


def forget_cumsum(f_t, bias):
    b, h, s = f_t.shape
    return pl.pallas_call(
        _cum_kernel,
        grid=(b,),
        in_specs=[pl.BlockSpec((1, h, s), lambda i: (i, 0, 0)), pl.BlockSpec((h, 1), lambda i: (0, 0))],
        out_specs=pl.BlockSpec((1, h, s), lambda i: (i, 0, 0)),
        out_shape=jax.ShapeDtypeStruct((b, h, s), F32),
        name="fox_cumsum",
    )(f_t, bias.reshape(h, 1))


def _flash_init(m_ref, l_ref, acc_ref):
    m_ref[...] = jnp.full(m_ref.shape, NEG_INF, F32)
    l_ref[...] = jnp.zeros(l_ref.shape, F32)
    acc_ref[...] = jnp.zeros(acc_ref.shape, F32)


def _flash_step(a, v, m_ref, l_ref, acc_ref):
    m_prev = m_ref[...]
    m_new = jnp.maximum(m_prev, jnp.max(a, axis=1, keepdims=True))
    alpha = jnp.exp2(m_prev - m_new)
    p = jnp.exp2(a - _lanes(m_new, a.shape[1]))
    l_ref[...] = alpha * l_ref[...] + jnp.sum(p, axis=1, keepdims=True)
    acc_ref[...] = alpha * acc_ref[...] + _dot(p.astype(BF16), v)
    m_ref[...] = m_new


def _causal_diag(t):
    return lax.broadcasted_iota(jnp.int32, (t, t), 1) <= lax.broadcasted_iota(jnp.int32, (t, t), 0)


def _fox_kernel(q_ref, k_ref, v_ref, cum_ref, o_ref, m_ref, l_ref, acc_ref, *, t):
    h = pl.program_id(1)
    qi = pl.program_id(2)
    c = HEAD_DIM ** -0.5 * LOG2E
    q = q_ref[0]

    def logits(j):
        return _dot_nt(q, k_ref[0, j * t:(j + 1) * t, :]) * c - cum_ref[0, pl.ds(h, 1), j * t:(j + 1) * t]

    def tile_body(n):
        _flash_init(m_ref, l_ref, acc_ref)
        a = logits(0)
        for j in range(n + 1):
            a_next = logits(j + 1) if j < n else None
            if j == n:
                a = jnp.where(_causal_diag(t), a, NEG_INF)
            _flash_step(a, v_ref[0, j * t:(j + 1) * t, :], m_ref, l_ref, acc_ref)
            a = a_next
        o_ref[0] = (acc_ref[...] / l_ref[...]).astype(o_ref.dtype)

    for n in range(k_ref.shape[1] // t):
        pl.when(qi == n)(functools.partial(tile_body, n))


def fox_attention(u, cum, t):
    b, s, _ = u.shape
    cq, ck, cv = COL_P["fq"], COL_P["fk"], COL_P["fv"]
    stat = pltpu.VMEM((t, LANES), F32)
    return pl.pallas_call(
        functools.partial(_fox_kernel, t=t),
        grid=(b, N_HEADS, s // t),
        in_specs=[
            pl.BlockSpec((1, t, LANES), lambda b_, h, i: (b_, i, cq + h)),
            pl.BlockSpec((1, s, LANES), lambda b_, h, i: (b_, 0, ck + h)),
            pl.BlockSpec((1, s, LANES), lambda b_, h, i: (b_, 0, cv + h)),
            pl.BlockSpec((1, N_HEADS, s), lambda b_, h, i: (b_, 0, 0)),
        ],
        out_specs=pl.BlockSpec((1, t, LANES), lambda b_, h, i: (b_, i, h)),
        out_shape=jax.ShapeDtypeStruct((b, s, BRANCH_WIDTH), BF16),
        scratch_shapes=[stat, stat, stat],
        compiler_params=_cparams(("parallel", "parallel", "arbitrary"), 40),
        name="fox_attention",
    )(u, u, u, cum)


def _sb_kernel(q_ref, k_ref, v_ref, o_ref, carry_ref, acc_ref, *, tq, tk):
    qi = pl.program_id(2)
    c = HEAD_DIM ** -0.5 * LOG2E
    q = q_ref[0]
    upper = (lax.broadcasted_iota(jnp.int32, (tk, tk), 0) > lax.broadcasted_iota(jnp.int32, (tk, tk), 1)).astype(BF16)
    per_tile = tq // tk

    def logits(c0):
        return _dot_nt(q, k_ref[0, c0:c0 + tk, :]) * c

    def chunk(z2, c0, mask):
        ls = jnp.minimum(z2, 0.0) - jnp.log2(1.0 + jnp.exp2(-jnp.abs(z2)))
        l_neg = ls - z2
        if mask is not None:
            l_neg = jnp.where(mask, l_neg, 0.0)
        hi = l_neg.astype(BF16)
        lo = (l_neg - hi.astype(F32)).astype(BF16)
        after = _lanes(carry_ref[...], tk) + _dot(hi, upper) + _dot(lo, upper)
        a = jnp.exp2(ls + after)
        if mask is not None:
            a = jnp.where(mask, a, 0.0)
        acc_ref[...] += _dot(a.astype(BF16), v_ref[0, c0:c0 + tk, :])
        carry_ref[...] += jnp.sum(l_neg, axis=1, keepdims=True)

    def tile_body(n):
        carry_ref[...] = jnp.zeros(carry_ref.shape, F32)
        acc_ref[...] = jnp.zeros(acc_ref.shape, F32)
        row = lax.broadcasted_iota(jnp.int32, (tq, tk), 0)
        col = lax.broadcasted_iota(jnp.int32, (tq, tk), 1)
        steps = [(n * tq + d * tk, d * tk + col < row) for d in reversed(range(per_tile))]
        steps += [(j * tk, None) for j in reversed(range(n * per_tile))]
        z2 = logits(steps[0][0])
        for i, (c0, mask) in enumerate(steps):
            z2_next = logits(steps[i + 1][0]) if i + 1 < len(steps) else None
            chunk(z2, c0, mask)
            z2 = z2_next
        o_ref[0] = acc_ref[...].astype(o_ref.dtype)

    for n in range(k_ref.shape[1] // tq):
        pl.when(qi == n)(functools.partial(tile_body, n))


def sb_attention(u, tq, tk):
    b, s, _ = u.shape
    cq, ck, cv = COL_P["sq"], COL_P["sk"], COL_P["sv"]
    return pl.pallas_call(
        functools.partial(_sb_kernel, tq=tq, tk=tk),
        grid=(b, N_HEADS, s // tq),
        in_specs=[
            pl.BlockSpec((1, tq, LANES), lambda b_, h, i: (b_, i, cq + h)),
            pl.BlockSpec((1, s, LANES), lambda b_, h, i: (b_, 0, ck + h)),
            pl.BlockSpec((1, s, LANES), lambda b_, h, i: (b_, 0, cv + h)),
        ],
        out_specs=pl.BlockSpec((1, tq, LANES), lambda b_, h, i: (b_, i, h)),
        out_shape=jax.ShapeDtypeStruct((b, s, BRANCH_WIDTH), BF16),
        scratch_shapes=[pltpu.VMEM((tq, LANES), F32), pltpu.VMEM((tq, LANES), F32)],
        compiler_params=_cparams(("parallel", "parallel", "arbitrary"), 40),
        name="sb_attention",
    )(u, u, u)


def _diff_kernel(q1_ref, q2_ref, k1_ref, k2_ref, v_ref, lq1_ref, lk1_ref, lq2_ref, lk2_ref, g_ref, o_ref,
                 m1_ref, l1_ref, a1_ref, m2_ref, l2_ref, a2_ref, *, t, lam_init):
    h = pl.program_id(1)
    qi = pl.program_id(2)
    c = DIFF_DIM ** -0.5 * LOG2E
    mine = (lax.broadcasted_iota(jnp.int32, (t, LANES), 1) // DIFF_DIM) == (h % 2)
    q1 = jnp.where(mine, q1_ref[0], 0.0).astype(BF16)
    q2 = jnp.where(mine, q2_ref[0], 0.0).astype(BF16)
    streams = ((q1, k1_ref, m1_ref, l1_ref, a1_ref), (q2, k2_ref, m2_ref, l2_ref, a2_ref))

    def logits(w, j):
        return _dot_nt(streams[w][0], streams[w][1][0, j * t:(j + 1) * t, :]) * c

    def tile_body(n):
        _flash_init(m1_ref, l1_ref, a1_ref)
        _flash_init(m2_ref, l2_ref, a2_ref)
        steps = [(w, j) for j in range(n + 1) for w in range(2)]
        a = logits(*steps[0])
        for i, (w, j) in enumerate(steps):
            a_next = logits(*steps[i + 1]) if i + 1 < len(steps) else None
            if j == n:
                a = jnp.where(_causal_diag(t), a, NEG_INF)
            _flash_step(a, v_ref[0, j * t:(j + 1) * t, :], *streams[w][2:])
            a = a_next
        lam = (jnp.exp(jnp.sum(lq1_ref[...] * lk1_ref[...], axis=1, keepdims=True))
               - jnp.exp(jnp.sum(lq2_ref[...] * lk2_ref[...], axis=1, keepdims=True)) + lam_init)
        o = a1_ref[...] / l1_ref[...] - lam * (a2_ref[...] / l2_ref[...])
        y = o * lax.rsqrt(jnp.mean(o * o, axis=-1, keepdims=True) + NORM_EPS)
        o_ref[0] = ((y * g_ref[...]) * (1.0 - lam_init)).astype(o_ref.dtype)

    for n in range(v_ref.shape[1] // t):
        pl.when(qi == n)(functools.partial(tile_body, n))


def diff_attention(u_d, u_p, lq1, lk1, lq2, lk2, g, lam_init, t):
    b, s, _ = u_d.shape
    cq1, cq2, ck1, ck2, cv = COL_D["dq1"], COL_D["dq2"], COL_D["dk1"], COL_D["dk2"], COL_P["dv"]
    vec64 = pl.BlockSpec((1, DIFF_DIM), lambda b_, h, i: (0, 0))
    stat = pltpu.VMEM((t, LANES), F32)
    return pl.pallas_call(
        functools.partial(_diff_kernel, t=t, lam_init=lam_init),
        grid=(b, N_HEADS, s // t),
        in_specs=[
            pl.BlockSpec((1, t, LANES), lambda b_, h, i: (b_, i, cq1 + h // 2)),
            pl.BlockSpec((1, t, LANES), lambda b_, h, i: (b_, i, cq2 + h // 2)),
            pl.BlockSpec((1, s, LANES), lambda b_, h, i: (b_, 0, ck1 + h // 2)),
            pl.BlockSpec((1, s, LANES), lambda b_, h, i: (b_, 0, ck2 + h // 2)),
            pl.BlockSpec((1, s, LANES), lambda b_, h, i: (b_, 0, cv + h)),
            vec64, vec64, vec64, vec64,
            pl.BlockSpec((1, HEAD_DIM), lambda b_, h, i: (0, 0)),
        ],
        out_specs=pl.BlockSpec((1, t, LANES), lambda b_, h, i: (b_, i, h)),
        out_shape=jax.ShapeDtypeStruct((b, s, BRANCH_WIDTH), BF16),
        scratch_shapes=[stat] * 6,
        compiler_params=_cparams(("parallel", "parallel", "arbitrary"), 40),
        name="diff_attention",
    )(u_d, u_d, u_d, u_d, u_p,
      lq1.reshape(1, DIFF_DIM), lk1.reshape(1, DIFF_DIM), lq2.reshape(1, DIFF_DIM), lk2.reshape(1, DIFF_DIM),
      g.reshape(1, HEAD_DIM))


def _compress_kernel(kt_ref, vt_ref, pek_ref, pev_ref, wk_ref, wv_ref, kc_ref, vc_ref,
                     xs_ref, xa_ref, xb_ref, *, n_blk):
    half = CMP_BLOCK // 2

    def compress(x, pe_ref, w_ref):
        xs_ref[...] = x.astype(F32)
        for r in range(half):
            piece = xs_ref[pl.ds(r, n_blk, stride=half), :]
            xa_ref[:, r * LANES:(r + 1) * LANES] = (piece + pe_ref[pl.ds(r, 1), :]).astype(BF16)
            xb_ref[:, r * LANES:(r + 1) * LANES] = (piece + pe_ref[pl.ds(half + r, 1), :]).astype(BF16)
        first = _dot(xa_ref[...], w_ref[0])
        second = _dot(xb_ref[...], w_ref[1])
        return first + pltpu.roll(second, n_blk - 1, axis=0)

    for g in range(N_KV_NSA):
        sl = slice(g * LANES, (g + 1) * LANES)
        kc_ref[0, g] = compress(kt_ref[0][:, sl], pek_ref, wk_ref).astype(BF16)
        vc_ref[0, g] = compress(vt_ref[0][:, sl], pev_ref, wv_ref).astype(BF16)


def nsa_compress(u_r, u_p, pe_k, pe_v, w_ck, w_cv):
    b, s, _ = u_r.shape
    n_blk = s // CMP_STRIDE
    half = CMP_BLOCK // 2
    wk = w_ck.reshape(2, half * HEAD_DIM, HEAD_DIM).astype(BF16)
    wv = w_cv.reshape(2, half * HEAD_DIM, HEAD_DIM).astype(BF16)
    pe = pl.BlockSpec((CMP_BLOCK, HEAD_DIM), lambda i: (0, 0))
    wspec = pl.BlockSpec((2, half * HEAD_DIM, HEAD_DIM), lambda i: (0, 0, 0))
    out = pl.BlockSpec((1, N_KV_NSA, n_blk, HEAD_DIM), lambda i: (i, 0, 0, 0))
    return pl.pallas_call(
        functools.partial(_compress_kernel, n_blk=n_blk),
        grid=(b,),
        in_specs=[pl.BlockSpec((1, s, 2 * LANES), lambda i: (i, 0, COL_R["nkc"] // 2)),
                  pl.BlockSpec((1, s, 2 * LANES), lambda i: (i, 0, COL_P["nvc"] // 2)),
                  pe, pe, wspec, wspec],
        out_specs=[out, out],
        out_shape=[jax.ShapeDtypeStruct((b, N_KV_NSA, n_blk, HEAD_DIM), BF16)] * 2,
        scratch_shapes=[pltpu.VMEM((s, LANES), F32), pltpu.VMEM((n_blk, half * LANES), BF16),
                        pltpu.VMEM((n_blk, half * LANES), BF16)],
        compiler_params=_cparams(("parallel",), 40),
        name="nsa_compress",
    )(u_r, u_p, pe_k, pe_v, wk, wv)


def _cmp_attn_kernel(q_ref, kc_ref, vc_ref, ov_ref, oc_ref, mt_ref, *, tq, n_blk, n_sel):
    qi = pl.program_id(1)
    q0 = qi * tq
    scale = HEAD_DIM ** -0.5
    hpg = N_HEADS // N_KV_NSA
    t_row = q0 + lax.broadcasted_iota(jnp.int32, (tq, n_blk), 0)
    n_col = lax.broadcasted_iota(jnp.int32, (tq, n_blk), 1)
    valid = n_col * CMP_STRIDE + (CMP_BLOCK - 1) <= t_row
    t_lane = q0 + lax.broadcasted_iota(jnp.int32, (n_blk, tq), 1)
    n_sub = lax.broadcasted_iota(jnp.int32, (n_blk, tq), 0)
    valid_t = n_sub * CMP_STRIDE + (CMP_BLOCK - 1) <= t_lane
    j_idx = lax.broadcasted_iota(jnp.int32, (n_sel, tq), 0)
    cur = (q0 + lax.broadcasted_iota(jnp.int32, (n_sel, tq), 1)) // SEL_BLOCK
    for g in range(N_KV_NSA):
        kc = kc_ref[0, g]
        vc = vc_ref[0, g]
        p_sum_t = jnp.zeros((n_blk, tq), F32)
        for hh in range(hpg):
            sl = slice((g * hpg + hh) * LANES, (g * hpg + hh + 1) * LANES)
            q = q_ref[0][:, sl]
            s = jnp.where(valid, _dot_nt(q, kc) * scale, NEG_INF)
            m = jnp.max(s, axis=1, keepdims=True)
            m = jnp.where(m == NEG_INF, 0.0, m)
            e = jnp.exp(s - m)
            p = e / jnp.maximum(jnp.sum(e, axis=1, keepdims=True), 1e-30)
            oc_ref[0, :, sl] = _dot(p.astype(BF16), vc)
            st = jnp.where(valid_t, _dot_nt(kc, q) * scale, NEG_INF)
            mt = jnp.max(st, axis=0, keepdims=True)
            mt = jnp.where(mt == NEG_INF, 0.0, mt)
            et = jnp.exp(st - mt)
            p_sum_t = p_sum_t + et / jnp.maximum(jnp.sum(et, axis=0, keepdims=True), 1e-30)
        hi = p_sum_t.astype(BF16)
        lo = (p_sum_t - hi.astype(F32)).astype(BF16)
        imp = _dot(ov_ref[...], hi) + _dot(ov_ref[...], lo)
        imp = jnp.where((j_idx == 0) | (j_idx == cur) | (j_idx == cur - 1), FORCE_SCORE, imp)
        imp = jnp.where(j_idx <= cur, imp, NEG_INF)
        rank = jnp.zeros((n_sel, tq), F32)
        for jp in range(n_sel):
            other = imp[jp:jp + 1, :]
            beats = (other > imp) | ((other == imp) & (j_idx > jp))
            rank = rank + beats.astype(F32)
        mt_ref[0, g] = (rank < float(min(SEL_TOP_N, n_sel))).astype(F32)


def nsa_cmp_attention(u_r, kc, vc, tq):
    b, s, _ = u_r.shape
    n_blk = s // CMP_STRIDE
    n_sel = s // SEL_BLOCK
    ci = jnp.arange(n_blk)[None, :] * CMP_STRIDE
    sj = jnp.arange(n_sel)[:, None] * SEL_BLOCK
    overlap_t = ((ci < sj + SEL_BLOCK) & (ci + CMP_BLOCK > sj)).astype(BF16)
    cblk = pl.BlockSpec((1, N_KV_NSA, n_blk, HEAD_DIM), lambda b_, i: (b_, 0, 0, 0))
    return pl.pallas_call(
        functools.partial(_cmp_attn_kernel, tq=tq, n_blk=n_blk, n_sel=n_sel),
        grid=(b, s // tq),
        in_specs=[pl.BlockSpec((1, tq, BRANCH_WIDTH), lambda b_, i: (b_, i, COL_R["nq"] // N_HEADS)), cblk, cblk,
                  pl.BlockSpec((n_sel, n_blk), lambda b_, i: (0, 0))],
        out_specs=[pl.BlockSpec((1, tq, BRANCH_WIDTH), lambda b_, i: (b_, i, 0)),
                   pl.BlockSpec((1, N_KV_NSA, n_sel, tq), lambda b_, i: (b_, 0, 0, i))],
        out_shape=[jax.ShapeDtypeStruct((b, s, BRANCH_WIDTH), F32),
                   jax.ShapeDtypeStruct((b, N_KV_NSA, n_sel, s), F32)],
        compiler_params=_cparams(("parallel", "parallel"), 40),
        name="nsa_cmp_attention",
    )(u_r, kc, vc, overlap_t)


def _sel_kernel(q_ref, k_ref, vt_ref, mt_ref, o_ref, m_ref, l_ref, acc_ref, *, t):
    qi = pl.program_id(2)
    c = HEAD_DIM ** -0.5 * LOG2E
    hpg = N_HEADS // N_KV_NSA
    per_tile = t // SEL_BLOCK
    causal_t = lax.broadcasted_iota(jnp.int32, (t, t), 0) <= lax.broadcasted_iota(jnp.int32, (t, t), 1)
    qs = [q_ref[0][:, hh * LANES:(hh + 1) * LANES] for hh in range(hpg)]

    def scores(hh, j):
        return _dot_nt(k_ref[0, j * t:(j + 1) * t, :], qs[hh]) * c

    def tile_body(n):
        _flash_init(m_ref, l_ref, acc_ref)
        steps = [(hh, j) for j in range(n + 1) for hh in range(hpg)]
        at = scores(*steps[0])
        for i, (hh, j) in enumerate(steps):
            at_next = scores(*steps[i + 1]) if i + 1 < len(steps) else None
            rows = [jnp.broadcast_to(mt_ref[0, 0, j * per_tile + r:j * per_tile + r + 1, :], (SEL_BLOCK, t))
                    for r in range(per_tile)]
            mask = jnp.concatenate(rows, axis=0) > 0.5
            if j == n:
                mask = mask & causal_t
            at = jnp.where(mask, at, NEG_INF)
            m_prev = m_ref[hh]
            m_new = jnp.maximum(m_prev, jnp.max(at, axis=0, keepdims=True))
            alpha = jnp.exp2(m_prev - m_new)
            p = jnp.exp2(at - m_new)
            l_ref[hh] = alpha * l_ref[hh] + jnp.sum(p, axis=0, keepdims=True)
            acc_ref[hh] = alpha * acc_ref[hh] + _dot(vt_ref[0, :, j * t:(j + 1) * t], p.astype(BF16))
            m_ref[hh] = m_new
            at = at_next
        for hh in range(hpg):
            o_ref[0, :, hh * LANES:(hh + 1) * LANES] = (acc_ref[hh] / l_ref[hh]).T

    for n in range(k_ref.shape[1] // t):
        pl.when(qi == n)(functools.partial(tile_body, n))


def nsa_sel_attention(u_r, vs_t, mt, t):
    b, s, _ = u_r.shape
    n_sel = s // SEL_BLOCK
    hpg = N_HEADS // N_KV_NSA
    cq, ck = COL_R["nq"] // hpg, COL_R["nks"]
    return pl.pallas_call(
        functools.partial(_sel_kernel, t=t),
        grid=(b, N_KV_NSA, s // t),
        in_specs=[pl.BlockSpec((1, t, hpg * LANES), lambda b_, g, i: (b_, i, cq + g)),
                  pl.BlockSpec((1, s, LANES), lambda b_, g, i: (b_, 0, ck + g)),
                  pl.BlockSpec((1, LANES, s), lambda b_, g, i: (b_, g, 0)),
                  pl.BlockSpec((1, 1, n_sel, t), lambda b_, g, i: (b_, g, 0, i))],
        out_specs=pl.BlockSpec((1, t, hpg * LANES), lambda b_, g, i: (b_, i, g)),
        out_shape=jax.ShapeDtypeStruct((b, s, BRANCH_WIDTH), F32),
        scratch_shapes=[pltpu.VMEM((hpg, 1, t), F32), pltpu.VMEM((hpg, 1, t), F32),
                        pltpu.VMEM((hpg, HEAD_DIM, t), F32)],
        compiler_params=_cparams(("parallel", "parallel", "arbitrary"), 40),
        name="nsa_sel_attention",
    )(u_r, u_r, vs_t, mt)


def _win_kernel(q_ref, k_ref, kp_ref, v_ref, vp_ref, oc_ref, os_ref, small_ref, o_ref, m_ref, l_ref, acc_ref, *, t):
    qi = pl.program_id(1)
    c = HEAD_DIM ** -0.5 * LOG2E
    hpg = N_HEADS // N_KV_NSA
    row = lax.broadcasted_iota(jnp.int32, (t, t), 0)
    col = lax.broadcasted_iota(jnp.int32, (t, t), 1)
    gates = jax.nn.sigmoid(small_ref[0])

    def tile_body(with_prev):
        heads = [(g, hh) for g in range(N_KV_NSA) for hh in range(hpg)]

        def scores(g, hh):
            gl = slice(g * LANES, (g + 1) * LANES)
            q = q_ref[0][:, (g * hpg + hh) * LANES:(g * hpg + hh + 1) * LANES]
            a = jnp.where(col <= row, _dot_nt(q, k_ref[0][:, gl]) * c, NEG_INF)
            ap = jnp.where(col > row, _dot_nt(q, kp_ref[0][:, gl]) * c, NEG_INF) if with_prev else None
            return a, ap

        nxt = scores(*heads[0])
        for i, (g, hh) in enumerate(heads):
            a, ap = nxt
            nxt = scores(*heads[i + 1]) if i + 1 < len(heads) else None
            gl = slice(g * LANES, (g + 1) * LANES)
            head = g * hpg + hh
            sl = slice(head * LANES, (head + 1) * LANES)
            _flash_init(m_ref, l_ref, acc_ref)
            _flash_step(a, v_ref[0][:, gl], m_ref, l_ref, acc_ref)
            if with_prev:
                _flash_step(ap, vp_ref[0][:, gl], m_ref, l_ref, acc_ref)
            o_w = acc_ref[...] / l_ref[...]
            cg = 4 + 3 * head
            o = (gates[:, cg:cg + 1] * oc_ref[0][:, sl] + gates[:, cg + 1:cg + 2] * os_ref[0][:, sl]
                 + gates[:, cg + 2:cg + 3] * o_w)
            o_ref[0, :, sl] = o.astype(o_ref.dtype)

    pl.when(qi == 0)(functools.partial(tile_body, False))
    pl.when(qi > 0)(functools.partial(tile_body, True))


def nsa_win_attention(u_r, u_p, o_c, o_s, small, t):
    b, s, _ = u_r.shape
    assert t == WINDOW
    wide = pl.BlockSpec((1, t, BRANCH_WIDTH), lambda b_, i: (b_, i, 0))
    ck, cv = COL_R["nkw"] // 2, COL_P["nvw"] // 2
    stat = pltpu.VMEM((t, LANES), F32)
    return pl.pallas_call(
        functools.partial(_win_kernel, t=t),
        grid=(b, s // t),
        in_specs=[pl.BlockSpec((1, t, BRANCH_WIDTH), lambda b_, i: (b_, i, COL_R["nq"] // N_HEADS)),
                  pl.BlockSpec((1, t, 2 * LANES), lambda b_, i: (b_, i, ck)),
                  pl.BlockSpec((1, t, 2 * LANES), lambda b_, i: (b_, jnp.maximum(i - 1, 0), ck)),
                  pl.BlockSpec((1, t, 2 * LANES), lambda b_, i: (b_, i, cv)),
                  pl.BlockSpec((1, t, 2 * LANES), lambda b_, i: (b_, jnp.maximum(i - 1, 0), cv)),
                  wide, wide,
                  pl.BlockSpec((1, t, LANES), lambda b_, i: (b_, i, 0))],
        out_specs=wide,
        out_shape=jax.ShapeDtypeStruct((b, s, BRANCH_WIDTH), BF16),
        scratch_shapes=[stat, stat, stat],
        compiler_params=_cparams(("parallel", "arbitrary"), 48),
        name="nsa_win_attention",
    )(u_r, u_r, u_r, u_p, u_p, o_c, o_s, small)


def _merge_kernel(h_ref, o0_ref, o1_ref, o2_ref, o3_ref, wg_ref, wb_ref, out_ref):
    h = h_ref[...]
    acc = None
    for n, o_ref in enumerate((o0_ref, o1_ref, o2_ref, o3_ref)):
        term = jax.nn.sigmoid(_dot(h, wg_ref[n])) * _dot(o_ref[...], wb_ref[n])
        acc = term if acc is None else acc + term
    out_ref[...] = acc.astype(out_ref.dtype)


def merge_branches(h, branches, wg, wb, tm, tn):
    n, d = h.shape
    bspec = pl.BlockSpec((tm, BRANCH_WIDTH), lambda j, i: (i, 0))
    return pl.pallas_call(
        _merge_kernel,
        grid=(d // tn, n // tm),
        in_specs=[pl.BlockSpec((tm, d), lambda j, i: (i, 0)), bspec, bspec, bspec, bspec,
                  pl.BlockSpec((4, d, tn), lambda j, i: (0, 0, j)),
                  pl.BlockSpec((4, BRANCH_WIDTH, tn), lambda j, i: (0, 0, j))],
        out_specs=pl.BlockSpec((tm, tn), lambda j, i: (i, j)),
        out_shape=jax.ShapeDtypeStruct((n, d), BF16),
        compiler_params=_cparams(("parallel", "arbitrary"), 56),
        name="merge_branches",
    )(h, *branches, wg, wb)


def _out_kernel(mg_ref, x_ref, w_ref, g_ref, wr_ref, br_ref, x1_ref, h2_ref, route_ref, *, tm):
    x1 = x_ref[...] + _dot(mg_ref[...], w_ref[...])
    x1_ref[...] = x1
    h2 = (x1 * lax.rsqrt(jnp.mean(x1 * x1, axis=-1, keepdims=True) + NORM_EPS)) * g_ref[...]
    for s in range(ROW_TILES):
        h2_ref[pl.ds(s, tm, stride=ROW_TILES), :] = h2[:, s * LANES:(s + 1) * LANES]
    logits = _dot(h2.astype(BF16), wr_ref[...]) + br_ref[...]
    lane = lax.broadcasted_iota(jnp.int32, logits.shape, 1)
    lane_f = lane.astype(F32)
    big = float(LANES)
    is_g = lane < N_GROUPS
    lg = jnp.where(is_g, logits, NEG_INF)
    mx = jnp.max(lg, axis=1, keepdims=True)
    gi = jnp.min(jnp.where(lg == mx, lane_f, big), axis=1, keepdims=True)
    pg = 1.0 / jnp.sum(jnp.where(is_g, jnp.exp(lg - mx), 0.0), axis=1, keepdims=True)
    e_idx = lane - N_GROUPS
    in_grp = (lane >= N_GROUPS) & (lane < N_GROUPS + N_EXPERTS) & ((e_idx // EXPERTS_PER_GROUP).astype(F32) == gi)
    le = jnp.where(in_grp, logits, NEG_INF)
    v1 = jnp.max(le, axis=1, keepdims=True)
    i1 = jnp.min(jnp.where(le == v1, lane_f, big), axis=1, keepdims=True)
    le2 = jnp.where(lane_f == i1, NEG_INF, le)
    v2 = jnp.max(le2, axis=1, keepdims=True)
    i2 = jnp.min(jnp.where(le2 == v2, lane_f, big), axis=1, keepdims=True)
    e2 = jnp.exp(v2 - v1)
    w1 = pg / (1.0 + e2)
    w2 = pg * e2 / (1.0 + e2)
    route = jnp.where(lane == 0, i1 - N_GROUPS, 0.0)
    route = jnp.where(lane == 1, i2 - N_GROUPS, route)
    route = jnp.where(lane == 2, w1, route)
    route = jnp.where(lane == 3, w2, route)
    route_ref[...] = route


def out_proj_router(merged, x, w_out, g2, w_router, b_router, tm):
    n, d = x.shape
    row = pl.BlockSpec((tm, d), lambda i: (i, 0))
    return pl.pallas_call(
        functools.partial(_out_kernel, tm=tm),
        grid=(n // tm,),
        in_specs=[row, row, pl.BlockSpec((d, d), lambda i: (0, 0)), pl.BlockSpec((1, d), lambda i: (0, 0)),
                  pl.BlockSpec((d, LANES), lambda i: (0, 0)), pl.BlockSpec((1, LANES), lambda i: (0, 0))],
        out_specs=[row, pl.BlockSpec((tm * ROW_TILES, LANES), lambda i: (i, 0)),
                   pl.BlockSpec((tm, LANES), lambda i: (i, 0))],
        out_shape=[jax.ShapeDtypeStruct((n, d), F32), jax.ShapeDtypeStruct((n * ROW_TILES, LANES), F32),
                   jax.ShapeDtypeStruct((n, LANES), F32)],
        compiler_params=_cparams(("parallel",), 56),
        name="out_proj_router",
    )(merged, x, w_out, g2.reshape(1, d), w_router, b_router)


def _row_copy(src_hbm, src_row, buf_ref, r, sem):
    src = pl.multiple_of(src_row * ROW_TILES, ROW_TILES)
    return pltpu.make_async_copy(src_hbm.at[pl.ds(src, ROW_TILES), :],
                                 buf_ref.at[pl.ds(r * ROW_TILES, ROW_TILES), :], sem)


def _start_rows(src_hbm, idx_ref, n_rows, buf_ref, sem):
    for r in range(n_rows):
        _row_copy(src_hbm, idx_ref[0, 0, r], buf_ref, r, sem).start(priority=r % 2)


def _wait_rows(src_hbm, buf_ref, sem):
    pltpu.make_async_copy(src_hbm.at[pl.ds(0, buf_ref.shape[0]), :], buf_ref, sem).wait()


def _ffn_kernel(blk_e_ref, n_used_ref, tok_ref, tok_next_ref, sw_ref, h2_hbm, wg32_ref, wu32_ref, wd32_ref, y_ref,
                xa_ref, xb_ref, wg_ref, wu_ref, wd_ref, sem, *, tm):
    blk = pl.program_id(0)
    n_used = n_used_ref[0]
    even = blk % 2 == 0

    @pl.when(blk == 0)
    def _():
        _start_rows(h2_hbm, tok_ref, tm, xa_ref, sem.at[0])

    @pl.when((blk == 0) | (blk_e_ref[blk] != blk_e_ref[jnp.maximum(blk - 1, 0)]))
    def _():
        rows = 256
        for src, dst in ((wg32_ref, wg_ref), (wu32_ref, wu_ref), (wd32_ref, wd_ref)):
            for r0 in range(0, dst.shape[1], rows):
                dst[0, r0:r0 + rows, :] = src[0, 0, r0:r0 + rows, :].astype(BF16)

    def work(cur_ref, cur_sem, nxt_ref, nxt_sem):
        _wait_rows(h2_hbm, cur_ref, cur_sem)
        _start_rows(h2_hbm, tok_next_ref, tm, nxt_ref, nxt_sem)
        x = jnp.concatenate([cur_ref[pl.ds(s, tm, stride=ROW_TILES), :] for s in range(ROW_TILES)],
                            axis=1).astype(BF16)
        gate = _dot(x, wg_ref[0])
        up = _dot(x, wu_ref[0])
        hid = (gate * jax.nn.sigmoid(gate) * up).astype(BF16)
        y = _dot(hid, wd_ref[0]) * sw_ref[...]
        for s in range(ROW_TILES):
            y_ref[pl.ds(s, tm, stride=ROW_TILES), :] = y[:, s * LANES:(s + 1) * LANES]

    @pl.when((blk < n_used) & even)
    def _():
        work(xa_ref, sem.at[0], xb_ref, sem.at[1])

    @pl.when((blk < n_used) & jnp.logical_not(even))
    def _():
        work(xb_ref, sem.at[1], xa_ref, sem.at[0])

    @pl.when(blk >= n_used)
    def _():
        y_ref[...] = jnp.zeros(y_ref.shape, F32)

    @pl.when((blk == n_used) & even)
    def _():
        _wait_rows(h2_hbm, xa_ref, sem.at[0])

    @pl.when((blk == n_used) & jnp.logical_not(even))
    def _():
        _wait_rows(h2_hbm, xb_ref, sem.at[1])


def moe_experts(blk_e, n_used, buf_tok, buf_w, h2_rows, w_g, w_u, w_d, layer, tm):
    n_blk = blk_e.shape[0]
    d, ff = w_g.shape[2], w_g.shape[3]
    tok = buf_tok.reshape(n_blk, 1, tm)
    once = pl.Buffered(1)
    grid_spec = pltpu.PrefetchScalarGridSpec(
        num_scalar_prefetch=2,
        grid=(n_blk,),
        in_specs=[
            pl.BlockSpec((1, 1, tm), lambda i, be, nu: (i, 0, 0), memory_space=pltpu.SMEM),
            pl.BlockSpec((1, 1, tm), lambda i, be, nu: (jnp.minimum(i + 1, n_blk - 1), 0, 0), memory_space=pltpu.SMEM),
            pl.BlockSpec((tm, 1), lambda i, be, nu: (i, 0)),
            pl.BlockSpec(memory_space=pl.ANY),
            pl.BlockSpec((1, 1, d, ff), lambda i, be, nu: (layer, be[i], 0, 0), pipeline_mode=once),
            pl.BlockSpec((1, 1, d, ff), lambda i, be, nu: (layer, be[i], 0, 0), pipeline_mode=once),
            pl.BlockSpec((1, 1, ff, d), lambda i, be, nu: (layer, be[i], 0, 0), pipeline_mode=once),
        ],
        out_specs=pl.BlockSpec((tm * ROW_TILES, LANES), lambda i, be, nu: (i, 0)),
        scratch_shapes=[pltpu.VMEM((tm * ROW_TILES, LANES), F32), pltpu.VMEM((tm * ROW_TILES, LANES), F32),
                        pltpu.VMEM((1, d, ff), BF16), pltpu.VMEM((1, d, ff), BF16), pltpu.VMEM((1, ff, d), BF16),
                        pltpu.SemaphoreType.DMA((2,))],
    )
    return pl.pallas_call(
        functools.partial(_ffn_kernel, tm=tm),
        grid_spec=grid_spec,
        out_shape=jax.ShapeDtypeStruct((n_blk * tm * ROW_TILES, LANES), F32),
        compiler_params=_cparams(("arbitrary",), 56),
        name="moe_experts",
    )(blk_e, n_used, tok, tok, buf_w.reshape(n_blk * tm, 1), h2_rows, w_g, w_u, w_d)


def _combine_kernel(pos_ref, pos_next_ref, x1_ref, yb_hbm, g_ref, x2_ref, hn_ref, ba_ref, bb_ref, sem, *, tm, n_tiles):
    i = pl.program_id(0)
    even = i % 2 == 0

    @pl.when(i == 0)
    def _():
        _start_rows(yb_hbm, pos_ref, 2 * tm, ba_ref, sem.at[0])

    def work(cur_ref, cur_sem, nxt_ref, nxt_sem):
        _wait_rows(yb_hbm, cur_ref, cur_sem)
        if nxt_ref is not None:
            _start_rows(yb_hbm, pos_next_ref, 2 * tm, nxt_ref, nxt_sem)
        pieces = []
        ssq = jnp.zeros((tm, 1), F32)
        for s in range(ROW_TILES):
            piece = (x1_ref[:, s * LANES:(s + 1) * LANES]
                     + cur_ref[pl.ds(s, tm, stride=2 * ROW_TILES), :]
                     + cur_ref[pl.ds(ROW_TILES + s, tm, stride=2 * ROW_TILES), :])
            x2_ref[:, s * LANES:(s + 1) * LANES] = piece
            ssq = ssq + jnp.sum(piece * piece, axis=1, keepdims=True)
            pieces.append(piece)
        inv = lax.rsqrt(ssq / D_MODEL + NORM_EPS)
        for s in range(ROW_TILES):
            sl = slice(s * LANES, (s + 1) * LANES)
            hn_ref[:, sl] = ((pieces[s] * inv) * g_ref[:, sl]).astype(hn_ref.dtype)

    @pl.when((i < n_tiles - 1) & even)
    def _():
        work(ba_ref, sem.at[0], bb_ref, sem.at[1])

    @pl.when((i < n_tiles - 1) & jnp.logical_not(even))
    def _():
        work(bb_ref, sem.at[1], ba_ref, sem.at[0])

    @pl.when(i == n_tiles - 1)
    def _():
        work(bb_ref, sem.at[1], None, None)


def moe_combine(pos, x1, yb_rows, g, hn_dtype, tm):
    n, d = x1.shape
    n_tiles = n // tm
    assert n_tiles % 2 == 0
    row = pl.BlockSpec((tm, d), lambda i: (i, 0))
    pos3 = pos.reshape(n_tiles, 1, 2 * tm)
    return pl.pallas_call(
        functools.partial(_combine_kernel, tm=tm, n_tiles=n_tiles),
        grid=(n_tiles,),
        in_specs=[pl.BlockSpec((1, 1, 2 * tm), lambda i: (i, 0, 0), memory_space=pltpu.SMEM),
                  pl.BlockSpec((1, 1, 2 * tm), lambda i: (jnp.minimum(i + 1, n_tiles - 1), 0, 0), memory_space=pltpu.SMEM),
                  row, pl.BlockSpec(memory_space=pl.ANY), pl.BlockSpec((1, d), lambda i: (0, 0))],
        out_specs=[row, row],
        out_shape=[jax.ShapeDtypeStruct((n, d), F32), jax.ShapeDtypeStruct((n, d), hn_dtype)],
        scratch_shapes=[pltpu.VMEM((2 * tm * ROW_TILES, LANES), F32), pltpu.VMEM((2 * tm * ROW_TILES, LANES), F32),
                        pltpu.SemaphoreType.DMA((2,))],
        compiler_params=_cparams(("arbitrary",), 48),
        name="moe_combine",
    )(pos3, pos3, x1, yb_rows, g.reshape(1, d))


def _dispatch_plan(route, tm):
    n = route.shape[0]
    m = 2 * n
    slot_e = route[:, 0:2].astype(jnp.int32).reshape(m)
    slot_w = route[:, 2:4].reshape(m)
    iota = jnp.arange(m, dtype=jnp.int32)
    se, order, sw = lax.sort((slot_e, iota, slot_w), num_keys=1)
    e_ids = jnp.arange(N_EXPERTS, dtype=jnp.int32)
    counts = jnp.sum((slot_e[None, :] == e_ids[:, None]).astype(jnp.int32), axis=1)
    start = jnp.cumsum(counts) - counts
    padded = (counts + tm - 1) // tm * tm
    pend = jnp.cumsum(padded)
    pstart = pend - padded
    delta = pstart - start
    dest = iota + jnp.sum(jnp.where(se[:, None] == e_ids[None, :], delta[None, :], 0), axis=1)
    _, pos = lax.sort((order, dest), num_keys=1)
    n_blk = (m + N_EXPERTS * tm) // tm
    blk_row0 = jnp.arange(n_blk, dtype=jnp.int32) * tm
    blk_e = jnp.minimum(jnp.sum((pend[None, :] <= blk_row0[:, None]).astype(jnp.int32), axis=1), N_EXPERTS - 1)
    onehot = blk_e[:, None] == e_ids[None, :]
    pick = lambda tab: jnp.sum(jnp.where(onehot, tab[None, :], 0), axis=1)
    local0 = blk_row0 - pick(pstart)
    n_valid = jnp.clip(pick(counts) - local0, 0, tm)
    within = jnp.arange(tm, dtype=jnp.int32)[None, :]
    src = jnp.clip(pick(start)[:, None] + local0[:, None] + within, 0, m - 1)
    valid = within < n_valid[:, None]
    buf_tok = jnp.where(valid, jnp.take(order, src) // 2, 0)
    buf_w = jnp.where(valid, jnp.take(sw, src), 0.0)
    n_used = (pend[-1] // tm).astype(jnp.int32).reshape(1)
    return blk_e.astype(jnp.int32), n_used, buf_tok, buf_w, pos


def _rope_tables(s):
    pos = jnp.arange(s, dtype=F32)[:, None]
    inv128 = jnp.exp(-math.log(ROPE_THETA) * jnp.arange(0, HEAD_DIM, 2, dtype=F32) / HEAD_DIM)
    a = pos * inv128[None, :]
    c128 = jnp.concatenate([jnp.cos(a), jnp.cos(a)], axis=1)
    s128 = jnp.concatenate([-jnp.sin(a), jnp.sin(a)], axis=1)
    inv64 = jnp.exp(-math.log(ROPE_THETA) * jnp.arange(0, DIFF_DIM, 2, dtype=F32) / DIFF_DIM)
    a = pos * inv64[None, :]
    co, si, z = jnp.cos(a), jnp.sin(a), jnp.zeros_like(a)
    c64 = jnp.concatenate([co, co, co, co], axis=1)
    sa64 = jnp.concatenate([-si, z, -si, z], axis=1)
    sb64 = jnp.concatenate([z, si, z, si], axis=1)
    return (c128, s128), (c64, sa64, sb64)


def _orig_offsets():
    offs, o = {}, 0
    for name, w in zip(_ORIG_NAMES, _ORIG_WIDTHS):
        offs[name] = (o, w)
        o += w
    return offs


def _w_in_prep_kernel(w_ref, p_ref, r_ref, d_ref, s_ref):
    offs = _orig_offsets()
    for pieces, dst_ref in ((_PLAIN, p_ref), (_ROPE128, r_ref), (_ROPE64, d_ref)):
        dst = 0
        for name, width in pieces:
            src = offs[name][0]
            dst_ref[:, dst:dst + width] = w_ref[0, :, src:src + width].astype(BF16)
            dst += width
    (f0, fw), (g0, gw) = offs["ff"], offs["ngt"]
    s_ref[...] = jnp.zeros(s_ref.shape, BF16)
    s_ref[:, 0:fw] = w_ref[0, :, f0:f0 + fw].astype(BF16)
    s_ref[:, fw:fw + gw] = w_ref[0, :, g0:g0 + gw].astype(BF16)


def split_w_in(w_in, layer, tr=256):
    _, k, c = w_in.shape
    row = lambda w: pl.BlockSpec((tr, w), lambda i: (i, 0))
    shape = lambda w: jax.ShapeDtypeStruct((k, w), BF16)
    return pl.pallas_call(
        _w_in_prep_kernel,
        grid=(k // tr,),
        in_specs=[pl.BlockSpec((1, tr, c), lambda i: (layer, i, 0))],
        out_specs=[row(C_PLAIN), row(C_ROPE128), row(C_ROPE64), row(LANES)],
        out_shape=[shape(C_PLAIN), shape(C_ROPE128), shape(C_ROPE64), shape(LANES)],
        compiler_params=_cparams(("parallel",), 48),
        name="w_in_prep",
    )(w_in)


def _tiles(s, n):
    return dict(t_attn=min(512, s), tq_nsa=min(256, s), tk_sb=min(256, s), tm_norm=min(512, n), tm_proj=min(512, n),
                tm_out=min(512, n), tm_comb=min(256, n), tm_moe=256)


def kernel(x, norm1_g, w_in, fox_bf, nsa_pe_k, nsa_pe_v, nsa_w_ck, nsa_w_cv, diff_lq1, diff_lk1, diff_lq2, diff_lk2, diff_norm_g, w_branch, w_mgate, w_out, norm2_g, w_rg, b_rg, w_re, b_re, w_eg, w_eu, w_ed, final_g):
    b, s, d = x.shape
    n = b * s
    depth = w_in.shape[0]
    t = _tiles(s, n)
    rope128_tabs, rope64_tabs = _rope_tables(s)
    xf = x.reshape(n, d)
    h = rms_norm_rows(xf, norm1_g[0], BF16, t["tm_norm"])
    out = None
    for l in range(depth):
        lam_init = 0.8 - 0.6 * math.exp(-0.3 * l)
        w_p, w_r, w_d, w_s = split_w_in(w_in, l)
        u_p = in_proj_plain(h, w_p, t["tm_proj"], C_PLAIN // 2).reshape(b, s, C_PLAIN)
        u_r, u_d, small = in_proj_rope(h, w_r, w_d, w_s, rope128_tabs, rope64_tabs, s, t["tm_proj"])
        u_r, u_d, small = u_r.reshape(b, s, C_ROPE128), u_d.reshape(b, s, C_ROPE64), small.reshape(b, s, LANES)
        cum = forget_cumsum(small[:, :, 0:N_HEADS].transpose(0, 2, 1), fox_bf[l])
        o_fox = fox_attention(u_p, cum, t["t_attn"])
        o_sb = sb_attention(u_p, t["t_attn"], t["tk_sb"])
        o_diff = diff_attention(u_d, u_p, diff_lq1[l], diff_lk1[l], diff_lq2[l], diff_lk2[l], diff_norm_g[l],
                                lam_init, t["t_attn"])
        kc, vc = nsa_compress(u_r, u_p, nsa_pe_k[l], nsa_pe_v[l], nsa_w_ck[l], nsa_w_cv[l])
        o_c, mt = nsa_cmp_attention(u_r, kc, vc, t["tq_nsa"])
        nvs0 = COL_P["nvs"] * LANES
        vs_t = u_p[:, :, nvs0:nvs0 + 2 * LANES].transpose(0, 2, 1)
        o_s = nsa_sel_attention(u_r, vs_t, mt, t["t_attn"])
        o_nsa = nsa_win_attention(u_r, u_p, o_c, o_s, small, t["t_attn"])
        branches = [o.reshape(n, BRANCH_WIDTH) for o in (o_fox, o_nsa, o_sb, o_diff)]
        merged = merge_branches(h, branches, w_mgate[l].astype(BF16), w_branch[l].astype(BF16), t["tm_proj"], 512)
        w_router = jnp.concatenate([w_rg[l], w_re[l], jnp.zeros((d, LANES - N_GROUPS - N_EXPERTS), F32)], axis=1).astype(BF16)
        b_router = jnp.concatenate([b_rg[l], b_re[l], jnp.zeros((LANES - N_GROUPS - N_EXPERTS,), F32)]).reshape(1, LANES)
        x1, h2_rows, route = out_proj_router(merged, xf, w_out[l].astype(BF16), norm2_g[l], w_router, b_router, t["tm_out"])
        blk_e, n_used, buf_tok, buf_w, pos = _dispatch_plan(route, t["tm_moe"])
        yb_rows = moe_experts(blk_e, n_used, buf_tok, buf_w, h2_rows, w_eg, w_eu, w_ed, l, t["tm_moe"])
        last = l == depth - 1
        g_next = final_g if last else norm1_g[l + 1]
        xf, hn = moe_combine(pos, x1, yb_rows, g_next, F32 if last else BF16, t["tm_comb"])
        h = hn
        out = hn
    return out.reshape(b, s, d)
```

```python
import functools
import math

import jax
import jax.numpy as jnp
from jax import lax
from jax.experimental import pallas as pl
from jax.experimental.pallas import tpu as pltpu

F32 = jnp.float32
BF16 = jnp.bfloat16
NEG_INF = float("-inf")
LOG2E = 1.4426950408889634

D_MODEL = 2048
HEAD_DIM = 128
DIFF_DIM = 64
N_HEADS = 4
N_KV_NSA = 2
BRANCH_WIDTH = 512
ROPE_THETA = 10000.0
NORM_EPS = 1e-6
CMP_BLOCK = 32
CMP_STRIDE = 16
SEL_BLOCK = 64
SEL_TOP_N = 16
WINDOW = 512
FORCE_SCORE = 1e6
N_GROUPS = 4
EXPERTS_PER_GROUP = 8
N_EXPERTS = 32
EXPERT_FF = 1024

LANES = 128
ROW_TILES = D_MODEL // LANES
MIB = 1024 * 1024

_PLAIN = (("fq", 512), ("fk", 512), ("fv", 512), ("nvc", 256), ("nvs", 256), ("nvw", 256),
          ("sq", 512), ("sk", 512), ("sv", 512), ("dv", 512))
_ROPE128 = (("nq", 512), ("nkc", 256), ("nks", 256), ("nkw", 256))
_ROPE64 = (("dq1", 256), ("dq2", 256), ("dk1", 256), ("dk2", 256))


def _layout(pieces):
    col, off = {}, 0
    for name, w in pieces:
        col[name] = off // LANES
        off += w
    return col, off


COL_P, C_PLAIN = _layout(_PLAIN)
COL_R, C_ROPE128 = _layout(_ROPE128)
COL_D, C_ROPE64 = _layout(_ROPE64)

_ORIG_WIDTHS = (512, 512, 512, 4, 512, 256, 256, 256, 256, 256, 256, 12, 512, 512, 512, 256, 256, 256, 256, 512)
_ORIG_NAMES = ("fq", "fk", "fv", "ff", "nq", "nkc", "nvc", "nks", "nvs", "nkw", "nvw", "ngt",
               "sq", "sk", "sv", "dq1", "dq2", "dk1", "dk2", "dv")


def _cparams(sem, vmem_mib):
    return pltpu.CompilerParams(dimension_semantics=sem, vmem_limit_bytes=vmem_mib * MIB)


def _log_sigmoid(z):
    return jnp.minimum(z, 0.0) - jnp.log1p(jnp.exp(-jnp.abs(z)))


def _dot_nt(a, b):
    return lax.dot_general(a, b, (((1,), (1,)), ((), ())), preferred_element_type=F32)


def _dot(a, b):
    return jnp.dot(a, b, preferred_element_type=F32)


def _lanes(x, width):
    return x if width == LANES else jnp.concatenate([x] * (width // LANES), axis=1)


def _norm_kernel(x_ref, g_ref, o_ref):
    x = x_ref[...]
    y = x * lax.rsqrt(jnp.mean(x * x, axis=-1, keepdims=True) + NORM_EPS)
    o_ref[...] = (y * g_ref[...]).astype(o_ref.dtype)


def rms_norm_rows(x, g, out_dtype, tm):
    n, d = x.shape
    return pl.pallas_call(
        _norm_kernel,
        grid=(n // tm,),
        in_specs=[pl.BlockSpec((tm, d), lambda i: (i, 0)), pl.BlockSpec((1, d), lambda i: (0, 0))],
        out_specs=pl.BlockSpec((tm, d), lambda i: (i, 0)),
        out_shape=jax.ShapeDtypeStruct((n, d), out_dtype),
        compiler_params=_cparams(("parallel",), 40),
        name="rms_norm",
    )(x, g.reshape(1, d))


def _mm_kernel(a_ref, w_ref, o_ref):
    o_ref[...] = _dot(a_ref[...], w_ref[...]).astype(o_ref.dtype)


def in_proj_plain(h, w, tm, tn):
    m, k = h.shape
    c = w.shape[1]
    return pl.pallas_call(
        _mm_kernel,
        grid=(c // tn, m // tm),
        in_specs=[pl.BlockSpec((tm, k), lambda j, i: (i, 0)), pl.BlockSpec((k, tn), lambda j, i: (0, j))],
        out_specs=pl.BlockSpec((tm, tn), lambda j, i: (i, j)),
        out_shape=jax.ShapeDtypeStruct((m, c), BF16),
        compiler_params=_cparams(("parallel", "arbitrary"), 48),
        name="in_proj_plain",
    )(h, w)


def _proj_rope_kernel(h_ref, wr_ref, wd_ref, ws_ref, c128_ref, s128_ref, c64_ref, sa64_ref, sb64_ref,
                      r_ref, d_ref, small_ref):
    h = h_ref[...]
    acc = _dot(h, wr_ref[...])
    c, s = c128_ref[...], s128_ref[...]
    for blk in range(C_ROPE128 // LANES):
        x = acc[:, blk * LANES:(blk + 1) * LANES]
        r_ref[:, blk * LANES:(blk + 1) * LANES] = (x * c + pltpu.roll(x, 64, axis=1) * s).astype(BF16)
    acc = _dot(h, wd_ref[...])
    c, sa, sb = c64_ref[...], sa64_ref[...], sb64_ref[...]
    for blk in range(C_ROPE64 // LANES):
        x = acc[:, blk * LANES:(blk + 1) * LANES]
        d_ref[:, blk * LANES:(blk + 1) * LANES] = (
            x * c + pltpu.roll(x, 96, axis=1) * sa + pltpu.roll(x, 32, axis=1) * sb).astype(BF16)
    small_ref[...] = _dot(h, ws_ref[...])


def in_proj_rope(h, w_r, w_d, w_s, rope128_tabs, rope64_tabs, s, tm):
    m, k = h.shape
    per_seq = s // tm
    row = lambda w: pl.BlockSpec((tm, w), lambda i: (i, 0))
    full = lambda w: pl.BlockSpec((k, w), lambda i: (0, 0))
    tab = pl.BlockSpec((tm, LANES), lambda i: (i % per_seq, 0))
    return pl.pallas_call(
        _proj_rope_kernel,
        grid=(m // tm,),
        in_specs=[row(k), full(C_ROPE128), full(C_ROPE64), full(LANES), tab, tab, tab, tab, tab],
        out_specs=[row(C_ROPE128), row(C_ROPE64), row(LANES)],
        out_shape=[jax.ShapeDtypeStruct((m, C_ROPE128), BF16), jax.ShapeDtypeStruct((m, C_ROPE64), BF16),
                   jax.ShapeDtypeStruct((m, LANES), F32)],
        compiler_params=_cparams(("parallel",), 48),
        name="in_proj_rope",
    )(h, w_r, w_d, w_s, *rope128_tabs, *rope64_tabs)


def _cum_kernel(f_ref, b_ref, o_ref):
    z = f_ref[0] + b_ref[...]
    ls = _log_sigmoid(z)
    s = ls.shape[1]
    lane = lax.broadcasted_iota(jnp.int32, ls.shape, 1)
    sh = 1
    while sh < s:
        ls = ls + jnp.where(lane >= sh, pltpu.roll(ls, sh, axis=1), 0.0)
        sh *= 2
    o_ref[0] = ls * LOG2E


def forget_cumsum(f_t, bias):
    b, h, s = f_t.shape
    return pl.pallas_call(
        _cum_kernel,
        grid=(b,),
        in_specs=[pl.BlockSpec((1, h, s), lambda i: (i, 0, 0)), pl.BlockSpec((h, 1), lambda i: (0, 0))],
        out_specs=pl.BlockSpec((1, h, s), lambda i: (i, 0, 0)),
        out_shape=jax.ShapeDtypeStruct((b, h, s), F32),
        name="fox_cumsum",
    )(f_t, bias.reshape(h, 1))


def _flash_init(m_ref, l_ref, acc_ref):
    m_ref[...] = jnp.full(m_ref.shape, NEG_INF, F32)
    l_ref[...] = jnp.zeros(l_ref.shape, F32)
    acc_ref[...] = jnp.zeros(acc_ref.shape, F32)


def _flash_step(a, v, m_ref, l_ref, acc_ref):
    m_prev = m_ref[...]
    m_new = jnp.maximum(m_prev, jnp.max(a, axis=1, keepdims=True))
    alpha = jnp.exp2(m_prev - m_new)
    p = jnp.exp2(a - _lanes(m_new, a.shape[1]))
    l_ref[...] = alpha * l_ref[...] + jnp.sum(p, axis=1, keepdims=True)
    acc_ref[...] = alpha * acc_ref[...] + _dot(p.astype(BF16), v)
    m_ref[...] = m_new


def _causal_diag(t):
    return lax.broadcasted_iota(jnp.int32, (t, t), 1) <= lax.broadcasted_iota(jnp.int32, (t, t), 0)


def _fox_kernel(q_ref, k_ref, v_ref, cum_ref, o_ref, m_ref, l_ref, acc_ref, *, t):
    h = pl.program_id(1)
    qi = pl.program_id(2)
    c = HEAD_DIM ** -0.5 * LOG2E
    q = q_ref[0]

    def logits(j):
        return _dot_nt(q, k_ref[0, j * t:(j + 1) * t, :]) * c - cum_ref[0, pl.ds(h, 1), j * t:(j + 1) * t]

    def tile_body(n):
        _flash_init(m_ref, l_ref, acc_ref)
        a = logits(0)
        for j in range(n + 1):
            a_next = logits(j + 1) if j < n else None
            if j == n:
                a = jnp.where(_causal_diag(t), a, NEG_INF)
            _flash_step(a, v_ref[0, j * t:(j + 1) * t, :], m_ref, l_ref, acc_ref)
            a = a_next
        o_ref[0] = (acc_ref[...] / l_ref[...]).astype(o_ref.dtype)

    for n in range(k_ref.shape[1] // t):
        pl.when(qi == n)(functools.partial(tile_body, n))


def fox_attention(u, cum, t):
    b, s, _ = u.shape
    cq, ck, cv = COL_P["fq"], COL_P["fk"], COL_P["fv"]
    stat = pltpu.VMEM((t, LANES), F32)
    return pl.pallas_call(
        functools.partial(_fox_kernel, t=t),
        grid=(b, N_HEADS, s // t),
        in_specs=[
            pl.BlockSpec((1, t, LANES), lambda b_, h, i: (b_, i, cq + h)),
            pl.BlockSpec((1, s, LANES), lambda b_, h, i: (b_, 0, ck + h)),
            pl.BlockSpec((1, s, LANES), lambda b_, h, i: (b_, 0, cv + h)),
            pl.BlockSpec((1, N_HEADS, s), lambda b_, h, i: (b_, 0, 0)),
        ],
        out_specs=pl.BlockSpec((1, t, LANES), lambda b_, h, i: (b_, i, h)),
        out_shape=jax.ShapeDtypeStruct((b, s, BRANCH_WIDTH), BF16),
        scratch_shapes=[stat, stat, stat],
        compiler_params=_cparams(("parallel", "parallel", "arbitrary"), 40),
        name="fox_attention",
    )(u, u, u, cum)


def _sb_kernel(q_ref, k_ref, v_ref, o_ref, carry_ref, acc_ref, *, tq, tk):
    qi = pl.program_id(2)
    c = HEAD_DIM ** -0.5 * LOG2E
    q = q_ref[0]
    upper = (lax.broadcasted_iota(jnp.int32, (tk, tk), 0) > lax.broadcasted_iota(jnp.int32, (tk, tk), 1)).astype(BF16)
    per_tile = tq // tk

    def logits(c0):
        return _dot_nt(q, k_ref[0, c0:c0 + tk, :]) * c

    def chunk(z2, c0, mask):
        ls = jnp.minimum(z2, 0.0) - jnp.log2(1.0 + jnp.exp2(-jnp.abs(z2)))
        l_neg = ls - z2
        if mask is not None:
            l_neg = jnp.where(mask, l_neg, 0.0)
        hi = l_neg.astype(BF16)
        lo = (l_neg - hi.astype(F32)).astype(BF16)
        after = _lanes(carry_ref[...], tk) + _dot(hi, upper) + _dot(lo, upper)
        a = jnp.exp2(ls + after)
        if mask is not None:
            a = jnp.where(mask, a, 0.0)
        acc_ref[...] += _dot(a.astype(BF16), v_ref[0, c0:c0 + tk, :])
        carry_ref[...] += jnp.sum(l_neg, axis=1, keepdims=True)

    def tile_body(n):
        carry_ref[...] = jnp.zeros(carry_ref.shape, F32)
        acc_ref[...] = jnp.zeros(acc_ref.shape, F32)
        row = lax.broadcasted_iota(jnp.int32, (tq, tk), 0)
        col = lax.broadcasted_iota(jnp.int32, (tq, tk), 1)
        steps = [(n * tq + d * tk, d * tk + col < row) for d in reversed(range(per_tile))]
        steps += [(j * tk, None) for j in reversed(range(n * per_tile))]
        z2 = logits(steps[0][0])
        for i, (c0, mask) in enumerate(steps):
            z2_next = logits(steps[i + 1][0]) if i + 1 < len(steps) else None
            chunk(z2, c0, mask)
            z2 = z2_next
        o_ref[0] = acc_ref[...].astype(o_ref.dtype)

    for n in range(k_ref.shape[1] // tq):
        pl.when(qi == n)(functools.partial(tile_body, n))


def sb_attention(u, tq, tk):
    b, s, _ = u.shape
    cq, ck, cv = COL_P["sq"], COL_P["sk"], COL_P["sv"]
    return pl.pallas_call(
        functools.partial(_sb_kernel, tq=tq, tk=tk),
        grid=(b, N_HEADS, s // tq),
        in_specs=[
            pl.BlockSpec((1, tq, LANES), lambda b_, h, i: (b_, i, cq + h)),
            pl.BlockSpec((1, s, LANES), lambda b_, h, i: (b_, 0, ck + h)),
            pl.BlockSpec((1, s, LANES), lambda b_, h, i: (b_, 0, cv + h)),
        ],
        out_specs=pl.BlockSpec((1, tq, LANES), lambda b_, h, i: (b_, i, h)),
        out_shape=jax.ShapeDtypeStruct((b, s, BRANCH_WIDTH), BF16),
        scratch_shapes=[pltpu.VMEM((tq, LANES), F32), pltpu.VMEM((tq, LANES), F32)],
        compiler_params=_cparams(("parallel", "parallel", "arbitrary"), 40),
        name="sb_attention",
    )(u, u, u)


def _diff_kernel(q1_ref, q2_ref, k1_ref, k2_ref, v_ref, lq1_ref, lk1_ref, lq2_ref, lk2_ref, g_ref, o_ref,
                 m1_ref, l1_ref, a1_ref, m2_ref, l2_ref, a2_ref, *, t, lam_init):
    h = pl.program_id(1)
    qi = pl.program_id(2)
    c = DIFF_DIM ** -0.5 * LOG2E
    mine = (lax.broadcasted_iota(jnp.int32, (t, LANES), 1) // DIFF_DIM) == (h % 2)
    q1 = jnp.where(mine, q1_ref[0], 0.0).astype(BF16)
    q2 = jnp.where(mine, q2_ref[0], 0.0).astype(BF16)
    streams = ((q1, k1_ref, m1_ref, l1_ref, a1_ref), (q2, k2_ref, m2_ref, l2_ref, a2_ref))

    def logits(w, j):
        return _dot_nt(streams[w][0], streams[w][1][0, j * t:(j + 1) * t, :]) * c

    def tile_body(n):
        _flash_init(m1_ref, l1_ref, a1_ref)
        _flash_init(m2_ref, l2_ref, a2_ref)
        steps = [(w, j) for j in range(n + 1) for w in range(2)]
        a = logits(*steps[0])
        for i, (w, j) in enumerate(steps):
            a_next = logits(*steps[i + 1]) if i + 1 < len(steps) else None
            if j == n:
                a = jnp.where(_causal_diag(t), a, NEG_INF)
            _flash_step(a, v_ref[0, j * t:(j + 1) * t, :], *streams[w][2:])
            a = a_next
        lam = (jnp.exp(jnp.sum(lq1_ref[...] * lk1_ref[...], axis=1, keepdims=True))
               - jnp.exp(jnp.sum(lq2_ref[...] * lk2_ref[...], axis=1, keepdims=True)) + lam_init)
        o = a1_ref[...] / l1_ref[...] - lam * (a2_ref[...] / l2_ref[...])
        y = o * lax.rsqrt(jnp.mean(o * o, axis=-1, keepdims=True) + NORM_EPS)
        o_ref[0] = ((y * g_ref[...]) * (1.0 - lam_init)).astype(o_ref.dtype)

    for n in range(v_ref.shape[1] // t):
        pl.when(qi == n)(functools.partial(tile_body, n))


def diff_attention(u_d, u_p, lq1, lk1, lq2, lk2, g, lam_init, t):
    b, s, _ = u_d.shape
    cq1, cq2, ck1, ck2, cv = COL_D["dq1"], COL_D["dq2"], COL_D["dk1"], COL_D["dk2"], COL_P["dv"]
    vec64 = pl.BlockSpec((1, DIFF_DIM), lambda b_, h, i: (0, 0))
    stat = pltpu.VMEM((t, LANES), F32)
    return pl.pallas_call(
        functools.partial(_diff_kernel, t=t, lam_init=lam_init),
        grid=(b, N_HEADS, s // t),
        in_specs=[
            pl.BlockSpec((1, t, LANES), lambda b_, h, i: (b_, i, cq1 + h // 2)),
            pl.BlockSpec((1, t, LANES), lambda b_, h, i: (b_, i, cq2 + h // 2)),
            pl.BlockSpec((1, s, LANES), lambda b_, h, i: (b_, 0, ck1 + h // 2)),
            pl.BlockSpec((1, s, LANES), lambda b_, h, i: (b_, 0, ck2 + h // 2)),
            pl.BlockSpec((1, s, LANES), lambda b_, h, i: (b_, 0, cv + h)),
            vec64, vec64, vec64, vec64,
            pl.BlockSpec((1, HEAD_DIM), lambda b_, h, i: (0, 0)),
        ],
        out_specs=pl.BlockSpec((1, t, LANES), lambda b_, h, i: (b_, i, h)),
        out_shape=jax.ShapeDtypeStruct((b, s, BRANCH_WIDTH), BF16),
        scratch_shapes=[stat] * 6,
        compiler_params=_cparams(("parallel", "parallel", "arbitrary"), 40),
        name="diff_attention",
    )(u_d, u_d, u_d, u_d, u_p,
      lq1.reshape(1, DIFF_DIM), lk1.reshape(1, DIFF_DIM), lq2.reshape(1, DIFF_DIM), lk2.reshape(1, DIFF_DIM),
      g.reshape(1, HEAD_DIM))


def _compress_kernel(kt_ref, vt_ref, pek_ref, pev_ref, wk_ref, wv_ref, kc_ref, vc_ref,
                     xs_ref, xa_ref, xb_ref, *, n_blk):
    half = CMP_BLOCK // 2

    def compress(x, pe_ref, w_ref):
        xs_ref[...] = x.astype(F32)
        for r in range(half):
            piece = xs_ref[pl.ds(r, n_blk, stride=half), :]
            xa_ref[:, r * LANES:(r + 1) * LANES] = (piece + pe_ref[pl.ds(r, 1), :]).astype(BF16)
            xb_ref[:, r * LANES:(r + 1) * LANES] = (piece + pe_ref[pl.ds(half + r, 1), :]).astype(BF16)
        first = _dot(xa_ref[...], w_ref[0])
        second = _dot(xb_ref[...], w_ref[1])
        return first + pltpu.roll(second, n_blk - 1, axis=0)

    for g in range(N_KV_NSA):
        sl = slice(g * LANES, (g + 1) * LANES)
        kc_ref[0, g] = compress(kt_ref[0][:, sl], pek_ref, wk_ref).astype(BF16)
        vc_ref[0, g] = compress(vt_ref[0][:, sl], pev_ref, wv_ref).astype(BF16)


def nsa_compress(u_r, u_p, pe_k, pe_v, w_ck, w_cv):
    b, s, _ = u_r.shape
    n_blk = s // CMP_STRIDE
    half = CMP_BLOCK // 2
    wk = w_ck.reshape(2, half * HEAD_DIM, HEAD_DIM).astype(BF16)
    wv = w_cv.reshape(2, half * HEAD_DIM, HEAD_DIM).astype(BF16)
    pe = pl.BlockSpec((CMP_BLOCK, HEAD_DIM), lambda i: (0, 0))
    wspec = pl.BlockSpec((2, half * HEAD_DIM, HEAD_DIM), lambda i: (0, 0, 0))
    out = pl.BlockSpec((1, N_KV_NSA, n_blk, HEAD_DIM), lambda i: (i, 0, 0, 0))
    return pl.pallas_call(
        functools.partial(_compress_kernel, n_blk=n_blk),
        grid=(b,),
        in_specs=[pl.BlockSpec((1, s, 2 * LANES), lambda i: (i, 0, COL_R["nkc"] // 2)),
                  pl.BlockSpec((1, s, 2 * LANES), lambda i: (i, 0, COL_P["nvc"] // 2)),
                  pe, pe, wspec, wspec],
        out_specs=[out, out],
        out_shape=[jax.ShapeDtypeStruct((b, N_KV_NSA, n_blk, HEAD_DIM), BF16)] * 2,
        scratch_shapes=[pltpu.VMEM((s, LANES), F32), pltpu.VMEM((n_blk, half * LANES), BF16),
                        pltpu.VMEM((n_blk, half * LANES), BF16)],
        compiler_params=_cparams(("parallel",), 40),
        name="nsa_compress",
    )(u_r, u_p, pe_k, pe_v, wk, wv)


def _cmp_attn_kernel(q_ref, kc_ref, vc_ref, ov_ref, oc_ref, mt_ref, *, tq, n_blk, n_sel):
    qi = pl.program_id(1)
    q0 = qi * tq
    scale = HEAD_DIM ** -0.5
    hpg = N_HEADS // N_KV_NSA
    t_row = q0 + lax.broadcasted_iota(jnp.int32, (tq, n_blk), 0)
    n_col = lax.broadcasted_iota(jnp.int32, (tq, n_blk), 1)
    valid = n_col * CMP_STRIDE + (CMP_BLOCK - 1) <= t_row
    t_lane = q0 + lax.broadcasted_iota(jnp.int32, (n_blk, tq), 1)
    n_sub = lax.broadcasted_iota(jnp.int32, (n_blk, tq), 0)
    valid_t = n_sub * CMP_STRIDE + (CMP_BLOCK - 1) <= t_lane
    j_idx = lax.broadcasted_iota(jnp.int32, (n_sel, tq), 0)
    cur = (q0 + lax.broadcasted_iota(jnp.int32, (n_sel, tq), 1)) // SEL_BLOCK
    for g in range(N_KV_NSA):
        kc = kc_ref[0, g]
        vc = vc_ref[0, g]
        p_sum_t = jnp.zeros((n_blk, tq), F32)
        for hh in range(hpg):
            sl = slice((g * hpg + hh) * LANES, (g * hpg + hh + 1) * LANES)
            q = q_ref[0][:, sl]
            s = jnp.where(valid, _dot_nt(q, kc) * scale, NEG_INF)
            m = jnp.max(s, axis=1, keepdims=True)
            m = jnp.where(m == NEG_INF, 0.0, m)
            e = jnp.exp(s - m)
            p = e / jnp.maximum(jnp.sum(e, axis=1, keepdims=True), 1e-30)
            oc_ref[0, :, sl] = _dot(p.astype(BF16), vc)
            st = jnp.where(valid_t, _dot_nt(kc, q) * scale, NEG_INF)
            mt = jnp.max(st, axis=0, keepdims=True)
            mt = jnp.where(mt == NEG_INF, 0.0, mt)
            et = jnp.exp(st - mt)
            p_sum_t = p_sum_t + et / jnp.maximum(jnp.sum(et, axis=0, keepdims=True), 1e-30)
        hi = p_sum_t.astype(BF16)
        lo = (p_sum_t - hi.astype(F32)).astype(BF16)
        imp = _dot(ov_ref[...], hi) + _dot(ov_ref[...], lo)
        imp = jnp.where((j_idx == 0) | (j_idx == cur) | (j_idx == cur - 1), FORCE_SCORE, imp)
        imp = jnp.where(j_idx <= cur, imp, NEG_INF)
        rank = jnp.zeros((n_sel, tq), F32)
        for jp in range(n_sel):
            other = imp[jp:jp + 1, :]
            beats = (other > imp) | ((other == imp) & (j_idx > jp))
            rank = rank + beats.astype(F32)
        mt_ref[0, g] = (rank < float(min(SEL_TOP_N, n_sel))).astype(F32)


def nsa_cmp_attention(u_r, kc, vc, tq):
    b, s, _ = u_r.shape
    n_blk = s // CMP_STRIDE
    n_sel = s // SEL_BLOCK
    ci = jnp.arange(n_blk)[None, :] * CMP_STRIDE
    sj = jnp.arange(n_sel)[:, None] * SEL_BLOCK
    overlap_t = ((ci < sj + SEL_BLOCK) & (ci + CMP_BLOCK > sj)).astype(BF16)
    cblk = pl.BlockSpec((1, N_KV_NSA, n_blk, HEAD_DIM), lambda b_, i: (b_, 0, 0, 0))
    return pl.pallas_call(
        functools.partial(_cmp_attn_kernel, tq=tq, n_blk=n_blk, n_sel=n_sel),
        grid=(b, s // tq),
        in_specs=[pl.BlockSpec((1, tq, BRANCH_WIDTH), lambda b_, i: (b_, i, COL_R["nq"] // N_HEADS)), cblk, cblk,
                  pl.BlockSpec((n_sel, n_blk), lambda b_, i: (0, 0))],
        out_specs=[pl.BlockSpec((1, tq, BRANCH_WIDTH), lambda b_, i: (b_, i, 0)),
                   pl.BlockSpec((1, N_KV_NSA, n_sel, tq), lambda b_, i: (b_, 0, 0, i))],
        out_shape=[jax.ShapeDtypeStruct((b, s, BRANCH_WIDTH), F32),
                   jax.ShapeDtypeStruct((b, N_KV_NSA, n_sel, s), F32)],
        compiler_params=_cparams(("parallel", "parallel"), 40),
        name="nsa_cmp_attention",
    )(u_r, kc, vc, overlap_t)


def _sel_kernel(q_ref, k_ref, vt_ref, mt_ref, o_ref, m_ref, l_ref, acc_ref, *, t):
    qi = pl.program_id(2)
    c = HEAD_DIM ** -0.5 * LOG2E
    hpg = N_HEADS // N_KV_NSA
    per_tile = t // SEL_BLOCK
    causal_t = lax.broadcasted_iota(jnp.int32, (t, t), 0) <= lax.broadcasted_iota(jnp.int32, (t, t), 1)
    qs = [q_ref[0][:, hh * LANES:(hh + 1) * LANES] for hh in range(hpg)]

    def scores(hh, j):
        return _dot_nt(k_ref[0, j * t:(j + 1) * t, :], qs[hh]) * c

    def tile_body(n):
        _flash_init(m_ref, l_ref, acc_ref)
        steps = [(hh, j) for j in range(n + 1) for hh in range(hpg)]
        at = scores(*steps[0])
        for i, (hh, j) in enumerate(steps):
            at_next = scores(*steps[i + 1]) if i + 1 < len(steps) else None
            rows = [jnp.broadcast_to(mt_ref[0, 0, j * per_tile + r:j * per_tile + r + 1, :], (SEL_BLOCK, t))
                    for r in range(per_tile)]
            mask = jnp.concatenate(rows, axis=0) > 0.5
            if j == n:
                mask = mask & causal_t
            at = jnp.where(mask, at, NEG_INF)
            m_prev = m_ref[hh]
            m_new = jnp.maximum(m_prev, jnp.max(at, axis=0, keepdims=True))
            alpha = jnp.exp2(m_prev - m_new)
            p = jnp.exp2(at - m_new)
            l_ref[hh] = alpha * l_ref[hh] + jnp.sum(p, axis=0, keepdims=True)
            acc_ref[hh] = alpha * acc_ref[hh] + _dot(vt_ref[0, :, j * t:(j + 1) * t], p.astype(BF16))
            m_ref[hh] = m_new
            at = at_next
        for hh in range(hpg):
            o_ref[0, :, hh * LANES:(hh + 1) * LANES] = (acc_ref[hh] / l_ref[hh]).T

    for n in range(k_ref.shape[1] // t):
        pl.when(qi == n)(functools.partial(tile_body, n))


def nsa_sel_attention(u_r, vs_t, mt, t):
    b, s, _ = u_r.shape
    n_sel = s // SEL_BLOCK
    hpg = N_HEADS // N_KV_NSA
    cq, ck = COL_R["nq"] // hpg, COL_R["nks"]
    return pl.pallas_call(
        functools.partial(_sel_kernel, t=t),
        grid=(b, N_KV_NSA, s // t),
        in_specs=[pl.BlockSpec((1, t, hpg * LANES), lambda b_, g, i: (b_, i, cq + g)),
                  pl.BlockSpec((1, s, LANES), lambda b_, g, i: (b_, 0, ck + g)),
                  pl.BlockSpec((1, LANES, s), lambda b_, g, i: (b_, g, 0)),
                  pl.BlockSpec((1, 1, n_sel, t), lambda b_, g, i: (b_, g, 0, i))],
        out_specs=pl.BlockSpec((1, t, hpg * LANES), lambda b_, g, i: (b_, i, g)),
        out_shape=jax.ShapeDtypeStruct((b, s, BRANCH_WIDTH), F32),
        scratch_shapes=[pltpu.VMEM((hpg, 1, t), F32), pltpu.VMEM((hpg, 1, t), F32),
                        pltpu.VMEM((hpg, HEAD_DIM, t), F32)],
        compiler_params=_cparams(("parallel", "parallel", "arbitrary"), 40),
        name="nsa_sel_attention",
    )(u_r, u_r, vs_t, mt)


def _win_kernel(q_ref, k_ref, kp_ref, v_ref, vp_ref, oc_ref, os_ref, small_ref, o_ref, m_ref, l_ref, acc_ref, *, t):
    qi = pl.program_id(1)
    c = HEAD_DIM ** -0.5 * LOG2E
    hpg = N_HEADS // N_KV_NSA
    row = lax.broadcasted_iota(jnp.int32, (t, t), 0)
    col = lax.broadcasted_iota(jnp.int32, (t, t), 1)
    gates = jax.nn.sigmoid(small_ref[0])

    def tile_body(with_prev):
        heads = [(g, hh) for g in range(N_KV_NSA) for hh in range(hpg)]

        def scores(g, hh):
            gl = slice(g * LANES, (g + 1) * LANES)
            q = q_ref[0][:, (g * hpg + hh) * LANES:(g * hpg + hh + 1) * LANES]
            a = jnp.where(col <= row, _dot_nt(q, k_ref[0][:, gl]) * c, NEG_INF)
            ap = jnp.where(col > row, _dot_nt(q, kp_ref[0][:, gl]) * c, NEG_INF) if with_prev else None
            return a, ap

        nxt = scores(*heads[0])
        for i, (g, hh) in enumerate(heads):
            a, ap = nxt
            nxt = scores(*heads[i + 1]) if i + 1 < len(heads) else None
            gl = slice(g * LANES, (g + 1) * LANES)
            head = g * hpg + hh
            sl = slice(head * LANES, (head + 1) * LANES)
            _flash_init(m_ref, l_ref, acc_ref)
            _flash_step(a, v_ref[0][:, gl], m_ref, l_ref, acc_ref)
            if with_prev:
                _flash_step(ap, vp_ref[0][:, gl], m_ref, l_ref, acc_ref)
            o_w = acc_ref[...] / l_ref[...]
            cg = 4 + 3 * head
            o = (gates[:, cg:cg + 1] * oc_ref[0][:, sl] + gates[:, cg + 1:cg + 2] * os_ref[0][:, sl]
                 + gates[:, cg + 2:cg + 3] * o_w)
            o_ref[0, :, sl] = o.astype(o_ref.dtype)

    pl.when(qi == 0)(functools.partial(tile_body, False))
    pl.when(qi > 0)(functools.partial(tile_body, True))


def nsa_win_attention(u_r, u_p, o_c, o_s, small, t):
    b, s, _ = u_r.shape
    assert t == WINDOW
    wide = pl.BlockSpec((1, t, BRANCH_WIDTH), lambda b_, i: (b_, i, 0))
    ck, cv = COL_R["nkw"] // 2, COL_P["nvw"] // 2
    stat = pltpu.VMEM((t, LANES), F32)
    return pl.pallas_call(
        functools.partial(_win_kernel, t=t),
        grid=(b, s // t),
        in_specs=[pl.BlockSpec((1, t, BRANCH_WIDTH), lambda b_, i: (b_, i, COL_R["nq"] // N_HEADS)),
                  pl.BlockSpec((1, t, 2 * LANES), lambda b_, i: (b_, i, ck)),
                  pl.BlockSpec((1, t, 2 * LANES), lambda b_, i: (b_, jnp.maximum(i - 1, 0), ck)),
                  pl.BlockSpec((1, t, 2 * LANES), lambda b_, i: (b_, i, cv)),
                  pl.BlockSpec((1, t, 2 * LANES), lambda b_, i: (b_, jnp.maximum(i - 1, 0), cv)),
                  wide, wide,
                  pl.BlockSpec((1, t, LANES), lambda b_, i: (b_, i, 0))],
        out_specs=wide,
        out_shape=jax.ShapeDtypeStruct((b, s, BRANCH_WIDTH), BF16),
        scratch_shapes=[stat, stat, stat],
        compiler_params=_cparams(("parallel", "arbitrary"), 48),
        name="nsa_win_attention",
    )(u_r, u_r, u_r, u_p, u_p, o_c, o_s, small)


def _merge_kernel(h_ref, o0_ref, o1_ref, o2_ref, o3_ref, wg_ref, wb_ref, out_ref):
    h = h_ref[...]
    acc = None
    for n, o_ref in enumerate((o0_ref, o1_ref, o2_ref, o3_ref)):
        term = jax.nn.sigmoid(_dot(h, wg_ref[n])) * _dot(o_ref[...], wb_ref[n])
        acc = term if acc is None else acc + term
    out_ref[...] = acc.astype(out_ref.dtype)


def merge_branches(h, branches, wg, wb, tm, tn):
    n, d = h.shape
    bspec = pl.BlockSpec((tm, BRANCH_WIDTH), lambda j, i: (i, 0))
    return pl.pallas_call(
        _merge_kernel,
        grid=(d // tn, n // tm),
        in_specs=[pl.BlockSpec((tm, d), lambda j, i: (i, 0)), bspec, bspec, bspec, bspec,
                  pl.BlockSpec((4, d, tn), lambda j, i: (0, 0, j)),
                  pl.BlockSpec((4, BRANCH_WIDTH, tn), lambda j, i: (0, 0, j))],
        out_specs=pl.BlockSpec((tm, tn), lambda j, i: (i, j)),
        out_shape=jax.ShapeDtypeStruct((n, d), BF16),
        compiler_params=_cparams(("parallel", "arbitrary"), 56),
        name="merge_branches",
    )(h, *branches, wg, wb)


def _out_kernel(mg_ref, x_ref, w_ref, g_ref, wr_ref, br_ref, x1_ref, h2_ref, route_ref, *, tm):
    x1 = x_ref[...] + _dot(mg_ref[...], w_ref[...])
    x1_ref[...] = x1
    h2 = (x1 * lax.rsqrt(jnp.mean(x1 * x1, axis=-1, keepdims=True) + NORM_EPS)) * g_ref[...]
    for s in range(ROW_TILES):
        h2_ref[pl.ds(s, tm, stride=ROW_TILES), :] = h2[:, s * LANES:(s + 1) * LANES]
    logits = _dot(h2.astype(BF16), wr_ref[...]) + br_ref[...]
    lane = lax.broadcasted_iota(jnp.int32, logits.shape, 1)
    lane_f = lane.astype(F32)
    big = float(LANES)
    is_g = lane < N_GROUPS
    lg = jnp.where(is_g, logits, NEG_INF)
    mx = jnp.max(lg, axis=1, keepdims=True)
    gi = jnp.min(jnp.where(lg == mx, lane_f, big), axis=1, keepdims=True)
    pg = 1.0 / jnp.sum(jnp.where(is_g, jnp.exp(lg - mx), 0.0), axis=1, keepdims=True)
    e_idx = lane - N_GROUPS
    in_grp = (lane >= N_GROUPS) & (lane < N_GROUPS + N_EXPERTS) & ((e_idx // EXPERTS_PER_GROUP).astype(F32) == gi)
    le = jnp.where(in_grp, logits, NEG_INF)
    v1 = jnp.max(le, axis=1, keepdims=True)
    i1 = jnp.min(jnp.where(le == v1, lane_f, big), axis=1, keepdims=True)
    le2 = jnp.where(lane_f == i1, NEG_INF, le)
    v2 = jnp.max(le2, axis=1, keepdims=True)
    i2 = jnp.min(jnp.where(le2 == v2, lane_f, big), axis=1, keepdims=True)
    e2 = jnp.exp(v2 - v1)
    w1 = pg / (1.0 + e2)
    w2 = pg * e2 / (1.0 + e2)
    route = jnp.where(lane == 0, i1 - N_GROUPS, 0.0)
    route = jnp.where(lane == 1, i2 - N_GROUPS, route)
    route = jnp.where(lane == 2, w1, route)
    route = jnp.where(lane == 3, w2, route)
    route_ref[...] = route


def out_proj_router(merged, x, w_out, g2, w_router, b_router, tm):
    n, d = x.shape
    row = pl.BlockSpec((tm, d), lambda i: (i, 0))
    return pl.pallas_call(
        functools.partial(_out_kernel, tm=tm),
        grid=(n // tm,),
        in_specs=[row, row, pl.BlockSpec((d, d), lambda i: (0, 0)), pl.BlockSpec((1, d), lambda i: (0, 0)),
                  pl.BlockSpec((d, LANES), lambda i: (0, 0)), pl.BlockSpec((1, LANES), lambda i: (0, 0))],
        out_specs=[row, pl.BlockSpec((tm * ROW_TILES, LANES), lambda i: (i, 0)),
                   pl.BlockSpec((tm, LANES), lambda i: (i, 0))],
        out_shape=[jax.ShapeDtypeStruct((n, d), F32), jax.ShapeDtypeStruct((n * ROW_TILES, LANES), F32),
                   jax.ShapeDtypeStruct((n, LANES), F32)],
        compiler_params=_cparams(("parallel",), 56),
        name="out_proj_router",
    )(merged, x, w_out, g2.reshape(1, d), w_router, b_router)


def _row_copy(src_hbm, src_row, buf_ref, r, sem):
    src = pl.multiple_of(src_row * ROW_TILES, ROW_TILES)
    return pltpu.make_async_copy(src_hbm.at[pl.ds(src, ROW_TILES), :],
                                 buf_ref.at[pl.ds(r * ROW_TILES, ROW_TILES), :], sem)


def _start_rows(src_hbm, idx_ref, n_rows, buf_ref, sem):
    for r in range(n_rows):
        _row_copy(src_hbm, idx_ref[0, 0, r], buf_ref, r, sem).start(priority=r % 2)


def _wait_rows(src_hbm, buf_ref, sem):
    pltpu.make_async_copy(src_hbm.at[pl.ds(0, buf_ref.shape[0]), :], buf_ref, sem).wait()


def _ffn_kernel(blk_e_ref, nxt_e_ref, n_used_ref, tok_ref, tok_next_ref, sw_ref, h2_hbm, wg_hbm, wu_hbm, wd_hbm, y_ref,
                xa_ref, xb_ref, sg_ref, su_ref, sd_ref, wg_ref, wu_ref, wd_ref, sem, wsem, *, tm, layer):
    blk = pl.program_id(0)
    n_used = n_used_ref[0]
    even = blk % 2 == 0
    cur_e = blk_e_ref[blk]

    def weight_copies(e):
        pairs = ((wg_hbm, sg_ref), (wu_hbm, su_ref), (wd_hbm, sd_ref))
        return [pltpu.make_async_copy(src.at[layer, e], dst, wsem.at[i]) for i, (src, dst) in enumerate(pairs)]

    @pl.when(blk == 0)
    def _():
        _start_rows(h2_hbm, tok_ref, tm, xa_ref, sem.at[0])
        for cp in weight_copies(cur_e):
            cp.start()

    @pl.when((blk < n_used) & ((blk == 0) | (cur_e != blk_e_ref[jnp.maximum(blk - 1, 0)])))
    def _():
        for cp in weight_copies(cur_e):
            cp.wait()
        rows = 128
        for src, dst in ((sg_ref, wg_ref), (su_ref, wu_ref), (sd_ref, wd_ref)):
            def cast_rows(i, carry, src=src, dst=dst):
                r0 = pl.multiple_of(i * rows, rows)
                dst[0, pl.ds(r0, rows), :] = src[pl.ds(r0, rows), :].astype(BF16)
                return carry

            lax.fori_loop(0, src.shape[0] // rows, cast_rows, 0)
        nxt_e = nxt_e_ref[blk]

        @pl.when(nxt_e != cur_e)
        def _():
            for cp in weight_copies(nxt_e):
                cp.start()

    def work(cur_ref, cur_sem, nxt_ref, nxt_sem):
        _wait_rows(h2_hbm, cur_ref, cur_sem)
        _start_rows(h2_hbm, tok_next_ref, tm, nxt_ref, nxt_sem)
        x = jnp.concatenate([cur_ref[pl.ds(s, tm, stride=ROW_TILES), :] for s in range(ROW_TILES)],
                            axis=1).astype(BF16)
        gate = _dot(x, wg_ref[0])
        up = _dot(x, wu_ref[0])
        hid = (gate * jax.nn.sigmoid(gate) * up).astype(BF16)
        y = _dot(hid, wd_ref[0]) * sw_ref[...]
        for s in range(ROW_TILES):
            y_ref[pl.ds(s, tm, stride=ROW_TILES), :] = y[:, s * LANES:(s + 1) * LANES]

    @pl.when((blk < n_used) & even)
    def _():
        work(xa_ref, sem.at[0], xb_ref, sem.at[1])

    @pl.when((blk < n_used) & jnp.logical_not(even))
    def _():
        work(xb_ref, sem.at[1], xa_ref, sem.at[0])

    @pl.when(blk >= n_used)
    def _():
        y_ref[...] = jnp.zeros(y_ref.shape, F32)

    @pl.when((blk == n_used) & even)
    def _():
        _wait_rows(h2_hbm, xa_ref, sem.at[0])

    @pl.when((blk == n_used) & jnp.logical_not(even))
    def _():
        _wait_rows(h2_hbm, xb_ref, sem.at[1])


def moe_experts(blk_e, nxt_e, n_used, buf_tok, buf_w, h2_rows, w_g, w_u, w_d, layer, tm):
    n_blk = blk_e.shape[0]
    d, ff = w_g.shape[2], w_g.shape[3]
    tok = buf_tok.reshape(n_blk, 1, tm)
    hbm = pl.BlockSpec(memory_space=pl.ANY)
    grid_spec = pltpu.PrefetchScalarGridSpec(
        num_scalar_prefetch=3,
        grid=(n_blk,),
        in_specs=[
            pl.BlockSpec((1, 1, tm), lambda i, *_: (i, 0, 0), memory_space=pltpu.SMEM),
            pl.BlockSpec((1, 1, tm), lambda i, *_: (jnp.minimum(i + 1, n_blk - 1), 0, 0), memory_space=pltpu.SMEM),
            pl.BlockSpec((tm, 1), lambda i, *_: (i, 0)),
            hbm, hbm, hbm, hbm,
        ],
        out_specs=pl.BlockSpec((tm * ROW_TILES, LANES), lambda i, *_: (i, 0)),
        scratch_shapes=[pltpu.VMEM((tm * ROW_TILES, LANES), F32), pltpu.VMEM((tm * ROW_TILES, LANES), F32),
                        pltpu.VMEM((d, ff), F32), pltpu.VMEM((d, ff), F32), pltpu.VMEM((ff, d), F32),
                        pltpu.VMEM((1, d, ff), BF16), pltpu.VMEM((1, d, ff), BF16), pltpu.VMEM((1, ff, d), BF16),
                        pltpu.SemaphoreType.DMA((2,)), pltpu.SemaphoreType.DMA((3,))],
    )
    return pl.pallas_call(
        functools.partial(_ffn_kernel, tm=tm, layer=layer),
        grid_spec=grid_spec,
        out_shape=jax.ShapeDtypeStruct((n_blk * tm * ROW_TILES, LANES), F32),
        compiler_params=_cparams(("arbitrary",), 56),
        name="moe_experts",
    )(blk_e, nxt_e, n_used, tok, tok, buf_w.reshape(n_blk * tm, 1), h2_rows, w_g, w_u, w_d)


def _combine_kernel(pos_ref, pos_next_ref, x1_ref, yb_hbm, g_ref, x2_ref, hn_ref, ba_ref, bb_ref, sem, *, tm, n_tiles):
    i = pl.program_id(0)
    even = i % 2 == 0

    @pl.when(i == 0)
    def _():
        _start_rows(yb_hbm, pos_ref, 2 * tm, ba_ref, sem.at[0])

    def work(cur_ref, cur_sem, nxt_ref, nxt_sem):
        _wait_rows(yb_hbm, cur_ref, cur_sem)
        if nxt_ref is not None:
            _start_rows(yb_hbm, pos_next_ref, 2 * tm, nxt_ref, nxt_sem)
        pieces = []
        ssq = jnp.zeros((tm, 1), F32)
        for s in range(ROW_TILES):
            piece = (x1_ref[:, s * LANES:(s + 1) * LANES]
                     + cur_ref[pl.ds(s, tm, stride=2 * ROW_TILES), :]
                     + cur_ref[pl.ds(ROW_TILES + s, tm, stride=2 * ROW_TILES), :])
            x2_ref[:, s * LANES:(s + 1) * LANES] = piece
            ssq = ssq + jnp.sum(piece * piece, axis=1, keepdims=True)
            pieces.append(piece)
        inv = lax.rsqrt(ssq / D_MODEL + NORM_EPS)
        for s in range(ROW_TILES):
            sl = slice(s * LANES, (s + 1) * LANES)
            hn_ref[:, sl] = ((pieces[s] * inv) * g_ref[:, sl]).astype(hn_ref.dtype)

    @pl.when((i < n_tiles - 1) & even)
    def _():
        work(ba_ref, sem.at[0], bb_ref, sem.at[1])

    @pl.when((i < n_tiles - 1) & jnp.logical_not(even))
    def _():
        work(bb_ref, sem.at[1], ba_ref, sem.at[0])

    @pl.when(i == n_tiles - 1)
    def _():
        work(bb_ref, sem.at[1], None, None)


def moe_combine(pos, x1, yb_rows, g, hn_dtype, tm):
    n, d = x1.shape
    n_tiles = n // tm
    assert n_tiles % 2 == 0
    row = pl.BlockSpec((tm, d), lambda i: (i, 0))
    pos3 = pos.reshape(n_tiles, 1, 2 * tm)
    return pl.pallas_call(
        functools.partial(_combine_kernel, tm=tm, n_tiles=n_tiles),
        grid=(n_tiles,),
        in_specs=[pl.BlockSpec((1, 1, 2 * tm), lambda i: (i, 0, 0), memory_space=pltpu.SMEM),
                  pl.BlockSpec((1, 1, 2 * tm), lambda i: (jnp.minimum(i + 1, n_tiles - 1), 0, 0), memory_space=pltpu.SMEM),
                  row, pl.BlockSpec(memory_space=pl.ANY), pl.BlockSpec((1, d), lambda i: (0, 0))],
        out_specs=[row, row],
        out_shape=[jax.ShapeDtypeStruct((n, d), F32), jax.ShapeDtypeStruct((n, d), hn_dtype)],
        scratch_shapes=[pltpu.VMEM((2 * tm * ROW_TILES, LANES), F32), pltpu.VMEM((2 * tm * ROW_TILES, LANES), F32),
                        pltpu.SemaphoreType.DMA((2,))],
        compiler_params=_cparams(("arbitrary",), 48),
        name="moe_combine",
    )(pos3, pos3, x1, yb_rows, g.reshape(1, d))


def _dispatch_plan(route, tm):
    n = route.shape[0]
    m = 2 * n
    slot_e = route[:, 0:2].astype(jnp.int32).reshape(m)
    slot_w = route[:, 2:4].reshape(m)
    iota = jnp.arange(m, dtype=jnp.int32)
    se, order, sw = lax.sort((slot_e, iota, slot_w), num_keys=1)
    e_ids = jnp.arange(N_EXPERTS, dtype=jnp.int32)
    counts = jnp.sum((slot_e[None, :] == e_ids[:, None]).astype(jnp.int32), axis=1)
    start = jnp.cumsum(counts) - counts
    padded = (counts + tm - 1) // tm * tm
    pend = jnp.cumsum(padded)
    pstart = pend - padded
    delta = pstart - start
    dest = iota + jnp.sum(jnp.where(se[:, None] == e_ids[None, :], delta[None, :], 0), axis=1)
    _, pos = lax.sort((order, dest), num_keys=1)
    n_blk = (m + N_EXPERTS * tm) // tm
    blk_row0 = jnp.arange(n_blk, dtype=jnp.int32) * tm
    blk_e = jnp.minimum(jnp.sum((pend[None, :] <= blk_row0[:, None]).astype(jnp.int32), axis=1), N_EXPERTS - 1)
    onehot = blk_e[:, None] == e_ids[None, :]
    pick = lambda tab: jnp.sum(jnp.where(onehot, tab[None, :], 0), axis=1)
    local0 = blk_row0 - pick(pstart)
    n_valid = jnp.clip(pick(counts) - local0, 0, tm)
    within = jnp.arange(tm, dtype=jnp.int32)[None, :]
    src = jnp.clip(pick(start)[:, None] + local0[:, None] + within, 0, m - 1)
    valid = within < n_valid[:, None]
    buf_tok = jnp.where(valid, jnp.take(order, src) // 2, 0)
    buf_w = jnp.where(valid, jnp.take(sw, src), 0.0)
    n_used = (pend[-1] // tm).astype(jnp.int32).reshape(1)
    later = (counts[None, :] > 0) & (e_ids[None, :] > e_ids[:, None])
    nxt_tab = jnp.min(jnp.where(later, e_ids[None, :], N_EXPERTS), axis=1)
    nxt_tab = jnp.where(nxt_tab == N_EXPERTS, e_ids, nxt_tab)
    return blk_e.astype(jnp.int32), pick(nxt_tab).astype(jnp.int32), n_used, buf_tok, buf_w, pos


def _rope_tables(s):
    pos = jnp.arange(s, dtype=F32)[:, None]
    inv128 = jnp.exp(-math.log(ROPE_THETA) * jnp.arange(0, HEAD_DIM, 2, dtype=F32) / HEAD_DIM)
    a = pos * inv128[None, :]
    c128 = jnp.concatenate([jnp.cos(a), jnp.cos(a)], axis=1)
    s128 = jnp.concatenate([-jnp.sin(a), jnp.sin(a)], axis=1)
    inv64 = jnp.exp(-math.log(ROPE_THETA) * jnp.arange(0, DIFF_DIM, 2, dtype=F32) / DIFF_DIM)
    a = pos * inv64[None, :]
    co, si, z = jnp.cos(a), jnp.sin(a), jnp.zeros_like(a)
    c64 = jnp.concatenate([co, co, co, co], axis=1)
    sa64 = jnp.concatenate([-si, z, -si, z], axis=1)
    sb64 = jnp.concatenate([z, si, z, si], axis=1)
    return (c128, s128), (c64, sa64, sb64)


def _orig_offsets():
    offs, o = {}, 0
    for name, w in zip(_ORIG_NAMES, _ORIG_WIDTHS):
        offs[name] = (o, w)
        o += w
    return offs


def _w_in_prep_kernel(w_ref, p_ref, r_ref, d_ref, s_ref):
    offs = _orig_offsets()
    for pieces, dst_ref in ((_PLAIN, p_ref), (_ROPE128, r_ref), (_ROPE64, d_ref)):
        dst = 0
        for name, width in pieces:
            src = offs[name][0]
            dst_ref[:, dst:dst + width] = w_ref[0, :, src:src + width].astype(BF16)
            dst += width
    (f0, fw), (g0, gw) = offs["ff"], offs["ngt"]
    s_ref[...] = jnp.zeros(s_ref.shape, BF16)
    s_ref[:, 0:fw] = w_ref[0, :, f0:f0 + fw].astype(BF16)
    s_ref[:, fw:fw + gw] = w_ref[0, :, g0:g0 + gw].astype(BF16)


def split_w_in(w_in, layer, tr=256):
    _, k, c = w_in.shape
    row = lambda w: pl.BlockSpec((tr, w), lambda i: (i, 0))
    shape = lambda w: jax.ShapeDtypeStruct((k, w), BF16)
    return pl.pallas_call(
        _w_in_prep_kernel,
        grid=(k // tr,),
        in_specs=[pl.BlockSpec((1, tr, c), lambda i: (layer, i, 0))],
        out_specs=[row(C_PLAIN), row(C_ROPE128), row(C_ROPE64), row(LANES)],
        out_shape=[shape(C_PLAIN), shape(C_ROPE128), shape(C_ROPE64), shape(LANES)],
        compiler_params=_cparams(("parallel",), 48),
        name="w_in_prep",
    )(w_in)


def _tiles(s, n):
    return dict(t_attn=min(512, s), tq_nsa=min(256, s), tk_sb=min(256, s), tm_norm=min(512, n), tm_proj=min(512, n),
                tm_out=min(512, n), tm_comb=min(256, n), tm_moe=256)


def kernel(x, norm1_g, w_in, fox_bf, nsa_pe_k, nsa_pe_v, nsa_w_ck, nsa_w_cv, diff_lq1, diff_lk1, diff_lq2, diff_lk2, diff_norm_g, w_branch, w_mgate, w_out, norm2_g, w_rg, b_rg, w_re, b_re, w_eg, w_eu, w_ed, final_g):
    b, s, d = x.shape
    n = b * s
    depth = w_in.shape[0]
    t = _tiles(s, n)
    rope128_tabs, rope64_tabs = _rope_tables(s)
    xf = x.reshape(n, d)
    h = rms_norm_rows(xf, norm1_g[0], BF16, t["tm_norm"])
    out = None
    for l in range(depth):
        lam_init = 0.8 - 0.6 * math.exp(-0.3 * l)
        w_p, w_r, w_d, w_s = split_w_in(w_in, l)
        u_p = in_proj_plain(h, w_p, t["tm_proj"], C_PLAIN // 2).reshape(b, s, C_PLAIN)
        u_r, u_d, small = in_proj_rope(h, w_r, w_d, w_s, rope128_tabs, rope64_tabs, s, t["tm_proj"])
        u_r, u_d, small = u_r.reshape(b, s, C_ROPE128), u_d.reshape(b, s, C_ROPE64), small.reshape(b, s, LANES)
        cum = forget_cumsum(small[:, :, 0:N_HEADS].transpose(0, 2, 1), fox_bf[l])
        o_fox = fox_attention(u_p, cum, t["t_attn"])
        o_sb = sb_attention(u_p, t["t_attn"], t["tk_sb"])
        o_diff = diff_attention(u_d, u_p, diff_lq1[l], diff_lk1[l], diff_lq2[l], diff_lk2[l], diff_norm_g[l],
                                lam_init, t["t_attn"])
        kc, vc = nsa_compress(u_r, u_p, nsa_pe_k[l], nsa_pe_v[l], nsa_w_ck[l], nsa_w_cv[l])
        o_c, mt = nsa_cmp_attention(u_r, kc, vc, t["tq_nsa"])
        nvs0 = COL_P["nvs"] * LANES
        vs_t = u_p[:, :, nvs0:nvs0 + 2 * LANES].transpose(0, 2, 1)
        o_s = nsa_sel_attention(u_r, vs_t, mt, t["t_attn"])
        o_nsa = nsa_win_attention(u_r, u_p, o_c, o_s, small, t["t_attn"])
        branches = [o.reshape(n, BRANCH_WIDTH) for o in (o_fox, o_nsa, o_sb, o_diff)]
        merged = merge_branches(h, branches, w_mgate[l].astype(BF16), w_branch[l].astype(BF16), t["tm_proj"], 512)
        w_router = jnp.concatenate([w_rg[l], w_re[l], jnp.zeros((d, LANES - N_GROUPS - N_EXPERTS), F32)], axis=1).astype(BF16)
        b_router = jnp.concatenate([b_rg[l], b_re[l], jnp.zeros((LANES - N_GROUPS - N_EXPERTS,), F32)]).reshape(1, LANES)
        x1, h2_rows, route = out_proj_router(merged, xf, w_out[l].astype(BF16), norm2_g[l], w_router, b_router, t["tm_out"])
        blk_e, nxt_e, n_used, buf_tok, buf_w, pos = _dispatch_plan(route, t["tm_moe"])
        yb_rows = moe_experts(blk_e, nxt_e, n_used, buf_tok, buf_w, h2_rows, w_eg, w_eu, w_ed, l, t["tm_moe"])
        last = l == depth - 1
        g_next = final_g if last else norm1_g[l + 1]
        xf, hn = moe_combine(pos, x1, yb_rows, g_next, F32 if last else BF16, t["tm_comb"])
        h = hn
        out = hn
    return out.reshape(b, s, d)
```

```python
import functools
import math

import jax
import jax.numpy as jnp
from jax import lax
from jax.experimental import pallas as pl
from jax.experimental.pallas import tpu as pltpu

F32 = jnp.float32
BF16 = jnp.bfloat16
NEG_INF = float("-inf")
LOG2E = 1.4426950408889634

D_MODEL = 2048
HEAD_DIM = 128
DIFF_DIM = 64
N_HEADS = 4
N_KV_NSA = 2
BRANCH_WIDTH = 512
ROPE_THETA = 10000.0
NORM_EPS = 1e-6
CMP_BLOCK = 32
CMP_STRIDE = 16
SEL_BLOCK = 64
SEL_TOP_N = 16
WINDOW = 512
FORCE_SCORE = 1e6
N_GROUPS = 4
EXPERTS_PER_GROUP = 8
N_EXPERTS = 32
EXPERT_FF = 1024

LANES = 128
ROW_TILES = D_MODEL // LANES
MIB = 1024 * 1024

_PLAIN = (("fq", 512), ("fk", 512), ("fv", 512), ("nvc", 256), ("nvs", 256), ("nvw", 256),
          ("sq", 512), ("sk", 512), ("sv", 512), ("dv", 512))
_ROPE128 = (("nq", 512), ("nkc", 256), ("nks", 256), ("nkw", 256))
_ROPE64 = (("dq1", 256), ("dq2", 256), ("dk1", 256), ("dk2", 256))


def _layout(pieces):
    col, off = {}, 0
    for name, w in pieces:
        col[name] = off // LANES
        off += w
    return col, off


COL_P, C_PLAIN = _layout(_PLAIN)
COL_R, C_ROPE128 = _layout(_ROPE128)
COL_D, C_ROPE64 = _layout(_ROPE64)

_ORIG_WIDTHS = (512, 512, 512, 4, 512, 256, 256, 256, 256, 256, 256, 12, 512, 512, 512, 256, 256, 256, 256, 512)
_ORIG_NAMES = ("fq", "fk", "fv", "ff", "nq", "nkc", "nvc", "nks", "nvs", "nkw", "nvw", "ngt",
               "sq", "sk", "sv", "dq1", "dq2", "dk1", "dk2", "dv")


def _cparams(sem, vmem_mib):
    return pltpu.CompilerParams(dimension_semantics=sem, vmem_limit_bytes=vmem_mib * MIB)


def _log_sigmoid(z):
    return jnp.minimum(z, 0.0) - jnp.log1p(jnp.exp(-jnp.abs(z)))


def _dot_nt(a, b):
    return lax.dot_general(a, b, (((1,), (1,)), ((), ())), preferred_element_type=F32)


def _dot(a, b):
    return jnp.dot(a, b, preferred_element_type=F32)


def _lanes(x, width):
    return x if width == LANES else jnp.concatenate([x] * (width // LANES), axis=1)


def _norm_kernel(x_ref, g_ref, o_ref):
    x = x_ref[...]
    y = x * lax.rsqrt(jnp.mean(x * x, axis=-1, keepdims=True) + NORM_EPS)
    o_ref[...] = (y * g_ref[...]).astype(o_ref.dtype)


def rms_norm_rows(x, g, out_dtype, tm):
    n, d = x.shape
    return pl.pallas_call(
        _norm_kernel,
        grid=(n // tm,),
        in_specs=[pl.BlockSpec((tm, d), lambda i: (i, 0)), pl.BlockSpec((1, d), lambda i: (0, 0))],
        out_specs=pl.BlockSpec((tm, d), lambda i: (i, 0)),
        out_shape=jax.ShapeDtypeStruct((n, d), out_dtype),
        compiler_params=_cparams(("parallel",), 40),
        name="rms_norm",
    )(x, g.reshape(1, d))


def _mm_kernel(a_ref, w_ref, o_ref):
    o_ref[...] = _dot(a_ref[...], w_ref[...]).astype(o_ref.dtype)


def in_proj_plain(h, w, tm, tn):
    m, k = h.shape
    c = w.shape[1]
    return pl.pallas_call(
        _mm_kernel,
        grid=(c // tn, m // tm),
        in_specs=[pl.BlockSpec((tm, k), lambda j, i: (i, 0)), pl.BlockSpec((k, tn), lambda j, i: (0, j))],
        out_specs=pl.BlockSpec((tm, tn), lambda j, i: (i, j)),
        out_shape=jax.ShapeDtypeStruct((m, c), BF16),
        compiler_params=_cparams(("parallel", "arbitrary"), 48),
        name="in_proj_plain",
    )(h, w)


def _proj_rope_kernel(h_ref, wr_ref, wd_ref, ws_ref, c128_ref, s128_ref, c64_ref, sa64_ref, sb64_ref,
                      r_ref, d_ref, small_ref):
    h = h_ref[...]
    acc = _dot(h, wr_ref[...])
    c, s = c128_ref[...], s128_ref[...]
    for blk in range(C_ROPE128 // LANES):
        x = acc[:, blk * LANES:(blk + 1) * LANES]
        r_ref[:, blk * LANES:(blk + 1) * LANES] = (x * c + pltpu.roll(x, 64, axis=1) * s).astype(BF16)
    acc = _dot(h, wd_ref[...])
    c, sa, sb = c64_ref[...], sa64_ref[...], sb64_ref[...]
    for blk in range(C_ROPE64 // LANES):
        x = acc[:, blk * LANES:(blk + 1) * LANES]
        d_ref[:, blk * LANES:(blk + 1) * LANES] = (
            x * c + pltpu.roll(x, 96, axis=1) * sa + pltpu.roll(x, 32, axis=1) * sb).astype(BF16)
    small_ref[...] = _dot(h, ws_ref[...])


def in_proj_rope(h, w_r, w_d, w_s, rope128_tabs, rope64_tabs, s, tm):
    m, k = h.shape
    per_seq = s // tm
    row = lambda w: pl.BlockSpec((tm, w), lambda i: (i, 0))
    full = lambda w: pl.BlockSpec((k, w), lambda i: (0, 0))
    tab = pl.BlockSpec((tm, LANES), lambda i: (i % per_seq, 0))
    return pl.pallas_call(
        _proj_rope_kernel,
        grid=(m // tm,),
        in_specs=[row(k), full(C_ROPE128), full(C_ROPE64), full(LANES), tab, tab, tab, tab, tab],
        out_specs=[row(C_ROPE128), row(C_ROPE64), row(LANES)],
        out_shape=[jax.ShapeDtypeStruct((m, C_ROPE128), BF16), jax.ShapeDtypeStruct((m, C_ROPE64), BF16),
                   jax.ShapeDtypeStruct((m, LANES), F32)],
        compiler_params=_cparams(("parallel",), 48),
        name="in_proj_rope",
    )(h, w_r, w_d, w_s, *rope128_tabs, *rope64_tabs)


def _cum_kernel(f_ref, b_ref, o_ref):
    z = f_ref[0] + b_ref[...]
    ls = _log_sigmoid(z)
    s = ls.shape[1]
    lane = lax.broadcasted_iota(jnp.int32, ls.shape, 1)
    sh = 1
    while sh < s:
        ls = ls + jnp.where(lane >= sh, pltpu.roll(ls, sh, axis=1), 0.0)
        sh *= 2
    o_ref[0] = ls * LOG2E


def forget_cumsum(f_t, bias):
    b, h, s = f_t.shape
    return pl.pallas_call(
        _cum_kernel,
        grid=(b,),
        in_specs=[pl.BlockSpec((1, h, s), lambda i: (i, 0, 0)), pl.BlockSpec((h, 1), lambda i: (0, 0))],
        out_specs=pl.BlockSpec((1, h, s), lambda i: (i, 0, 0)),
        out_shape=jax.ShapeDtypeStruct((b, h, s), F32),
        name="fox_cumsum",
    )(f_t, bias.reshape(h, 1))


def _flash_init(m_ref, l_ref, acc_ref):
    m_ref[...] = jnp.full(m_ref.shape, NEG_INF, F32)
    l_ref[...] = jnp.zeros(l_ref.shape, F32)
    acc_ref[...] = jnp.zeros(acc_ref.shape, F32)


def _flash_step(a, v, m_ref, l_ref, acc_ref):
    m_prev = m_ref[...]
    m_new = jnp.maximum(m_prev, jnp.max(a, axis=1, keepdims=True))
    alpha = jnp.exp2(m_prev - m_new)
    p = jnp.exp2(a - _lanes(m_new, a.shape[1]))
    l_ref[...] = alpha * l_ref[...] + jnp.sum(p, axis=1, keepdims=True)
    acc_ref[...] = alpha * acc_ref[...] + _dot(p.astype(BF16), v)
    m_ref[...] = m_new


def _causal_diag(t):
    return lax.broadcasted_iota(jnp.int32, (t, t), 1) <= lax.broadcasted_iota(jnp.int32, (t, t), 0)


def _fox_kernel(q_ref, k_ref, v_ref, cum_ref, o_ref, m_ref, l_ref, acc_ref, *, t):
    h = pl.program_id(1)
    qi = pl.program_id(2)
    c = HEAD_DIM ** -0.5 * LOG2E
    q = q_ref[0]

    def logits(j):
        return _dot_nt(q, k_ref[0, j * t:(j + 1) * t, :]) * c - cum_ref[0, pl.ds(h, 1), j * t:(j + 1) * t]

    def tile_body(n):
        _flash_init(m_ref, l_ref, acc_ref)
        a = logits(0)
        for j in range(n + 1):
            a_next = logits(j + 1) if j < n else None
            if j == n:
                a = jnp.where(_causal_diag(t), a, NEG_INF)
            _flash_step(a, v_ref[0, j * t:(j + 1) * t, :], m_ref, l_ref, acc_ref)
            a = a_next
        o_ref[0] = (acc_ref[...] / l_ref[...]).astype(o_ref.dtype)

    for n in range(k_ref.shape[1] // t):
        pl.when(qi == n)(functools.partial(tile_body, n))


def fox_attention(u, cum, t):
    b, s, _ = u.shape
    cq, ck, cv = COL_P["fq"], COL_P["fk"], COL_P["fv"]
    stat = pltpu.VMEM((t, LANES), F32)
    return pl.pallas_call(
        functools.partial(_fox_kernel, t=t),
        grid=(b, N_HEADS, s // t),
        in_specs=[
            pl.BlockSpec((1, t, LANES), lambda b_, h, i: (b_, i, cq + h)),
            pl.BlockSpec((1, s, LANES), lambda b_, h, i: (b_, 0, ck + h)),
            pl.BlockSpec((1, s, LANES), lambda b_, h, i: (b_, 0, cv + h)),
            pl.BlockSpec((1, N_HEADS, s), lambda b_, h, i: (b_, 0, 0)),
        ],
        out_specs=pl.BlockSpec((1, t, LANES), lambda b_, h, i: (b_, i, h)),
        out_shape=jax.ShapeDtypeStruct((b, s, BRANCH_WIDTH), BF16),
        scratch_shapes=[stat, stat, stat],
        compiler_params=_cparams(("parallel", "parallel", "arbitrary"), 40),
        name="fox_attention",
    )(u, u, u, cum)


def _sb_kernel(q_ref, k_ref, v_ref, o_ref, carry_ref, acc_ref, *, tq, tk):
    qi = pl.program_id(2)
    c = HEAD_DIM ** -0.5 * LOG2E
    q = q_ref[0]
    upper = (lax.broadcasted_iota(jnp.int32, (tk, tk), 0) > lax.broadcasted_iota(jnp.int32, (tk, tk), 1)).astype(BF16)
    per_tile = tq // tk

    def logits(c0):
        return _dot_nt(q, k_ref[0, c0:c0 + tk, :]) * c

    def chunk(z2, c0, mask):
        ls = jnp.minimum(z2, 0.0) - jnp.log2(1.0 + jnp.exp2(-jnp.abs(z2)))
        l_neg = ls - z2
        if mask is not None:
            l_neg = jnp.where(mask, l_neg, 0.0)
        hi = l_neg.astype(BF16)
        lo = (l_neg - hi.astype(F32)).astype(BF16)
        after = _lanes(carry_ref[...], tk) + _dot(hi, upper) + _dot(lo, upper)
        a = jnp.exp2(ls + after)
        if mask is not None:
            a = jnp.where(mask, a, 0.0)
        acc_ref[...] += _dot(a.astype(BF16), v_ref[0, c0:c0 + tk, :])
        carry_ref[...] += jnp.sum(l_neg, axis=1, keepdims=True)

    def tile_body(n):
        carry_ref[...] = jnp.zeros(carry_ref.shape, F32)
        acc_ref[...] = jnp.zeros(acc_ref.shape, F32)
        row = lax.broadcasted_iota(jnp.int32, (tq, tk), 0)
        col = lax.broadcasted_iota(jnp.int32, (tq, tk), 1)
        steps = [(n * tq + d * tk, d * tk + col < row) for d in reversed(range(per_tile))]
        steps += [(j * tk, None) for j in reversed(range(n * per_tile))]
        z2 = logits(steps[0][0])
        for i, (c0, mask) in enumerate(steps):
            z2_next = logits(steps[i + 1][0]) if i + 1 < len(steps) else None
            chunk(z2, c0, mask)
            z2 = z2_next
        o_ref[0] = acc_ref[...].astype(o_ref.dtype)

    for n in range(k_ref.shape[1] // tq):
        pl.when(qi == n)(functools.partial(tile_body, n))


def sb_attention(u, tq, tk):
    b, s, _ = u.shape
    cq, ck, cv = COL_P["sq"], COL_P["sk"], COL_P["sv"]
    return pl.pallas_call(
        functools.partial(_sb_kernel, tq=tq, tk=tk),
        grid=(b, N_HEADS, s // tq),
        in_specs=[
            pl.BlockSpec((1, tq, LANES), lambda b_, h, i: (b_, i, cq + h)),
            pl.BlockSpec((1, s, LANES), lambda b_, h, i: (b_, 0, ck + h)),
            pl.BlockSpec((1, s, LANES), lambda b_, h, i: (b_, 0, cv + h)),
        ],
        out_specs=pl.BlockSpec((1, tq, LANES), lambda b_, h, i: (b_, i, h)),
        out_shape=jax.ShapeDtypeStruct((b, s, BRANCH_WIDTH), BF16),
        scratch_shapes=[pltpu.VMEM((tq, LANES), F32), pltpu.VMEM((tq, LANES), F32)],
        compiler_params=_cparams(("parallel", "parallel", "arbitrary"), 40),
        name="sb_attention",
    )(u, u, u)


def _diff_kernel(q1_ref, q2_ref, k1_ref, k2_ref, v_ref, lq1_ref, lk1_ref, lq2_ref, lk2_ref, g_ref, o_ref,
                 m1_ref, l1_ref, a1_ref, m2_ref, l2_ref, a2_ref, *, t, lam_init):
    h = pl.program_id(1)
    qi = pl.program_id(2)
    c = DIFF_DIM ** -0.5 * LOG2E
    mine = (lax.broadcasted_iota(jnp.int32, (t, LANES), 1) // DIFF_DIM) == (h % 2)
    q1 = jnp.where(mine, q1_ref[0], 0.0).astype(BF16)
    q2 = jnp.where(mine, q2_ref[0], 0.0).astype(BF16)
    streams = ((q1, k1_ref, m1_ref, l1_ref, a1_ref), (q2, k2_ref, m2_ref, l2_ref, a2_ref))

    def logits(w, j):
        return _dot_nt(streams[w][0], streams[w][1][0, j * t:(j + 1) * t, :]) * c

    def tile_body(n):
        _flash_init(m1_ref, l1_ref, a1_ref)
        _flash_init(m2_ref, l2_ref, a2_ref)
        steps = [(w, j) for j in range(n + 1) for w in range(2)]
        a = logits(*steps[0])
        for i, (w, j) in enumerate(steps):
            a_next = logits(*steps[i + 1]) if i + 1 < len(steps) else None
            if j == n:
                a = jnp.where(_causal_diag(t), a, NEG_INF)
            _flash_step(a, v_ref[0, j * t:(j + 1) * t, :], *streams[w][2:])
            a = a_next
        lam = (jnp.exp(jnp.sum(lq1_ref[...] * lk1_ref[...], axis=1, keepdims=True))
               - jnp.exp(jnp.sum(lq2_ref[...] * lk2_ref[...], axis=1, keepdims=True)) + lam_init)
        o = a1_ref[...] / l1_ref[...] - lam * (a2_ref[...] / l2_ref[...])
        y = o * lax.rsqrt(jnp.mean(o * o, axis=-1, keepdims=True) + NORM_EPS)
        o_ref[0] = ((y * g_ref[...]) * (1.0 - lam_init)).astype(o_ref.dtype)

    for n in range(v_ref.shape[1] // t):
        pl.when(qi == n)(functools.partial(tile_body, n))


def diff_attention(u_d, u_p, lq1, lk1, lq2, lk2, g, lam_init, t):
    b, s, _ = u_d.shape
    cq1, cq2, ck1, ck2, cv = COL_D["dq1"], COL_D["dq2"], COL_D["dk1"], COL_D["dk2"], COL_P["dv"]
    vec64 = pl.BlockSpec((1, DIFF_DIM), lambda b_, h, i: (0, 0))
    stat = pltpu.VMEM((t, LANES), F32)
    return pl.pallas_call(
        functools.partial(_diff_kernel, t=t, lam_init=lam_init),
        grid=(b, N_HEADS, s // t),
        in_specs=[
            pl.BlockSpec((1, t, LANES), lambda b_, h, i: (b_, i, cq1 + h // 2)),
            pl.BlockSpec((1, t, LANES), lambda b_, h, i: (b_, i, cq2 + h // 2)),
            pl.BlockSpec((1, s, LANES), lambda b_, h, i: (b_, 0, ck1 + h // 2)),
            pl.BlockSpec((1, s, LANES), lambda b_, h, i: (b_, 0, ck2 + h // 2)),
            pl.BlockSpec((1, s, LANES), lambda b_, h, i: (b_, 0, cv + h)),
            vec64, vec64, vec64, vec64,
            pl.BlockSpec((1, HEAD_DIM), lambda b_, h, i: (0, 0)),
        ],
        out_specs=pl.BlockSpec((1, t, LANES), lambda b_, h, i: (b_, i, h)),
        out_shape=jax.ShapeDtypeStruct((b, s, BRANCH_WIDTH), BF16),
        scratch_shapes=[stat] * 6,
        compiler_params=_cparams(("parallel", "parallel", "arbitrary"), 40),
        name="diff_attention",
    )(u_d, u_d, u_d, u_d, u_p,
      lq1.reshape(1, DIFF_DIM), lk1.reshape(1, DIFF_DIM), lq2.reshape(1, DIFF_DIM), lk2.reshape(1, DIFF_DIM),
      g.reshape(1, HEAD_DIM))


def _compress_kernel(kt_ref, vt_ref, pek_ref, pev_ref, wk_ref, wv_ref, kc_ref, vc_ref,
                     xs_ref, xa_ref, xb_ref, *, n_blk):
    half = CMP_BLOCK // 2

    def compress(x, pe_ref, w_ref):
        xs_ref[...] = x.astype(F32)
        for r in range(half):
            piece = xs_ref[pl.ds(r, n_blk, stride=half), :]
            xa_ref[:, r * LANES:(r + 1) * LANES] = (piece + pe_ref[pl.ds(r, 1), :]).astype(BF16)
            xb_ref[:, r * LANES:(r + 1) * LANES] = (piece + pe_ref[pl.ds(half + r, 1), :]).astype(BF16)
        first = _dot(xa_ref[...], w_ref[0])
        second = _dot(xb_ref[...], w_ref[1])
        return first + pltpu.roll(second, n_blk - 1, axis=0)

    for g in range(N_KV_NSA):
        sl = slice(g * LANES, (g + 1) * LANES)
        kc_ref[0, g] = compress(kt_ref[0][:, sl], pek_ref, wk_ref).astype(BF16)
        vc_ref[0, g] = compress(vt_ref[0][:, sl], pev_ref, wv_ref).astype(BF16)


def nsa_compress(u_r, u_p, pe_k, pe_v, w_ck, w_cv):
    b, s, _ = u_r.shape
    n_blk = s // CMP_STRIDE
    half = CMP_BLOCK // 2
    wk = w_ck.reshape(2, half * HEAD_DIM, HEAD_DIM).astype(BF16)
    wv = w_cv.reshape(2, half * HEAD_DIM, HEAD_DIM).astype(BF16)
    pe = pl.BlockSpec((CMP_BLOCK, HEAD_DIM), lambda i: (0, 0))
    wspec = pl.BlockSpec((2, half * HEAD_DIM, HEAD_DIM), lambda i: (0, 0, 0))
    out = pl.BlockSpec((1, N_KV_NSA, n_blk, HEAD_DIM), lambda i: (i, 0, 0, 0))
    return pl.pallas_call(
        functools.partial(_compress_kernel, n_blk=n_blk),
        grid=(b,),
        in_specs=[pl.BlockSpec((1, s, 2 * LANES), lambda i: (i, 0, COL_R["nkc"] // 2)),
                  pl.BlockSpec((1, s, 2 * LANES), lambda i: (i, 0, COL_P["nvc"] // 2)),
                  pe, pe, wspec, wspec],
        out_specs=[out, out],
        out_shape=[jax.ShapeDtypeStruct((b, N_KV_NSA, n_blk, HEAD_DIM), BF16)] * 2,
        scratch_shapes=[pltpu.VMEM((s, LANES), F32), pltpu.VMEM((n_blk, half * LANES), BF16),
                        pltpu.VMEM((n_blk, half * LANES), BF16)],
        compiler_params=_cparams(("parallel",), 40),
        name="nsa_compress",
    )(u_r, u_p, pe_k, pe_v, wk, wv)


def _cmp_attn_kernel(q_ref, kc_ref, vc_ref, ov_ref, oc_ref, mt_ref, *, tq, n_blk, n_sel):
    qi = pl.program_id(1)
    q0 = qi * tq
    scale = HEAD_DIM ** -0.5
    hpg = N_HEADS // N_KV_NSA
    t_row = q0 + lax.broadcasted_iota(jnp.int32, (tq, n_blk), 0)
    n_col = lax.broadcasted_iota(jnp.int32, (tq, n_blk), 1)
    valid = n_col * CMP_STRIDE + (CMP_BLOCK - 1) <= t_row
    t_lane = q0 + lax.broadcasted_iota(jnp.int32, (n_blk, tq), 1)
    n_sub = lax.broadcasted_iota(jnp.int32, (n_blk, tq), 0)
    valid_t = n_sub * CMP_STRIDE + (CMP_BLOCK - 1) <= t_lane
    j_idx = lax.broadcasted_iota(jnp.int32, (n_sel, tq), 0)
    cur = (q0 + lax.broadcasted_iota(jnp.int32, (n_sel, tq), 1)) // SEL_BLOCK
    for g in range(N_KV_NSA):
        kc = kc_ref[0, g]
        vc = vc_ref[0, g]
        p_sum_t = jnp.zeros((n_blk, tq), F32)
        for hh in range(hpg):
            sl = slice((g * hpg + hh) * LANES, (g * hpg + hh + 1) * LANES)
            q = q_ref[0][:, sl]
            s = jnp.where(valid, _dot_nt(q, kc) * scale, NEG_INF)
            m = jnp.max(s, axis=1, keepdims=True)
            m = jnp.where(m == NEG_INF, 0.0, m)
            e = jnp.exp(s - m)
            p = e / jnp.maximum(jnp.sum(e, axis=1, keepdims=True), 1e-30)
            oc_ref[0, :, sl] = _dot(p.astype(BF16), vc)
            st = jnp.where(valid_t, _dot_nt(kc, q) * scale, NEG_INF)
            mt = jnp.max(st, axis=0, keepdims=True)
            mt = jnp.where(mt == NEG_INF, 0.0, mt)
            et = jnp.exp(st - mt)
            p_sum_t = p_sum_t + et / jnp.maximum(jnp.sum(et, axis=0, keepdims=True), 1e-30)
        hi = p_sum_t.astype(BF16)
        lo = (p_sum_t - hi.astype(F32)).astype(BF16)
        imp = _dot(ov_ref[...], hi) + _dot(ov_ref[...], lo)
        imp = jnp.where((j_idx == 0) | (j_idx == cur) | (j_idx == cur - 1), FORCE_SCORE, imp)
        imp = jnp.where(j_idx <= cur, imp, NEG_INF)
        rank = jnp.zeros((n_sel, tq), F32)
        for jp in range(n_sel):
            other = imp[jp:jp + 1, :]
            beats = (other > imp) | ((other == imp) & (j_idx > jp))
            rank = rank + beats.astype(F32)
        mt_ref[0, g] = (rank < float(min(SEL_TOP_N, n_sel))).astype(F32)


def nsa_cmp_attention(u_r, kc, vc, tq):
    b, s, _ = u_r.shape
    n_blk = s // CMP_STRIDE
    n_sel = s // SEL_BLOCK
    ci = jnp.arange(n_blk)[None, :] * CMP_STRIDE
    sj = jnp.arange(n_sel)[:, None] * SEL_BLOCK
    overlap_t = ((ci < sj + SEL_BLOCK) & (ci + CMP_BLOCK > sj)).astype(BF16)
    cblk = pl.BlockSpec((1, N_KV_NSA, n_blk, HEAD_DIM), lambda b_, i: (b_, 0, 0, 0))
    return pl.pallas_call(
        functools.partial(_cmp_attn_kernel, tq=tq, n_blk=n_blk, n_sel=n_sel),
        grid=(b, s // tq),
        in_specs=[pl.BlockSpec((1, tq, BRANCH_WIDTH), lambda b_, i: (b_, i, COL_R["nq"] // N_HEADS)), cblk, cblk,
                  pl.BlockSpec((n_sel, n_blk), lambda b_, i: (0, 0))],
        out_specs=[pl.BlockSpec((1, tq, BRANCH_WIDTH), lambda b_, i: (b_, i, 0)),
                   pl.BlockSpec((1, N_KV_NSA, n_sel, tq), lambda b_, i: (b_, 0, 0, i))],
        out_shape=[jax.ShapeDtypeStruct((b, s, BRANCH_WIDTH), F32),
                   jax.ShapeDtypeStruct((b, N_KV_NSA, n_sel, s), F32)],
        compiler_params=_cparams(("parallel", "parallel"), 40),
        name="nsa_cmp_attention",
    )(u_r, kc, vc, overlap_t)


def _sel_kernel(q_ref, k_ref, vt_ref, mt_ref, o_ref, m_ref, l_ref, acc_ref, *, t):
    qi = pl.program_id(2)
    c = HEAD_DIM ** -0.5 * LOG2E
    hpg = N_HEADS // N_KV_NSA
    per_tile = t // SEL_BLOCK
    causal_t = lax.broadcasted_iota(jnp.int32, (t, t), 0) <= lax.broadcasted_iota(jnp.int32, (t, t), 1)
    qs = [q_ref[0][:, hh * LANES:(hh + 1) * LANES] for hh in range(hpg)]

    def scores(hh, j):
        return _dot_nt(k_ref[0, j * t:(j + 1) * t, :], qs[hh]) * c

    def tile_body(n):
        _flash_init(m_ref, l_ref, acc_ref)
        steps = [(hh, j) for j in range(n + 1) for hh in range(hpg)]
        at = scores(*steps[0])
        for i, (hh, j) in enumerate(steps):
            at_next = scores(*steps[i + 1]) if i + 1 < len(steps) else None
            rows = [jnp.broadcast_to(mt_ref[0, 0, j * per_tile + r:j * per_tile + r + 1, :], (SEL_BLOCK, t))
                    for r in range(per_tile)]
            mask = jnp.concatenate(rows, axis=0) > 0.5
            if j == n:
                mask = mask & causal_t
            at = jnp.where(mask, at, NEG_INF)
            m_prev = m_ref[hh]
            m_new = jnp.maximum(m_prev, jnp.max(at, axis=0, keepdims=True))
            alpha = jnp.exp2(m_prev - m_new)
            p = jnp.exp2(at - m_new)
            l_ref[hh] = alpha * l_ref[hh] + jnp.sum(p, axis=0, keepdims=True)
            acc_ref[hh] = alpha * acc_ref[hh] + _dot(vt_ref[0, :, j * t:(j + 1) * t], p.astype(BF16))
            m_ref[hh] = m_new
            at = at_next
        for hh in range(hpg):
            o_ref[0, :, hh * LANES:(hh + 1) * LANES] = (acc_ref[hh] / l_ref[hh]).T

    for n in range(k_ref.shape[1] // t):
        pl.when(qi == n)(functools.partial(tile_body, n))


def nsa_sel_attention(u_r, vs_t, mt, t):
    b, s, _ = u_r.shape
    n_sel = s // SEL_BLOCK
    hpg = N_HEADS // N_KV_NSA
    cq, ck = COL_R["nq"] // hpg, COL_R["nks"]
    return pl.pallas_call(
        functools.partial(_sel_kernel, t=t),
        grid=(b, N_KV_NSA, s // t),
        in_specs=[pl.BlockSpec((1, t, hpg * LANES), lambda b_, g, i: (b_, i, cq + g)),
                  pl.BlockSpec((1, s, LANES), lambda b_, g, i: (b_, 0, ck + g)),
                  pl.BlockSpec((1, LANES, s), lambda b_, g, i: (b_, g, 0)),
                  pl.BlockSpec((1, 1, n_sel, t), lambda b_, g, i: (b_, g, 0, i))],
        out_specs=pl.BlockSpec((1, t, hpg * LANES), lambda b_, g, i: (b_, i, g)),
        out_shape=jax.ShapeDtypeStruct((b, s, BRANCH_WIDTH), F32),
        scratch_shapes=[pltpu.VMEM((hpg, 1, t), F32), pltpu.VMEM((hpg, 1, t), F32),
                        pltpu.VMEM((hpg, HEAD_DIM, t), F32)],
        compiler_params=_cparams(("parallel", "parallel", "arbitrary"), 40),
        name="nsa_sel_attention",
    )(u_r, u_r, vs_t, mt)


def _win_kernel(q_ref, k_ref, kp_ref, v_ref, vp_ref, oc_ref, os_ref, small_ref, o_ref, m_ref, l_ref, acc_ref, *, t):
    qi = pl.program_id(1)
    c = HEAD_DIM ** -0.5 * LOG2E
    hpg = N_HEADS // N_KV_NSA
    row = lax.broadcasted_iota(jnp.int32, (t, t), 0)
    col = lax.broadcasted_iota(jnp.int32, (t, t), 1)
    gates = jax.nn.sigmoid(small_ref[0])

    def tile_body(with_prev):
        heads = [(g, hh) for g in range(N_KV_NSA) for hh in range(hpg)]

        def scores(g, hh):
            gl = slice(g * LANES, (g + 1) * LANES)
            q = q_ref[0][:, (g * hpg + hh) * LANES:(g * hpg + hh + 1) * LANES]
            a = jnp.where(col <= row, _dot_nt(q, k_ref[0][:, gl]) * c, NEG_INF)
            ap = jnp.where(col > row, _dot_nt(q, kp_ref[0][:, gl]) * c, NEG_INF) if with_prev else None
            return a, ap

        nxt = scores(*heads[0])
        for i, (g, hh) in enumerate(heads):
            a, ap = nxt
            nxt = scores(*heads[i + 1]) if i + 1 < len(heads) else None
            gl = slice(g * LANES, (g + 1) * LANES)
            head = g * hpg + hh
            sl = slice(head * LANES, (head + 1) * LANES)
            _flash_init(m_ref, l_ref, acc_ref)
            _flash_step(a, v_ref[0][:, gl], m_ref, l_ref, acc_ref)
            if with_prev:
                _flash_step(ap, vp_ref[0][:, gl], m_ref, l_ref, acc_ref)
            o_w = acc_ref[...] / l_ref[...]
            cg = 4 + 3 * head
            o = (gates[:, cg:cg + 1] * oc_ref[0][:, sl] + gates[:, cg + 1:cg + 2] * os_ref[0][:, sl]
                 + gates[:, cg + 2:cg + 3] * o_w)
            o_ref[0, :, sl] = o.astype(o_ref.dtype)

    pl.when(qi == 0)(functools.partial(tile_body, False))
    pl.when(qi > 0)(functools.partial(tile_body, True))


def nsa_win_attention(u_r, u_p, o_c, o_s, small, t):
    b, s, _ = u_r.shape
    assert t == WINDOW
    wide = pl.BlockSpec((1, t, BRANCH_WIDTH), lambda b_, i: (b_, i, 0))
    ck, cv = COL_R["nkw"] // 2, COL_P["nvw"] // 2
    stat = pltpu.VMEM((t, LANES), F32)
    return pl.pallas_call(
        functools.partial(_win_kernel, t=t),
        grid=(b, s // t),
        in_specs=[pl.BlockSpec((1, t, BRANCH_WIDTH), lambda b_, i: (b_, i, COL_R["nq"] // N_HEADS)),
                  pl.BlockSpec((1, t, 2 * LANES), lambda b_, i: (b_, i, ck)),
                  pl.BlockSpec((1, t, 2 * LANES), lambda b_, i: (b_, jnp.maximum(i - 1, 0), ck)),
                  pl.BlockSpec((1, t, 2 * LANES), lambda b_, i: (b_, i, cv)),
                  pl.BlockSpec((1, t, 2 * LANES), lambda b_, i: (b_, jnp.maximum(i - 1, 0), cv)),
                  wide, wide,
                  pl.BlockSpec((1, t, LANES), lambda b_, i: (b_, i, 0))],
        out_specs=wide,
        out_shape=jax.ShapeDtypeStruct((b, s, BRANCH_WIDTH), BF16),
        scratch_shapes=[stat, stat, stat],
        compiler_params=_cparams(("parallel", "arbitrary"), 48),
        name="nsa_win_attention",
    )(u_r, u_r, u_r, u_p, u_p, o_c, o_s, small)


def _merge_kernel(h_ref, o0_ref, o1_ref, o2_ref, o3_ref, wg_ref, wb_ref, out_ref):
    h = h_ref[...]
    acc = None
    for n, o_ref in enumerate((o0_ref, o1_ref, o2_ref, o3_ref)):
        term = jax.nn.sigmoid(_dot(h, wg_ref[n])) * _dot(o_ref[...], wb_ref[n])
        acc = term if acc is None else acc + term
    out_ref[...] = acc.astype(out_ref.dtype)


def merge_branches(h, branches, wg, wb, tm, tn):
    n, d = h.shape
    bspec = pl.BlockSpec((tm, BRANCH_WIDTH), lambda j, i: (i, 0))
    return pl.pallas_call(
        _merge_kernel,
        grid=(d // tn, n // tm),
        in_specs=[pl.BlockSpec((tm, d), lambda j, i: (i, 0)), bspec, bspec, bspec, bspec,
                  pl.BlockSpec((4, d, tn), lambda j, i: (0, 0, j)),
                  pl.BlockSpec((4, BRANCH_WIDTH, tn), lambda j, i: (0, 0, j))],
        out_specs=pl.BlockSpec((tm, tn), lambda j, i: (i, j)),
        out_shape=jax.ShapeDtypeStruct((n, d), BF16),
        compiler_params=_cparams(("parallel", "arbitrary"), 56),
        name="merge_branches",
    )(h, *branches, wg, wb)


def _out_kernel(mg_ref, x_ref, w_ref, g_ref, wr_ref, br_ref, x1_ref, h2_ref, route_ref, *, tm):
    n_part = 2 if tm % 32 == 0 else 1
    th = tm // n_part
    parts = [x_ref[p * th:(p + 1) * th, :] + _dot(mg_ref[p * th:(p + 1) * th, :], w_ref[...]) for p in range(n_part)]
    for p in range(n_part):
        _out_epilogue(parts[p], p * th, th, g_ref, wr_ref, br_ref, x1_ref, h2_ref, route_ref)


def _out_epilogue(x1, r0, th, g_ref, wr_ref, br_ref, x1_ref, h2_ref, route_ref):
    x1_ref[r0:r0 + th, :] = x1
    h2 = (x1 * lax.rsqrt(jnp.mean(x1 * x1, axis=-1, keepdims=True) + NORM_EPS)) * g_ref[...]
    for s in range(ROW_TILES):
        h2_ref[pl.ds(r0 * ROW_TILES + s, th, stride=ROW_TILES), :] = h2[:, s * LANES:(s + 1) * LANES]
    logits = _dot(h2.astype(BF16), wr_ref[...]) + br_ref[...]
    lane = lax.broadcasted_iota(jnp.int32, logits.shape, 1)
    lane_f = lane.astype(F32)
    big = float(LANES)
    is_g = lane < N_GROUPS
    lg = jnp.where(is_g, logits, NEG_INF)
    mx = jnp.max(lg, axis=1, keepdims=True)
    gi = jnp.min(jnp.where(lg == mx, lane_f, big), axis=1, keepdims=True)
    pg = 1.0 / jnp.sum(jnp.where(is_g, jnp.exp(lg - mx), 0.0), axis=1, keepdims=True)
    e_idx = lane - N_GROUPS
    in_grp = (lane >= N_GROUPS) & (lane < N_GROUPS + N_EXPERTS) & ((e_idx // EXPERTS_PER_GROUP).astype(F32) == gi)
    le = jnp.where(in_grp, logits, NEG_INF)
    v1 = jnp.max(le, axis=1, keepdims=True)
    i1 = jnp.min(jnp.where(le == v1, lane_f, big), axis=1, keepdims=True)
    le2 = jnp.where(lane_f == i1, NEG_INF, le)
    v2 = jnp.max(le2, axis=1, keepdims=True)
    i2 = jnp.min(jnp.where(le2 == v2, lane_f, big), axis=1, keepdims=True)
    e2 = jnp.exp(v2 - v1)
    w1 = pg / (1.0 + e2)
    w2 = pg * e2 / (1.0 + e2)
    route = jnp.where(lane == 0, i1 - N_GROUPS, 0.0)
    route = jnp.where(lane == 1, i2 - N_GROUPS, route)
    route = jnp.where(lane == 2, w1, route)
    route = jnp.where(lane == 3, w2, route)
    route_ref[r0:r0 + th, :] = route


def out_proj_router(merged, x, w_out, g2, w_router, b_router, tm):
    n, d = x.shape
    row = pl.BlockSpec((tm, d), lambda i: (i, 0))
    return pl.pallas_call(
        functools.partial(_out_kernel, tm=tm),
        grid=(n // tm,),
        in_specs=[row, row, pl.BlockSpec((d, d), lambda i: (0, 0)), pl.BlockSpec((1, d), lambda i: (0, 0)),
                  pl.BlockSpec((d, LANES), lambda i: (0, 0)), pl.BlockSpec((1, LANES), lambda i: (0, 0))],
        out_specs=[row, pl.BlockSpec((tm * ROW_TILES, LANES), lambda i: (i, 0)),
                   pl.BlockSpec((tm, LANES), lambda i: (i, 0))],
        out_shape=[jax.ShapeDtypeStruct((n, d), F32), jax.ShapeDtypeStruct((n * ROW_TILES, LANES), F32),
                   jax.ShapeDtypeStruct((n, LANES), F32)],
        compiler_params=_cparams(("parallel",), 56),
        name="out_proj_router",
    )(merged, x, w_out, g2.reshape(1, d), w_router, b_router)


ROW_PITCH = ROW_TILES + 8


def _start_rows(src_hbm, idx_ref, n_rows, buf_ref, sem, place=lambda r: r):
    for r in range(n_rows):
        src = pl.multiple_of(idx_ref[0, 0, r] * ROW_TILES, ROW_TILES)
        pltpu.make_async_copy(src_hbm.at[pl.ds(src, ROW_TILES), :],
                              buf_ref.at[pl.ds(place(r) * ROW_PITCH, ROW_TILES), :], sem).start(priority=r % 2)


def _wait_rows(src_hbm, n_rows, buf_ref, sem):
    n_sub = n_rows * ROW_TILES
    pltpu.make_async_copy(src_hbm.at[pl.ds(0, n_sub), :], buf_ref.at[pl.ds(0, n_sub), :], sem).wait()


def _ffn_kernel(blk_e_ref, nxt_e_ref, n_used_ref, tok_ref, tok_next_ref, sw_ref, h2_hbm, wg_hbm, wu_hbm, wd_hbm, y_ref,
                xa_ref, xb_ref, sg_ref, su_ref, sd_ref, wg_ref, wu_ref, wd_ref, sem, wsem, *, tm, layer):
    blk = pl.program_id(0)
    n_used = n_used_ref[0]
    even = blk % 2 == 0
    cur_e = blk_e_ref[blk]

    def weight_copies(e):
        pairs = ((wg_hbm, sg_ref), (wu_hbm, su_ref), (wd_hbm, sd_ref))
        return [pltpu.make_async_copy(src.at[layer, e], dst, wsem.at[i]) for i, (src, dst) in enumerate(pairs)]

    @pl.when(blk == 0)
    def _():
        _start_rows(h2_hbm, tok_ref, tm, xa_ref, sem.at[0])
        for cp in weight_copies(cur_e):
            cp.start()

    @pl.when((blk < n_used) & ((blk == 0) | (cur_e != blk_e_ref[jnp.maximum(blk - 1, 0)])))
    def _():
        for cp in weight_copies(cur_e):
            cp.wait()
        rows = 128
        for src, dst in ((sg_ref, wg_ref), (su_ref, wu_ref), (sd_ref, wd_ref)):
            def cast_rows(i, carry, src=src, dst=dst):
                r0 = pl.multiple_of(i * rows, rows)
                dst[0, pl.ds(r0, rows), :] = src[pl.ds(r0, rows), :].astype(BF16)
                return carry

            lax.fori_loop(0, src.shape[0] // rows, cast_rows, 0)
        nxt_e = nxt_e_ref[blk]

        @pl.when(nxt_e != cur_e)
        def _():
            for cp in weight_copies(nxt_e):
                cp.start()

    def work(cur_ref, cur_sem, nxt_ref, nxt_sem):
        _wait_rows(h2_hbm, tm, cur_ref, cur_sem)
        _start_rows(h2_hbm, tok_next_ref, tm, nxt_ref, nxt_sem)
        x = jnp.concatenate([cur_ref[pl.ds(s, tm, stride=ROW_PITCH), :] for s in range(ROW_TILES)],
                            axis=1).astype(BF16)
        gate = _dot(x, wg_ref[0])
        up = _dot(x, wu_ref[0])
        hid = (gate * jax.nn.sigmoid(gate) * up).astype(BF16)
        y = _dot(hid, wd_ref[0]) * sw_ref[...]
        for s in range(ROW_TILES):
            y_ref[pl.ds(s, tm, stride=ROW_TILES), :] = y[:, s * LANES:(s + 1) * LANES]

    @pl.when((blk < n_used) & even)
    def _():
        work(xa_ref, sem.at[0], xb_ref, sem.at[1])

    @pl.when((blk < n_used) & jnp.logical_not(even))
    def _():
        work(xb_ref, sem.at[1], xa_ref, sem.at[0])

    @pl.when(blk >= n_used)
    def _():
        y_ref[...] = jnp.zeros(y_ref.shape, F32)

    @pl.when((blk == n_used) & even)
    def _():
        _wait_rows(h2_hbm, tm, xa_ref, sem.at[0])

    @pl.when((blk == n_used) & jnp.logical_not(even))
    def _():
        _wait_rows(h2_hbm, tm, xb_ref, sem.at[1])


def moe_experts(blk_e, nxt_e, n_used, buf_tok, buf_w, h2_rows, w_g, w_u, w_d, layer, tm):
    n_blk = blk_e.shape[0]
    d, ff = w_g.shape[2], w_g.shape[3]
    tok = buf_tok.reshape(n_blk, 1, tm)
    hbm = pl.BlockSpec(memory_space=pl.ANY)
    grid_spec = pltpu.PrefetchScalarGridSpec(
        num_scalar_prefetch=3,
        grid=(n_blk,),
        in_specs=[
            pl.BlockSpec((1, 1, tm), lambda i, *_: (i, 0, 0), memory_space=pltpu.SMEM),
            pl.BlockSpec((1, 1, tm), lambda i, *_: (jnp.minimum(i + 1, n_blk - 1), 0, 0), memory_space=pltpu.SMEM),
            pl.BlockSpec((tm, 1), lambda i, *_: (i, 0)),
            hbm, hbm, hbm, hbm,
        ],
        out_specs=pl.BlockSpec((tm * ROW_TILES, LANES), lambda i, *_: (i, 0)),
        scratch_shapes=[pltpu.VMEM((tm * ROW_PITCH, LANES), F32), pltpu.VMEM((tm * ROW_PITCH, LANES), F32),
                        pltpu.VMEM((d, ff), F32), pltpu.VMEM((d, ff), F32), pltpu.VMEM((ff, d), F32),
                        pltpu.VMEM((1, d, ff), BF16), pltpu.VMEM((1, d, ff), BF16), pltpu.VMEM((1, ff, d), BF16),
                        pltpu.SemaphoreType.DMA((2,)), pltpu.SemaphoreType.DMA((3,))],
    )
    return pl.pallas_call(
        functools.partial(_ffn_kernel, tm=tm, layer=layer),
        grid_spec=grid_spec,
        out_shape=jax.ShapeDtypeStruct((n_blk * tm * ROW_TILES, LANES), F32),
        compiler_params=_cparams(("arbitrary",), 56),
        name="moe_experts",
    )(blk_e, nxt_e, n_used, tok, tok, buf_w.reshape(n_blk * tm, 1), h2_rows, w_g, w_u, w_d)


def _combine_kernel(pos_ref, pos_next_ref, x1_ref, yb_hbm, g_ref, x2_ref, hn_ref, ba_ref, bb_ref, sem, *, tm, n_tiles):
    i = pl.program_id(0)
    even = i % 2 == 0
    pitch = ROW_PITCH
    place = lambda r: (r % 2) * tm + r // 2

    @pl.when(i == 0)
    def _():
        _start_rows(yb_hbm, pos_ref, 2 * tm, ba_ref, sem.at[0], place)

    def work(cur_ref, cur_sem, nxt_ref, nxt_sem):
        _wait_rows(yb_hbm, 2 * tm, cur_ref, cur_sem)
        if nxt_ref is not None:
            _start_rows(yb_hbm, pos_next_ref, 2 * tm, nxt_ref, nxt_sem, place)
        pieces = []
        ssq = jnp.zeros((tm, 1), F32)
        for s in range(ROW_TILES):
            piece = (x1_ref[:, s * LANES:(s + 1) * LANES]
                     + cur_ref[pl.ds(s, tm, stride=pitch), :]
                     + cur_ref[pl.ds(tm * pitch + s, tm, stride=pitch), :])
            x2_ref[:, s * LANES:(s + 1) * LANES] = piece
            ssq = ssq + jnp.sum(piece * piece, axis=1, keepdims=True)
            pieces.append(piece)
        inv = lax.rsqrt(ssq / D_MODEL + NORM_EPS)
        for s in range(ROW_TILES):
            sl = slice(s * LANES, (s + 1) * LANES)
            hn_ref[:, sl] = ((pieces[s] * inv) * g_ref[:, sl]).astype(hn_ref.dtype)

    @pl.when((i < n_tiles - 1) & even)
    def _():
        work(ba_ref, sem.at[0], bb_ref, sem.at[1])

    @pl.when((i < n_tiles - 1) & jnp.logical_not(even))
    def _():
        work(bb_ref, sem.at[1], ba_ref, sem.at[0])

    @pl.when(i == n_tiles - 1)
    def _():
        work(bb_ref, sem.at[1], None, None)


def moe_combine(pos, x1, yb_rows, g, hn_dtype, tm):
    n, d = x1.shape
    n_tiles = n // tm
    assert n_tiles % 2 == 0
    row = pl.BlockSpec((tm, d), lambda i: (i, 0))
    pos3 = pos.reshape(n_tiles, 1, 2 * tm)
    return pl.pallas_call(
        functools.partial(_combine_kernel, tm=tm, n_tiles=n_tiles),
        grid=(n_tiles,),
        in_specs=[pl.BlockSpec((1, 1, 2 * tm), lambda i: (i, 0, 0), memory_space=pltpu.SMEM),
                  pl.BlockSpec((1, 1, 2 * tm), lambda i: (jnp.minimum(i + 1, n_tiles - 1), 0, 0), memory_space=pltpu.SMEM),
                  row, pl.BlockSpec(memory_space=pl.ANY), pl.BlockSpec((1, d), lambda i: (0, 0))],
        out_specs=[row, row],
        out_shape=[jax.ShapeDtypeStruct((n, d), F32), jax.ShapeDtypeStruct((n, d), hn_dtype)],
        scratch_shapes=[pltpu.VMEM((2 * tm * ROW_PITCH, LANES), F32), pltpu.VMEM((2 * tm * ROW_PITCH, LANES), F32),
                        pltpu.SemaphoreType.DMA((2,))],
        compiler_params=_cparams(("arbitrary",), 48),
        name="moe_combine",
    )(pos3, pos3, x1, yb_rows, g.reshape(1, d))


def _dispatch_plan(route, tm):
    n = route.shape[0]
    m = 2 * n
    slot_e = route[:, 0:2].astype(jnp.int32).reshape(m)
    slot_w = route[:, 2:4].reshape(m)
    iota = jnp.arange(m, dtype=jnp.int32)
    se, order, sw = lax.sort((slot_e, iota, slot_w), num_keys=1)
    e_ids = jnp.arange(N_EXPERTS, dtype=jnp.int32)
    counts = jnp.sum((slot_e[None, :] == e_ids[:, None]).astype(jnp.int32), axis=1)
    start = jnp.cumsum(counts) - counts
    padded = (counts + tm - 1) // tm * tm
    pend = jnp.cumsum(padded)
    pstart = pend - padded
    delta = pstart - start
    dest = iota + jnp.sum(jnp.where(se[:, None] == e_ids[None, :], delta[None, :], 0), axis=1)
    _, pos = lax.sort((order, dest), num_keys=1)
    n_blk = (m + N_EXPERTS * tm) // tm
    blk_row0 = jnp.arange(n_blk, dtype=jnp.int32) * tm
    blk_e = jnp.minimum(jnp.sum((pend[None, :] <= blk_row0[:, None]).astype(jnp.int32), axis=1), N_EXPERTS - 1)
    onehot = blk_e[:, None] == e_ids[None, :]
    pick = lambda tab: jnp.sum(jnp.where(onehot, tab[None, :], 0), axis=1)
    local0 = blk_row0 - pick(pstart)
    n_valid = jnp.clip(pick(counts) - local0, 0, tm)
    within = jnp.arange(tm, dtype=jnp.int32)[None, :]
    src = jnp.clip(pick(start)[:, None] + local0[:, None] + within, 0, m - 1)
    valid = within < n_valid[:, None]
    buf_tok = jnp.where(valid, jnp.take(order, src) // 2, 0)
    buf_w = jnp.where(valid, jnp.take(sw, src), 0.0)
    n_used = (pend[-1] // tm).astype(jnp.int32).reshape(1)
    later = (counts[None, :] > 0) & (e_ids[None, :] > e_ids[:, None])
    nxt_tab = jnp.min(jnp.where(later, e_ids[None, :], N_EXPERTS), axis=1)
    nxt_tab = jnp.where(nxt_tab == N_EXPERTS, e_ids, nxt_tab)
    return blk_e.astype(jnp.int32), pick(nxt_tab).astype(jnp.int32), n_used, buf_tok, buf_w, pos


def _rope_tables(s):
    pos = jnp.arange(s, dtype=F32)[:, None]
    inv128 = jnp.exp(-math.log(ROPE_THETA) * jnp.arange(0, HEAD_DIM, 2, dtype=F32) / HEAD_DIM)
    a = pos * inv128[None, :]
    c128 = jnp.concatenate([jnp.cos(a), jnp.cos(a)], axis=1)
    s128 = jnp.concatenate([-jnp.sin(a), jnp.sin(a)], axis=1)
    inv64 = jnp.exp(-math.log(ROPE_THETA) * jnp.arange(0, DIFF_DIM, 2, dtype=F32) / DIFF_DIM)
    a = pos * inv64[None, :]
    co, si, z = jnp.cos(a), jnp.sin(a), jnp.zeros_like(a)
    c64 = jnp.concatenate([co, co, co, co], axis=1)
    sa64 = jnp.concatenate([-si, z, -si, z], axis=1)
    sb64 = jnp.concatenate([z, si, z, si], axis=1)
    return (c128, s128), (c64, sa64, sb64)


def _orig_offsets():
    offs, o = {}, 0
    for name, w in zip(_ORIG_NAMES, _ORIG_WIDTHS):
        offs[name] = (o, w)
        o += w
    return offs


def _w_in_prep_kernel(w_ref, p_ref, r_ref, d_ref, s_ref):
    offs = _orig_offsets()
    for pieces, dst_ref in ((_PLAIN, p_ref), (_ROPE128, r_ref), (_ROPE64, d_ref)):
        dst = 0
        for name, width in pieces:
            src = offs[name][0]
            dst_ref[:, dst:dst + width] = w_ref[0, :, src:src + width].astype(BF16)
            dst += width
    (f0, fw), (g0, gw) = offs["ff"], offs["ngt"]
    s_ref[...] = jnp.zeros(s_ref.shape, BF16)
    s_ref[:, 0:fw] = w_ref[0, :, f0:f0 + fw].astype(BF16)
    s_ref[:, fw:fw + gw] = w_ref[0, :, g0:g0 + gw].astype(BF16)


def split_w_in(w_in, layer, tr=256):
    _, k, c = w_in.shape
    row = lambda w: pl.BlockSpec((tr, w), lambda i: (i, 0))
    shape = lambda w: jax.ShapeDtypeStruct((k, w), BF16)
    return pl.pallas_call(
        _w_in_prep_kernel,
        grid=(k // tr,),
        in_specs=[pl.BlockSpec((1, tr, c), lambda i: (layer, i, 0))],
        out_specs=[row(C_PLAIN), row(C_ROPE128), row(C_ROPE64), row(LANES)],
        out_shape=[shape(C_PLAIN), shape(C_ROPE128), shape(C_ROPE64), shape(LANES)],
        compiler_params=_cparams(("parallel",), 48),
        name="w_in_prep",
    )(w_in)


def _tiles(s, n):
    return dict(t_attn=min(512, s), tq_nsa=min(256, s), tk_sb=min(256, s), tm_norm=min(512, n), tm_proj=min(512, n),
                tm_out=min(512, n), tm_comb=min(256, n), tm_moe=256)


def kernel(x, norm1_g, w_in, fox_bf, nsa_pe_k, nsa_pe_v, nsa_w_ck, nsa_w_cv, diff_lq1, diff_lk1, diff_lq2, diff_lk2, diff_norm_g, w_branch, w_mgate, w_out, norm2_g, w_rg, b_rg, w_re, b_re, w_eg, w_eu, w_ed, final_g):
    b, s, d = x.shape
    n = b * s
    depth = w_in.shape[0]
    t = _tiles(s, n)
    rope128_tabs, rope64_tabs = _rope_tables(s)
    xf = x.reshape(n, d)
    h = rms_norm_rows(xf, norm1_g[0], BF16, t["tm_norm"])
    out = None
    for l in range(depth):
        lam_init = 0.8 - 0.6 * math.exp(-0.3 * l)
        w_p, w_r, w_d, w_s = split_w_in(w_in, l)
        u_p = in_proj_plain(h, w_p, t["tm_proj"], C_PLAIN // 2).reshape(b, s, C_PLAIN)
        u_r, u_d, small = in_proj_rope(h, w_r, w_d, w_s, rope128_tabs, rope64_tabs, s, t["tm_proj"])
        u_r, u_d, small = u_r.reshape(b, s, C_ROPE128), u_d.reshape(b, s, C_ROPE64), small.reshape(b, s, LANES)
        cum = forget_cumsum(small[:, :, 0:N_HEADS].transpose(0, 2, 1), fox_bf[l])
        o_fox = fox_attention(u_p, cum, t["t_attn"])
        o_sb = sb_attention(u_p, t["t_attn"], t["tk_sb"])
        o_diff = diff_attention(u_d, u_p, diff_lq1[l], diff_lk1[l], diff_lq2[l], diff_lk2[l], diff_norm_g[l],
                                lam_init, t["t_attn"])
        kc, vc = nsa_compress(u_r, u_p, nsa_pe_k[l], nsa_pe_v[l], nsa_w_ck[l], nsa_w_cv[l])
        o_c, mt = nsa_cmp_attention(u_r, kc, vc, t["tq_nsa"])
        nvs0 = COL_P["nvs"] * LANES
        vs_t = u_p[:, :, nvs0:nvs0 + 2 * LANES].transpose(0, 2, 1)
        o_s = nsa_sel_attention(u_r, vs_t, mt, t["t_attn"])
        o_nsa = nsa_win_attention(u_r, u_p, o_c, o_s, small, t["t_attn"])
        branches = [o.reshape(n, BRANCH_WIDTH) for o in (o_fox, o_nsa, o_sb, o_diff)]
        merged = merge_branches(h, branches, w_mgate[l].astype(BF16), w_branch[l].astype(BF16), t["tm_proj"], 512)
        w_router = jnp.concatenate([w_rg[l], w_re[l], jnp.zeros((d, LANES - N_GROUPS - N_EXPERTS), F32)], axis=1).astype(BF16)
        b_router = jnp.concatenate([b_rg[l], b_re[l], jnp.zeros((LANES - N_GROUPS - N_EXPERTS,), F32)]).reshape(1, LANES)
        x1, h2_rows, route = out_proj_router(merged, xf, w_out[l].astype(BF16), norm2_g[l], w_router, b_router, t["tm_out"])
        blk_e, nxt_e, n_used, buf_tok, buf_w, pos = _dispatch_plan(route, t["tm_moe"])
        yb_rows = moe_experts(blk_e, nxt_e, n_used, buf_tok, buf_w, h2_rows, w_eg, w_eu, w_ed, l, t["tm_moe"])
        last = l == depth - 1
        g_next = final_g if last else norm1_g[l + 1]
        xf, hn = moe_combine(pos, x1, yb_rows, g_next, F32 if last else BF16, t["tm_comb"])
        h = hn
        out = hn
    return out.reshape(b, s, d)
```

```python
import functools
import math

import jax
import jax.numpy as jnp
from jax import lax
from jax.experimental import pallas as pl
from jax.experimental.pallas import tpu as pltpu

F32 = jnp.float32
BF16 = jnp.bfloat16
NEG_INF = float("-inf")
LOG2E = 1.4426950408889634

D_MODEL = 2048
HEAD_DIM = 128
DIFF_DIM = 64
N_HEADS = 4
N_KV_NSA = 2
BRANCH_WIDTH = 512
ROPE_THETA = 10000.0
NORM_EPS = 1e-6
CMP_BLOCK = 32
CMP_STRIDE = 16
SEL_BLOCK = 64
SEL_TOP_N = 16
WINDOW = 512
FORCE_SCORE = 1e6
N_GROUPS = 4
EXPERTS_PER_GROUP = 8
N_EXPERTS = 32
EXPERT_FF = 1024

LANES = 128
ROW_TILES = D_MODEL // LANES
MIB = 1024 * 1024

_PLAIN = (("fq", 512), ("fk", 512), ("fv", 512), ("nvc", 256), ("nvs", 256), ("nvw", 256),
          ("sq", 512), ("sk", 512), ("sv", 512), ("dv", 512))
_ROPE128 = (("nq", 512), ("nkc", 256), ("nks", 256), ("nkw", 256))
_ROPE64 = (("dq1", 256), ("dq2", 256), ("dk1", 256), ("dk2", 256))


def _layout(pieces):
    col, off = {}, 0
    for name, w in pieces:
        col[name] = off // LANES
        off += w
    return col, off


COL_P, C_PLAIN = _layout(_PLAIN)
COL_R, C_ROPE128 = _layout(_ROPE128)
COL_D, C_ROPE64 = _layout(_ROPE64)

_ORIG_WIDTHS = (512, 512, 512, 4, 512, 256, 256, 256, 256, 256, 256, 12, 512, 512, 512, 256, 256, 256, 256, 512)
_ORIG_NAMES = ("fq", "fk", "fv", "ff", "nq", "nkc", "nvc", "nks", "nvs", "nkw", "nvw", "ngt",
               "sq", "sk", "sv", "dq1", "dq2", "dk1", "dk2", "dv")


def _cparams(sem, vmem_mib):
    return pltpu.CompilerParams(dimension_semantics=sem, vmem_limit_bytes=vmem_mib * MIB)


def _log_sigmoid(z):
    return jnp.minimum(z, 0.0) - jnp.log1p(jnp.exp(-jnp.abs(z)))


def _dot_nt(a, b):
    return lax.dot_general(a, b, (((1,), (1,)), ((), ())), preferred_element_type=F32)


def _dot(a, b):
    return jnp.dot(a, b, preferred_element_type=F32)


def _lanes(x, width):
    return x if width == LANES else jnp.concatenate([x] * (width // LANES), axis=1)


def _norm_kernel(x_ref, g_ref, o_ref):
    x = x_ref[...]
    y = x * lax.rsqrt(jnp.mean(x * x, axis=-1, keepdims=True) + NORM_EPS)
    o_ref[...] = (y * g_ref[...]).astype(o_ref.dtype)


def rms_norm_rows(x, g, out_dtype, tm):
    n, d = x.shape
    return pl.pallas_call(
        _norm_kernel,
        grid=(n // tm,),
        in_specs=[pl.BlockSpec((tm, d), lambda i: (i, 0)), pl.BlockSpec((1, d), lambda i: (0, 0))],
        out_specs=pl.BlockSpec((tm, d), lambda i: (i, 0)),
        out_shape=jax.ShapeDtypeStruct((n, d), out_dtype),
        compiler_params=_cparams(("parallel",), 40),
        name="rms_norm",
    )(x, g.reshape(1, d))


def _mm_kernel(a_ref, w_ref, o_ref):
    o_ref[...] = _dot(a_ref[...], w_ref[...]).astype(o_ref.dtype)


def in_proj_plain(h, w, tm, tn):
    m, k = h.shape
    c = w.shape[1]
    return pl.pallas_call(
        _mm_kernel,
        grid=(c // tn, m // tm),
        in_specs=[pl.BlockSpec((tm, k), lambda j, i: (i, 0)), pl.BlockSpec((k, tn), lambda j, i: (0, j))],
        out_specs=pl.BlockSpec((tm, tn), lambda j, i: (i, j)),
        out_shape=jax.ShapeDtypeStruct((m, c), BF16),
        compiler_params=_cparams(("parallel", "arbitrary"), 48),
        name="in_proj_plain",
    )(h, w)


def _proj_rope_kernel(h_ref, wr_ref, wd_ref, ws_ref, c128_ref, s128_ref, c64_ref, sa64_ref, sb64_ref,
                      r_ref, d_ref, small_ref):
    h = h_ref[...]
    acc_r = _dot(h, wr_ref[...])
    acc_d = _dot(h, wd_ref[...])
    small_ref[...] = _dot(h, ws_ref[...])
    c, s = c128_ref[...], s128_ref[...]
    for blk in range(C_ROPE128 // LANES):
        x = acc_r[:, blk * LANES:(blk + 1) * LANES]
        r_ref[:, blk * LANES:(blk + 1) * LANES] = (x * c + pltpu.roll(x, 64, axis=1) * s).astype(BF16)
    c, sa, sb = c64_ref[...], sa64_ref[...], sb64_ref[...]
    for blk in range(C_ROPE64 // LANES):
        x = acc_d[:, blk * LANES:(blk + 1) * LANES]
        d_ref[:, blk * LANES:(blk + 1) * LANES] = (
            x * c + pltpu.roll(x, 96, axis=1) * sa + pltpu.roll(x, 32, axis=1) * sb).astype(BF16)


def in_proj_rope(h, w_r, w_d, w_s, rope128_tabs, rope64_tabs, s, tm):
    m, k = h.shape
    per_seq = s // tm
    row = lambda w: pl.BlockSpec((tm, w), lambda i: (i, 0))
    full = lambda w: pl.BlockSpec((k, w), lambda i: (0, 0))
    tab = pl.BlockSpec((tm, LANES), lambda i: (i % per_seq, 0))
    return pl.pallas_call(
        _proj_rope_kernel,
        grid=(m // tm,),
        in_specs=[row(k), full(C_ROPE128), full(C_ROPE64), full(LANES), tab, tab, tab, tab, tab],
        out_specs=[row(C_ROPE128), row(C_ROPE64), row(LANES)],
        out_shape=[jax.ShapeDtypeStruct((m, C_ROPE128), BF16), jax.ShapeDtypeStruct((m, C_ROPE64), BF16),
                   jax.ShapeDtypeStruct((m, LANES), F32)],
        compiler_params=_cparams(("parallel",), 48),
        name="in_proj_rope",
    )(h, w_r, w_d, w_s, *rope128_tabs, *rope64_tabs)


def _cum_kernel(f_ref, b_ref, o_ref):
    z = f_ref[0] + b_ref[...]
    ls = _log_sigmoid(z)
    s = ls.shape[1]
    lane = lax.broadcasted_iota(jnp.int32, ls.shape, 1)
    sh = 1
    while sh < s:
        ls = ls + jnp.where(lane >= sh, pltpu.roll(ls, sh, axis=1), 0.0)
        sh *= 2
    o_ref[0] = ls * LOG2E


def forget_cumsum(f_t, bias):
    b, h, s = f_t.shape
    return pl.pallas_call(
        _cum_kernel,
        grid=(b,),
        in_specs=[pl.BlockSpec((1, h, s), lambda i: (i, 0, 0)), pl.BlockSpec((h, 1), lambda i: (0, 0))],
        out_specs=pl.BlockSpec((1, h, s), lambda i: (i, 0, 0)),
        out_shape=jax.ShapeDtypeStruct((b, h, s), F32),
        name="fox_cumsum",
    )(f_t, bias.reshape(h, 1))


def _flash_init(m_ref, l_ref, acc_ref):
    m_ref[...] = jnp.full(m_ref.shape, NEG_INF, F32)
    l_ref[...] = jnp.zeros(l_ref.shape, F32)
    acc_ref[...] = jnp.zeros(acc_ref.shape, F32)


def _flash_step(a, v, m_ref, l_ref, acc_ref):
    m_prev = m_ref[...]
    m_new = jnp.maximum(m_prev, jnp.max(a, axis=1, keepdims=True))
    alpha = jnp.exp2(m_prev - m_new)
    p = jnp.exp2(a - _lanes(m_new, a.shape[1]))
    l_ref[...] = alpha * l_ref[...] + jnp.sum(p, axis=1, keepdims=True)
    acc_ref[...] = alpha * acc_ref[...] + _dot(p.astype(BF16), v)
    m_ref[...] = m_new


def _causal_diag(t):
    return lax.broadcasted_iota(jnp.int32, (t, t), 1) <= lax.broadcasted_iota(jnp.int32, (t, t), 0)


def _fox_kernel(q_ref, k_ref, v_ref, cum_ref, o_ref, m_ref, l_ref, acc_ref, *, t):
    h = pl.program_id(1)
    qi = pl.program_id(2)
    c = HEAD_DIM ** -0.5 * LOG2E
    q = q_ref[0]

    def logits(j):
        return _dot_nt(q, k_ref[0, j * t:(j + 1) * t, :]) * c - cum_ref[0, pl.ds(h, 1), j * t:(j + 1) * t]

    def tile_body(n):
        _flash_init(m_ref, l_ref, acc_ref)
        a = logits(0)
        for j in range(n + 1):
            a_next = logits(j + 1) if j < n else None
            if j == n:
                a = jnp.where(_causal_diag(t), a, NEG_INF)
            _flash_step(a, v_ref[0, j * t:(j + 1) * t, :], m_ref, l_ref, acc_ref)
            a = a_next
        o_ref[0] = (acc_ref[...] / l_ref[...]).astype(o_ref.dtype)

    for n in range(k_ref.shape[1] // t):
        pl.when(qi == n)(functools.partial(tile_body, n))


def fox_attention(u, cum, t):
    b, s, _ = u.shape
    cq, ck, cv = COL_P["fq"], COL_P["fk"], COL_P["fv"]
    stat = pltpu.VMEM((t, LANES), F32)
    return pl.pallas_call(
        functools.partial(_fox_kernel, t=t),
        grid=(b, N_HEADS, s // t),
        in_specs=[
            pl.BlockSpec((1, t, LANES), lambda b_, h, i: (b_, i, cq + h)),
            pl.BlockSpec((1, s, LANES), lambda b_, h, i: (b_, 0, ck + h)),
            pl.BlockSpec((1, s, LANES), lambda b_, h, i: (b_, 0, cv + h)),
            pl.BlockSpec((1, N_HEADS, s), lambda b_, h, i: (b_, 0, 0)),
        ],
        out_specs=pl.BlockSpec((1, t, LANES), lambda b_, h, i: (b_, i, h)),
        out_shape=jax.ShapeDtypeStruct((b, s, BRANCH_WIDTH), BF16),
        scratch_shapes=[stat, stat, stat],
        compiler_params=_cparams(("parallel", "parallel", "arbitrary"), 40),
        name="fox_attention",
    )(u, u, u, cum)


def _sb_kernel(q_ref, k_ref, v_ref, o_ref, carry_ref, acc_ref, *, tq, tk):
    qi = pl.program_id(2)
    c = HEAD_DIM ** -0.5 * LOG2E
    q = q_ref[0]
    upper = (lax.broadcasted_iota(jnp.int32, (tk, tk), 0) > lax.broadcasted_iota(jnp.int32, (tk, tk), 1)).astype(BF16)
    per_tile = tq // tk

    def logits(c0):
        return _dot_nt(q, k_ref[0, c0:c0 + tk, :]) * c

    def chunk(z2, c0, mask):
        ls = jnp.minimum(z2, 0.0) - jnp.log2(1.0 + jnp.exp2(-jnp.abs(z2)))
        l_neg = ls - z2
        if mask is not None:
            l_neg = jnp.where(mask, l_neg, 0.0)
        hi = l_neg.astype(BF16)
        lo = (l_neg - hi.astype(F32)).astype(BF16)
        after = _lanes(carry_ref[...], tk) + _dot(hi, upper) + _dot(lo, upper)
        a = jnp.exp2(ls + after)
        if mask is not None:
            a = jnp.where(mask, a, 0.0)
        acc_ref[...] += _dot(a.astype(BF16), v_ref[0, c0:c0 + tk, :])
        carry_ref[...] += jnp.sum(l_neg, axis=1, keepdims=True)

    def tile_body(n):
        carry_ref[...] = jnp.zeros(carry_ref.shape, F32)
        acc_ref[...] = jnp.zeros(acc_ref.shape, F32)
        row = lax.broadcasted_iota(jnp.int32, (tq, tk), 0)
        col = lax.broadcasted_iota(jnp.int32, (tq, tk), 1)
        steps = [(n * tq + d * tk, d * tk + col < row) for d in reversed(range(per_tile))]
        steps += [(j * tk, None) for j in reversed(range(n * per_tile))]
        z2 = logits(steps[0][0])
        for i, (c0, mask) in enumerate(steps):
            z2_next = logits(steps[i + 1][0]) if i + 1 < len(steps) else None
            chunk(z2, c0, mask)
            z2 = z2_next
        o_ref[0] = acc_ref[...].astype(o_ref.dtype)

    for n in range(k_ref.shape[1] // tq):
        pl.when(qi == n)(functools.partial(tile_body, n))


def sb_attention(u, tq, tk):
    b, s, _ = u.shape
    cq, ck, cv = COL_P["sq"], COL_P["sk"], COL_P["sv"]
    return pl.pallas_call(
        functools.partial(_sb_kernel, tq=tq, tk=tk),
        grid=(b, N_HEADS, s // tq),
        in_specs=[
            pl.BlockSpec((1, tq, LANES), lambda b_, h, i: (b_, i, cq + h)),
            pl.BlockSpec((1, s, LANES), lambda b_, h, i: (b_, 0, ck + h)),
            pl.BlockSpec((1, s, LANES), lambda b_, h, i: (b_, 0, cv + h)),
        ],
        out_specs=pl.BlockSpec((1, tq, LANES), lambda b_, h, i: (b_, i, h)),
        out_shape=jax.ShapeDtypeStruct((b, s, BRANCH_WIDTH), BF16),
        scratch_shapes=[pltpu.VMEM((tq, LANES), F32), pltpu.VMEM((tq, LANES), F32)],
        compiler_params=_cparams(("parallel", "parallel", "arbitrary"), 40),
        name="sb_attention",
    )(u, u, u)


def _diff_kernel(q1_ref, q2_ref, k1_ref, k2_ref, v_ref, lq1_ref, lk1_ref, lq2_ref, lk2_ref, g_ref, o_ref,
                 m1_ref, l1_ref, a1_ref, m2_ref, l2_ref, a2_ref, *, t, lam_init):
    h = pl.program_id(1)
    qi = pl.program_id(2)
    c = DIFF_DIM ** -0.5 * LOG2E
    mine = (lax.broadcasted_iota(jnp.int32, (t, LANES), 1) // DIFF_DIM) == (h % 2)
    q1 = jnp.where(mine, q1_ref[0], 0.0).astype(BF16)
    q2 = jnp.where(mine, q2_ref[0], 0.0).astype(BF16)
    streams = ((q1, k1_ref, m1_ref, l1_ref, a1_ref), (q2, k2_ref, m2_ref, l2_ref, a2_ref))

    def logits(w, j):
        return _dot_nt(streams[w][0], streams[w][1][0, j * t:(j + 1) * t, :]) * c

    def tile_body(n):
        _flash_init(m1_ref, l1_ref, a1_ref)
        _flash_init(m2_ref, l2_ref, a2_ref)
        steps = [(w, j) for j in range(n + 1) for w in range(2)]
        a = logits(*steps[0])
        for i, (w, j) in enumerate(steps):
            a_next = logits(*steps[i + 1]) if i + 1 < len(steps) else None
            if j == n:
                a = jnp.where(_causal_diag(t), a, NEG_INF)
            _flash_step(a, v_ref[0, j * t:(j + 1) * t, :], *streams[w][2:])
            a = a_next
        lam = (jnp.exp(jnp.sum(lq1_ref[...] * lk1_ref[...], axis=1, keepdims=True))
               - jnp.exp(jnp.sum(lq2_ref[...] * lk2_ref[...], axis=1, keepdims=True)) + lam_init)
        o = a1_ref[...] / l1_ref[...] - lam * (a2_ref[...] / l2_ref[...])
        y = o * lax.rsqrt(jnp.mean(o * o, axis=-1, keepdims=True) + NORM_EPS)
        o_ref[0] = ((y * g_ref[...]) * (1.0 - lam_init)).astype(o_ref.dtype)

    for n in range(v_ref.shape[1] // t):
        pl.when(qi == n)(functools.partial(tile_body, n))


def diff_attention(u_d, u_p, lq1, lk1, lq2, lk2, g, lam_init, t):
    b, s, _ = u_d.shape
    cq1, cq2, ck1, ck2, cv = COL_D["dq1"], COL_D["dq2"], COL_D["dk1"], COL_D["dk2"], COL_P["dv"]
    vec64 = pl.BlockSpec((1, DIFF_DIM), lambda b_, h, i: (0, 0))
    stat = pltpu.VMEM((t, LANES), F32)
    return pl.pallas_call(
        functools.partial(_diff_kernel, t=t, lam_init=lam_init),
        grid=(b, N_HEADS, s // t),
        in_specs=[
            pl.BlockSpec((1, t, LANES), lambda b_, h, i: (b_, i, cq1 + h // 2)),
            pl.BlockSpec((1, t, LANES), lambda b_, h, i: (b_, i, cq2 + h // 2)),
            pl.BlockSpec((1, s, LANES), lambda b_, h, i: (b_, 0, ck1 + h // 2)),
            pl.BlockSpec((1, s, LANES), lambda b_, h, i: (b_, 0, ck2 + h // 2)),
            pl.BlockSpec((1, s, LANES), lambda b_, h, i: (b_, 0, cv + h)),
            vec64, vec64, vec64, vec64,
            pl.BlockSpec((1, HEAD_DIM), lambda b_, h, i: (0, 0)),
        ],
        out_specs=pl.BlockSpec((1, t, LANES), lambda b_, h, i: (b_, i, h)),
        out_shape=jax.ShapeDtypeStruct((b, s, BRANCH_WIDTH), BF16),
        scratch_shapes=[stat] * 6,
        compiler_params=_cparams(("parallel", "parallel", "arbitrary"), 40),
        name="diff_attention",
    )(u_d, u_d, u_d, u_d, u_p,
      lq1.reshape(1, DIFF_DIM), lk1.reshape(1, DIFF_DIM), lq2.reshape(1, DIFF_DIM), lk2.reshape(1, DIFF_DIM),
      g.reshape(1, HEAD_DIM))


def _compress_kernel(kt_ref, vt_ref, pek_ref, pev_ref, wk_ref, wv_ref, kc_ref, vc_ref,
                     xs_ref, xa_ref, xb_ref, *, n_blk):
    half = CMP_BLOCK // 2

    def compress(x, pe_ref, w_ref):
        xs_ref[...] = x.astype(F32)
        for r in range(half):
            piece = xs_ref[pl.ds(r, n_blk, stride=half), :]
            xa_ref[:, r * LANES:(r + 1) * LANES] = (piece + pe_ref[pl.ds(r, 1), :]).astype(BF16)
            xb_ref[:, r * LANES:(r + 1) * LANES] = (piece + pe_ref[pl.ds(half + r, 1), :]).astype(BF16)
        first = _dot(xa_ref[...], w_ref[0])
        second = _dot(xb_ref[...], w_ref[1])
        return first + pltpu.roll(second, n_blk - 1, axis=0)

    for g in range(N_KV_NSA):
        sl = slice(g * LANES, (g + 1) * LANES)
        kc_ref[0, g] = compress(kt_ref[0][:, sl], pek_ref, wk_ref).astype(BF16)
        vc_ref[0, g] = compress(vt_ref[0][:, sl], pev_ref, wv_ref).astype(BF16)


def nsa_compress(u_r, u_p, pe_k, pe_v, w_ck, w_cv):
    b, s, _ = u_r.shape
    n_blk = s // CMP_STRIDE
    half = CMP_BLOCK // 2
    wk = w_ck.reshape(2, half * HEAD_DIM, HEAD_DIM).astype(BF16)
    wv = w_cv.reshape(2, half * HEAD_DIM, HEAD_DIM).astype(BF16)
    pe = pl.BlockSpec((CMP_BLOCK, HEAD_DIM), lambda i: (0, 0))
    wspec = pl.BlockSpec((2, half * HEAD_DIM, HEAD_DIM), lambda i: (0, 0, 0))
    out = pl.BlockSpec((1, N_KV_NSA, n_blk, HEAD_DIM), lambda i: (i, 0, 0, 0))
    return pl.pallas_call(
        functools.partial(_compress_kernel, n_blk=n_blk),
        grid=(b,),
        in_specs=[pl.BlockSpec((1, s, 2 * LANES), lambda i: (i, 0, COL_R["nkc"] // 2)),
                  pl.BlockSpec((1, s, 2 * LANES), lambda i: (i, 0, COL_P["nvc"] // 2)),
                  pe, pe, wspec, wspec],
        out_specs=[out, out],
        out_shape=[jax.ShapeDtypeStruct((b, N_KV_NSA, n_blk, HEAD_DIM), BF16)] * 2,
        scratch_shapes=[pltpu.VMEM((s, LANES), F32), pltpu.VMEM((n_blk, half * LANES), BF16),
                        pltpu.VMEM((n_blk, half * LANES), BF16)],
        compiler_params=_cparams(("parallel",), 40),
        name="nsa_compress",
    )(u_r, u_p, pe_k, pe_v, wk, wv)


def _cmp_attn_kernel(q_ref, kc_ref, vc_ref, ov_ref, oc_ref, mt_ref, *, tq, n_blk, n_sel):
    qi = pl.program_id(1)
    q0 = qi * tq
    scale = HEAD_DIM ** -0.5
    hpg = N_HEADS // N_KV_NSA
    t_row = q0 + lax.broadcasted_iota(jnp.int32, (tq, n_blk), 0)
    n_col = lax.broadcasted_iota(jnp.int32, (tq, n_blk), 1)
    valid = n_col * CMP_STRIDE + (CMP_BLOCK - 1) <= t_row
    t_lane = q0 + lax.broadcasted_iota(jnp.int32, (n_blk, tq), 1)
    n_sub = lax.broadcasted_iota(jnp.int32, (n_blk, tq), 0)
    valid_t = n_sub * CMP_STRIDE + (CMP_BLOCK - 1) <= t_lane
    j_idx = lax.broadcasted_iota(jnp.int32, (n_sel, tq), 0)
    cur = (q0 + lax.broadcasted_iota(jnp.int32, (n_sel, tq), 1)) // SEL_BLOCK
    for g in range(N_KV_NSA):
        kc = kc_ref[0, g]
        vc = vc_ref[0, g]
        p_sum_t = jnp.zeros((n_blk, tq), F32)
        for hh in range(hpg):
            sl = slice((g * hpg + hh) * LANES, (g * hpg + hh + 1) * LANES)
            q = q_ref[0][:, sl]
            s = jnp.where(valid, _dot_nt(q, kc) * scale, NEG_INF)
            m = jnp.max(s, axis=1, keepdims=True)
            m = jnp.where(m == NEG_INF, 0.0, m)
            e = jnp.exp(s - m)
            p = e / jnp.maximum(jnp.sum(e, axis=1, keepdims=True), 1e-30)
            oc_ref[0, :, sl] = _dot(p.astype(BF16), vc)
            st = jnp.where(valid_t, _dot_nt(kc, q) * scale, NEG_INF)
            mt = jnp.max(st, axis=0, keepdims=True)
            mt = jnp.where(mt == NEG_INF, 0.0, mt)
            et = jnp.exp(st - mt)
            p_sum_t = p_sum_t + et / jnp.maximum(jnp.sum(et, axis=0, keepdims=True), 1e-30)
        hi = p_sum_t.astype(BF16)
        lo = (p_sum_t - hi.astype(F32)).astype(BF16)
        imp = _dot(ov_ref[...], hi) + _dot(ov_ref[...], lo)
        imp = jnp.where((j_idx == 0) | (j_idx == cur) | (j_idx == cur - 1), FORCE_SCORE, imp)
        imp = jnp.where(j_idx <= cur, imp, NEG_INF)
        rank = jnp.zeros((n_sel, tq), F32)
        for jp in range(n_sel):
            other = imp[jp:jp + 1, :]
            beats = (other > imp) | ((other == imp) & (j_idx > jp))
            rank = rank + beats.astype(F32)
        mt_ref[0, g] = (rank < float(min(SEL_TOP_N, n_sel))).astype(F32)


def nsa_cmp_attention(u_r, kc, vc, tq):
    b, s, _ = u_r.shape
    n_blk = s // CMP_STRIDE
    n_sel = s // SEL_BLOCK
    ci = jnp.arange(n_blk)[None, :] * CMP_STRIDE
    sj = jnp.arange(n_sel)[:, None] * SEL_BLOCK
    overlap_t = ((ci < sj + SEL_BLOCK) & (ci + CMP_BLOCK > sj)).astype(BF16)
    cblk = pl.BlockSpec((1, N_KV_NSA, n_blk, HEAD_DIM), lambda b_, i: (b_, 0, 0, 0))
    return pl.pallas_call(
        functools.partial(_cmp_attn_kernel, tq=tq, n_blk=n_blk, n_sel=n_sel),
        grid=(b, s // tq),
        in_specs=[pl.BlockSpec((1, tq, BRANCH_WIDTH), lambda b_, i: (b_, i, COL_R["nq"] // N_HEADS)), cblk, cblk,
                  pl.BlockSpec((n_sel, n_blk), lambda b_, i: (0, 0))],
        out_specs=[pl.BlockSpec((1, tq, BRANCH_WIDTH), lambda b_, i: (b_, i, 0)),
                   pl.BlockSpec((1, N_KV_NSA, n_sel, tq), lambda b_, i: (b_, 0, 0, i))],
        out_shape=[jax.ShapeDtypeStruct((b, s, BRANCH_WIDTH), F32),
                   jax.ShapeDtypeStruct((b, N_KV_NSA, n_sel, s), F32)],
        compiler_params=_cparams(("parallel", "parallel"), 40),
        name="nsa_cmp_attention",
    )(u_r, kc, vc, overlap_t)


def _sel_kernel(q_ref, k_ref, vt_ref, mt_ref, o_ref, m_ref, l_ref, acc_ref, *, t):
    qi = pl.program_id(2)
    c = HEAD_DIM ** -0.5 * LOG2E
    hpg = N_HEADS // N_KV_NSA
    per_tile = t // SEL_BLOCK
    causal_t = lax.broadcasted_iota(jnp.int32, (t, t), 0) <= lax.broadcasted_iota(jnp.int32, (t, t), 1)
    qs = [q_ref[0][:, hh * LANES:(hh + 1) * LANES] for hh in range(hpg)]

    def scores(hh, j):
        return _dot_nt(k_ref[0, j * t:(j + 1) * t, :], qs[hh]) * c

    def tile_body(n):
        _flash_init(m_ref, l_ref, acc_ref)
        steps = [(hh, j) for j in range(n + 1) for hh in range(hpg)]
        at = scores(*steps[0])
        for i, (hh, j) in enumerate(steps):
            at_next = scores(*steps[i + 1]) if i + 1 < len(steps) else None
            rows = [jnp.broadcast_to(mt_ref[0, 0, j * per_tile + r:j * per_tile + r + 1, :], (SEL_BLOCK, t))
                    for r in range(per_tile)]
            mask = jnp.concatenate(rows, axis=0) > 0.5
            if j == n:
                mask = mask & causal_t
            at = jnp.where(mask, at, NEG_INF)
            m_prev = m_ref[hh]
            m_new = jnp.maximum(m_prev, jnp.max(at, axis=0, keepdims=True))
            alpha = jnp.exp2(m_prev - m_new)
            p = jnp.exp2(at - m_new)
            l_ref[hh] = alpha * l_ref[hh] + jnp.sum(p, axis=0, keepdims=True)
            acc_ref[hh] = alpha * acc_ref[hh] + _dot(vt_ref[0, :, j * t:(j + 1) * t], p.astype(BF16))
            m_ref[hh] = m_new
            at = at_next
        for hh in range(hpg):
            o_ref[0, :, hh * LANES:(hh + 1) * LANES] = (acc_ref[hh] / l_ref[hh]).T

    for n in range(k_ref.shape[1] // t):
        pl.when(qi == n)(functools.partial(tile_body, n))


def nsa_sel_attention(u_r, vs_t, mt, t):
    b, s, _ = u_r.shape
    n_sel = s // SEL_BLOCK
    hpg = N_HEADS // N_KV_NSA
    cq, ck = COL_R["nq"] // hpg, COL_R["nks"]
    return pl.pallas_call(
        functools.partial(_sel_kernel, t=t),
        grid=(b, N_KV_NSA, s // t),
        in_specs=[pl.BlockSpec((1, t, hpg * LANES), lambda b_, g, i: (b_, i, cq + g)),
                  pl.BlockSpec((1, s, LANES), lambda b_, g, i: (b_, 0, ck + g)),
                  pl.BlockSpec((1, LANES, s), lambda b_, g, i: (b_, g, 0)),
                  pl.BlockSpec((1, 1, n_sel, t), lambda b_, g, i: (b_, g, 0, i))],
        out_specs=pl.BlockSpec((1, t, hpg * LANES), lambda b_, g, i: (b_, i, g)),
        out_shape=jax.ShapeDtypeStruct((b, s, BRANCH_WIDTH), F32),
        scratch_shapes=[pltpu.VMEM((hpg, 1, t), F32), pltpu.VMEM((hpg, 1, t), F32),
                        pltpu.VMEM((hpg, HEAD_DIM, t), F32)],
        compiler_params=_cparams(("parallel", "parallel", "arbitrary"), 40),
        name="nsa_sel_attention",
    )(u_r, u_r, vs_t, mt)


def _win_kernel(q_ref, k_ref, kp_ref, v_ref, vp_ref, oc_ref, os_ref, small_ref, o_ref, m_ref, l_ref, acc_ref, *, t):
    qi = pl.program_id(1)
    c = HEAD_DIM ** -0.5 * LOG2E
    hpg = N_HEADS // N_KV_NSA
    row = lax.broadcasted_iota(jnp.int32, (t, t), 0)
    col = lax.broadcasted_iota(jnp.int32, (t, t), 1)
    gates = jax.nn.sigmoid(small_ref[0])

    def tile_body(with_prev):
        heads = [(g, hh) for g in range(N_KV_NSA) for hh in range(hpg)]

        def scores(g, hh):
            gl = slice(g * LANES, (g + 1) * LANES)
            q = q_ref[0][:, (g * hpg + hh) * LANES:(g * hpg + hh + 1) * LANES]
            a = jnp.where(col <= row, _dot_nt(q, k_ref[0][:, gl]) * c, NEG_INF)
            ap = jnp.where(col > row, _dot_nt(q, kp_ref[0][:, gl]) * c, NEG_INF) if with_prev else None
            return a, ap

        nxt = scores(*heads[0])
        for i, (g, hh) in enumerate(heads):
            a, ap = nxt
            nxt = scores(*heads[i + 1]) if i + 1 < len(heads) else None
            gl = slice(g * LANES, (g + 1) * LANES)
            head = g * hpg + hh
            sl = slice(head * LANES, (head + 1) * LANES)
            _flash_init(m_ref, l_ref, acc_ref)
            _flash_step(a, v_ref[0][:, gl], m_ref, l_ref, acc_ref)
            if with_prev:
                _flash_step(ap, vp_ref[0][:, gl], m_ref, l_ref, acc_ref)
            o_w = acc_ref[...] / l_ref[...]
            cg = 4 + 3 * head
            o = (gates[:, cg:cg + 1] * oc_ref[0][:, sl] + gates[:, cg + 1:cg + 2] * os_ref[0][:, sl]
                 + gates[:, cg + 2:cg + 3] * o_w)
            o_ref[0, :, sl] = o.astype(o_ref.dtype)

    pl.when(qi == 0)(functools.partial(tile_body, False))
    pl.when(qi > 0)(functools.partial(tile_body, True))


def nsa_win_attention(u_r, u_p, o_c, o_s, small, t):
    b, s, _ = u_r.shape
    assert t == WINDOW
    wide = pl.BlockSpec((1, t, BRANCH_WIDTH), lambda b_, i: (b_, i, 0))
    ck, cv = COL_R["nkw"] // 2, COL_P["nvw"] // 2
    stat = pltpu.VMEM((t, LANES), F32)
    return pl.pallas_call(
        functools.partial(_win_kernel, t=t),
        grid=(b, s // t),
        in_specs=[pl.BlockSpec((1, t, BRANCH_WIDTH), lambda b_, i: (b_, i, COL_R["nq"] // N_HEADS)),
                  pl.BlockSpec((1, t, 2 * LANES), lambda b_, i: (b_, i, ck)),
                  pl.BlockSpec((1, t, 2 * LANES), lambda b_, i: (b_, jnp.maximum(i - 1, 0), ck)),
                  pl.BlockSpec((1, t, 2 * LANES), lambda b_, i: (b_, i, cv)),
                  pl.BlockSpec((1, t, 2 * LANES), lambda b_, i: (b_, jnp.maximum(i - 1, 0), cv)),
                  wide, wide,
                  pl.BlockSpec((1, t, LANES), lambda b_, i: (b_, i, 0))],
        out_specs=wide,
        out_shape=jax.ShapeDtypeStruct((b, s, BRANCH_WIDTH), BF16),
        scratch_shapes=[stat, stat, stat],
        compiler_params=_cparams(("parallel", "arbitrary"), 48),
        name="nsa_win_attention",
    )(u_r, u_r, u_r, u_p, u_p, o_c, o_s, small)


def _merge_kernel(h_ref, o0_ref, o1_ref, o2_ref, o3_ref, wg_ref, wb_ref, out_ref):
    h = h_ref[...]
    acc = None
    for n, o_ref in enumerate((o0_ref, o1_ref, o2_ref, o3_ref)):
        term = jax.nn.sigmoid(_dot(h, wg_ref[n])) * _dot(o_ref[...], wb_ref[n])
        acc = term if acc is None else acc + term
    out_ref[...] = acc.astype(out_ref.dtype)


def merge_branches(h, branches, wg, wb, tm, tn):
    n, d = h.shape
    bspec = pl.BlockSpec((tm, BRANCH_WIDTH), lambda j, i: (i, 0))
    return pl.pallas_call(
        _merge_kernel,
        grid=(d // tn, n // tm),
        in_specs=[pl.BlockSpec((tm, d), lambda j, i: (i, 0)), bspec, bspec, bspec, bspec,
                  pl.BlockSpec((4, d, tn), lambda j, i: (0, 0, j)),
                  pl.BlockSpec((4, BRANCH_WIDTH, tn), lambda j, i: (0, 0, j))],
        out_specs=pl.BlockSpec((tm, tn), lambda j, i: (i, j)),
        out_shape=jax.ShapeDtypeStruct((n, d), BF16),
        compiler_params=_cparams(("parallel", "arbitrary"), 56),
        name="merge_branches",
    )(h, *branches, wg, wb)


def _out_kernel(mg_ref, x_ref, w_ref, g_ref, wr_ref, br_ref, x1_ref, h2_ref, route_ref, *, tm):
    n_part = 2 if tm % 32 == 0 else 1
    th = tm // n_part
    parts = [x_ref[p * th:(p + 1) * th, :] + _dot(mg_ref[p * th:(p + 1) * th, :], w_ref[...]) for p in range(n_part)]
    for p in range(n_part):
        _out_epilogue(parts[p], p * th, th, g_ref, wr_ref, br_ref, x1_ref, h2_ref, route_ref)


def _out_epilogue(x1, r0, th, g_ref, wr_ref, br_ref, x1_ref, h2_ref, route_ref):
    x1_ref[r0:r0 + th, :] = x1
    h2 = (x1 * lax.rsqrt(jnp.mean(x1 * x1, axis=-1, keepdims=True) + NORM_EPS)) * g_ref[...]
    for s in range(ROW_TILES):
        h2_ref[pl.ds(r0 * ROW_TILES + s, th, stride=ROW_TILES), :] = h2[:, s * LANES:(s + 1) * LANES]
    logits = _dot(h2.astype(BF16), wr_ref[...]) + br_ref[...]
    lane = lax.broadcasted_iota(jnp.int32, logits.shape, 1)
    lane_f = lane.astype(F32)
    big = float(LANES)
    is_g = lane < N_GROUPS
    lg = jnp.where(is_g, logits, NEG_INF)
    mx = jnp.max(lg, axis=1, keepdims=True)
    gi = jnp.min(jnp.where(lg == mx, lane_f, big), axis=1, keepdims=True)
    pg = 1.0 / jnp.sum(jnp.where(is_g, jnp.exp(lg - mx), 0.0), axis=1, keepdims=True)
    e_idx = lane - N_GROUPS
    in_grp = (lane >= N_GROUPS) & (lane < N_GROUPS + N_EXPERTS) & ((e_idx // EXPERTS_PER_GROUP).astype(F32) == gi)
    le = jnp.where(in_grp, logits, NEG_INF)
    v1 = jnp.max(le, axis=1, keepdims=True)
    i1 = jnp.min(jnp.where(le == v1, lane_f, big), axis=1, keepdims=True)
    le2 = jnp.where(lane_f == i1, NEG_INF, le)
    v2 = jnp.max(le2, axis=1, keepdims=True)
    i2 = jnp.min(jnp.where(le2 == v2, lane_f, big), axis=1, keepdims=True)
    e2 = jnp.exp(v2 - v1)
    w1 = pg / (1.0 + e2)
    w2 = pg * e2 / (1.0 + e2)
    route = jnp.where(lane == 0, i1 - N_GROUPS, 0.0)
    route = jnp.where(lane == 1, i2 - N_GROUPS, route)
    route = jnp.where(lane == 2, w1, route)
    route = jnp.where(lane == 3, w2, route)
    route_ref[r0:r0 + th, :] = route


def out_proj_router(merged, x, w_out, g2, w_router, b_router, tm):
    n, d = x.shape
    row = pl.BlockSpec((tm, d), lambda i: (i, 0))
    return pl.pallas_call(
        functools.partial(_out_kernel, tm=tm),
        grid=(n // tm,),
        in_specs=[row, row, pl.BlockSpec((d, d), lambda i: (0, 0)), pl.BlockSpec((1, d), lambda i: (0, 0)),
                  pl.BlockSpec((d, LANES), lambda i: (0, 0)), pl.BlockSpec((1, LANES), lambda i: (0, 0))],
        out_specs=[row, pl.BlockSpec((tm * ROW_TILES, LANES), lambda i: (i, 0)),
                   pl.BlockSpec((tm, LANES), lambda i: (i, 0))],
        out_shape=[jax.ShapeDtypeStruct((n, d), F32), jax.ShapeDtypeStruct((n * ROW_TILES, LANES), F32),
                   jax.ShapeDtypeStruct((n, LANES), F32)],
        compiler_params=_cparams(("parallel",), 56),
        name="out_proj_router",
    )(merged, x, w_out, g2.reshape(1, d), w_router, b_router)


ROW_PITCH = ROW_TILES + 8


def _start_rows(src_hbm, idx_ref, n_rows, buf_ref, sem, place=lambda r: r, priorities=2):
    for r in range(n_rows):
        src = pl.multiple_of(idx_ref[0, 0, r] * ROW_TILES, ROW_TILES)
        pltpu.make_async_copy(src_hbm.at[pl.ds(src, ROW_TILES), :],
                              buf_ref.at[pl.ds(place(r) * ROW_PITCH, ROW_TILES), :], sem).start(priority=r % priorities)


def _wait_rows(src_hbm, n_rows, buf_ref, sem):
    n_sub = n_rows * ROW_TILES
    pltpu.make_async_copy(src_hbm.at[pl.ds(0, n_sub), :], buf_ref.at[pl.ds(0, n_sub), :], sem).wait()


def _ffn_kernel(blk_e_ref, nxt_e_ref, n_used_ref, tok_ref, tok_next_ref, sw_ref, h2_hbm, wg_hbm, wu_hbm, wd_hbm, y_ref,
                xa_ref, xb_ref, sg_ref, su_ref, sd_ref, wg_ref, wu_ref, wd_ref, sem, wsem, *, tm, layer):
    blk = pl.program_id(0)
    n_used = n_used_ref[0]
    even = blk % 2 == 0
    cur_e = blk_e_ref[blk]

    def weight_copies(e):
        pairs = ((wg_hbm, sg_ref), (wu_hbm, su_ref), (wd_hbm, sd_ref))
        return [pltpu.make_async_copy(src.at[layer, e], dst, wsem.at[i]) for i, (src, dst) in enumerate(pairs)]

    @pl.when(blk == 0)
    def _():
        _start_rows(h2_hbm, tok_ref, tm, xa_ref, sem.at[0], priorities=1)
        for cp in weight_copies(cur_e):
            cp.start(priority=1)

    @pl.when((blk < n_used) & ((blk == 0) | (cur_e != blk_e_ref[jnp.maximum(blk - 1, 0)])))
    def _():
        for cp in weight_copies(cur_e):
            cp.wait()
        rows = 128
        for src, dst in ((sg_ref, wg_ref), (su_ref, wu_ref), (sd_ref, wd_ref)):
            def cast_rows(i, carry, src=src, dst=dst):
                r0 = pl.multiple_of(i * rows, rows)
                dst[0, pl.ds(r0, rows), :] = src[pl.ds(r0, rows), :].astype(BF16)
                return carry

            lax.fori_loop(0, src.shape[0] // rows, cast_rows, 0)
        nxt_e = nxt_e_ref[blk]

        @pl.when(nxt_e != cur_e)
        def _():
            for cp in weight_copies(nxt_e):
                cp.start(priority=1)

    def work(cur_ref, cur_sem, nxt_ref, nxt_sem):
        _wait_rows(h2_hbm, tm, cur_ref, cur_sem)
        _start_rows(h2_hbm, tok_next_ref, tm, nxt_ref, nxt_sem, priorities=1)
        x = jnp.concatenate([cur_ref[pl.ds(s, tm, stride=ROW_PITCH), :] for s in range(ROW_TILES)],
                            axis=1).astype(BF16)
        gate = _dot(x, wg_ref[0])
        up = _dot(x, wu_ref[0])
        hid = (gate * jax.nn.sigmoid(gate) * up).astype(BF16)
        y = _dot(hid, wd_ref[0]) * sw_ref[...]
        for s in range(ROW_TILES):
            y_ref[pl.ds(s, tm, stride=ROW_TILES), :] = y[:, s * LANES:(s + 1) * LANES]

    @pl.when((blk < n_used) & even)
    def _():
        work(xa_ref, sem.at[0], xb_ref, sem.at[1])

    @pl.when((blk < n_used) & jnp.logical_not(even))
    def _():
        work(xb_ref, sem.at[1], xa_ref, sem.at[0])

    @pl.when(blk >= n_used)
    def _():
        y_ref[...] = jnp.zeros(y_ref.shape, F32)

    @pl.when((blk == n_used) & even)
    def _():
        _wait_rows(h2_hbm, tm, xa_ref, sem.at[0])

    @pl.when((blk == n_used) & jnp.logical_not(even))
    def _():
        _wait_rows(h2_hbm, tm, xb_ref, sem.at[1])


def moe_experts(blk_e, nxt_e, n_used, buf_tok, buf_w, h2_rows, w_g, w_u, w_d, layer, tm):
    n_blk = blk_e.shape[0]
    d, ff = w_g.shape[2], w_g.shape[3]
    tok = buf_tok.reshape(n_blk, 1, tm)
    hbm = pl.BlockSpec(memory_space=pl.ANY)
    grid_spec = pltpu.PrefetchScalarGridSpec(
        num_scalar_prefetch=3,
        grid=(n_blk,),
        in_specs=[
            pl.BlockSpec((1, 1, tm), lambda i, *_: (i, 0, 0), memory_space=pltpu.SMEM),
            pl.BlockSpec((1, 1, tm), lambda i, *_: (jnp.minimum(i + 1, n_blk - 1), 0, 0), memory_space=pltpu.SMEM),
            pl.BlockSpec((tm, 1), lambda i, *_: (i, 0)),
            hbm, hbm, hbm, hbm,
        ],
        out_specs=pl.BlockSpec((tm * ROW_TILES, LANES), lambda i, *_: (i, 0)),
        scratch_shapes=[pltpu.VMEM((tm * ROW_PITCH, LANES), F32), pltpu.VMEM((tm * ROW_PITCH, LANES), F32),
                        pltpu.VMEM((d, ff), F32), pltpu.VMEM((d, ff), F32), pltpu.VMEM((ff, d), F32),
                        pltpu.VMEM((1, d, ff), BF16), pltpu.VMEM((1, d, ff), BF16), pltpu.VMEM((1, ff, d), BF16),
                        pltpu.SemaphoreType.DMA((2,)), pltpu.SemaphoreType.DMA((3,))],
    )
    return pl.pallas_call(
        functools.partial(_ffn_kernel, tm=tm, layer=layer),
        grid_spec=grid_spec,
        out_shape=jax.ShapeDtypeStruct((n_blk * tm * ROW_TILES, LANES), F32),
        compiler_params=_cparams(("arbitrary",), 56),
        name="moe_experts",
    )(blk_e, nxt_e, n_used, tok, tok, buf_w.reshape(n_blk * tm, 1), h2_rows, w_g, w_u, w_d)


def _combine_kernel(pos_ref, pos_next_ref, x1_ref, yb_hbm, g_ref, *rest, tm, n_tiles, emit_x2):
    x2_ref, (hn_ref, ba_ref, bb_ref, sem) = (rest[0], rest[1:]) if emit_x2 else (None, rest)
    i = pl.program_id(0)
    even = i % 2 == 0
    pitch = ROW_PITCH
    place = lambda r: (r % 2) * tm + r // 2

    @pl.when(i == 0)
    def _():
        _start_rows(yb_hbm, pos_ref, 2 * tm, ba_ref, sem.at[0], place)

    def work(cur_ref, cur_sem, nxt_ref, nxt_sem):
        _wait_rows(yb_hbm, 2 * tm, cur_ref, cur_sem)
        if nxt_ref is not None:
            _start_rows(yb_hbm, pos_next_ref, 2 * tm, nxt_ref, nxt_sem, place)
        pieces = []
        ssq = jnp.zeros((tm, 1), F32)
        for s in range(ROW_TILES):
            piece = (x1_ref[:, s * LANES:(s + 1) * LANES]
                     + cur_ref[pl.ds(s, tm, stride=pitch), :]
                     + cur_ref[pl.ds(tm * pitch + s, tm, stride=pitch), :])
            if x2_ref is not None:
                x2_ref[:, s * LANES:(s + 1) * LANES] = piece
            ssq = ssq + jnp.sum(piece * piece, axis=1, keepdims=True)
            pieces.append(piece)
        inv = lax.rsqrt(ssq / D_MODEL + NORM_EPS)
        for s in range(ROW_TILES):
            sl = slice(s * LANES, (s + 1) * LANES)
            hn_ref[:, sl] = ((pieces[s] * inv) * g_ref[:, sl]).astype(hn_ref.dtype)

    @pl.when((i < n_tiles - 1) & even)
    def _():
        work(ba_ref, sem.at[0], bb_ref, sem.at[1])

    @pl.when((i < n_tiles - 1) & jnp.logical_not(even))
    def _():
        work(bb_ref, sem.at[1], ba_ref, sem.at[0])

    @pl.when(i == n_tiles - 1)
    def _():
        work(bb_ref, sem.at[1], None, None)


def moe_combine(pos, x1, yb_rows, g, hn_dtype, tm, emit_x2):
    n, d = x1.shape
    n_tiles = n // tm
    assert n_tiles % 2 == 0
    row = pl.BlockSpec((tm, d), lambda i: (i, 0))
    pos3 = pos.reshape(n_tiles, 1, 2 * tm)
    x2_shape = [jax.ShapeDtypeStruct((n, d), F32)] if emit_x2 else []
    return pl.pallas_call(
        functools.partial(_combine_kernel, tm=tm, n_tiles=n_tiles, emit_x2=emit_x2),
        grid=(n_tiles,),
        in_specs=[pl.BlockSpec((1, 1, 2 * tm), lambda i: (i, 0, 0), memory_space=pltpu.SMEM),
                  pl.BlockSpec((1, 1, 2 * tm), lambda i: (jnp.minimum(i + 1, n_tiles - 1), 0, 0), memory_space=pltpu.SMEM),
                  row, pl.BlockSpec(memory_space=pl.ANY), pl.BlockSpec((1, d), lambda i: (0, 0))],
        out_specs=[row] * (len(x2_shape) + 1),
        out_shape=x2_shape + [jax.ShapeDtypeStruct((n, d), hn_dtype)],
        scratch_shapes=[pltpu.VMEM((2 * tm * ROW_PITCH, LANES), F32), pltpu.VMEM((2 * tm * ROW_PITCH, LANES), F32),
                        pltpu.SemaphoreType.DMA((2,))],
        compiler_params=_cparams(("arbitrary",), 48),
        name="moe_combine",
    )(pos3, pos3, x1, yb_rows, g.reshape(1, d))


def _dispatch_plan(route, tm):
    n = route.shape[0]
    m = 2 * n
    slot_e = route[:, 0:2].astype(jnp.int32).reshape(m)
    slot_w = route[:, 2:4].reshape(m)
    iota = jnp.arange(m, dtype=jnp.int32)
    se, order, sw = lax.sort((slot_e, iota, slot_w), num_keys=1)
    e_ids = jnp.arange(N_EXPERTS, dtype=jnp.int32)
    counts = jnp.sum((slot_e[None, :] == e_ids[:, None]).astype(jnp.int32), axis=1)
    start = jnp.cumsum(counts) - counts
    padded = (counts + tm - 1) // tm * tm
    pend = jnp.cumsum(padded)
    pstart = pend - padded
    delta = pstart - start
    dest = iota + jnp.sum(jnp.where(se[:, None] == e_ids[None, :], delta[None, :], 0), axis=1)
    _, pos = lax.sort((order, dest), num_keys=1)
    n_blk = (m + N_EXPERTS * tm) // tm
    blk_row0 = jnp.arange(n_blk, dtype=jnp.int32) * tm
    blk_e = jnp.minimum(jnp.sum((pend[None, :] <= blk_row0[:, None]).astype(jnp.int32), axis=1), N_EXPERTS - 1)
    onehot = blk_e[:, None] == e_ids[None, :]
    pick = lambda tab: jnp.sum(jnp.where(onehot, tab[None, :], 0), axis=1)
    local0 = blk_row0 - pick(pstart)
    n_valid = jnp.clip(pick(counts) - local0, 0, tm)
    within = jnp.arange(tm, dtype=jnp.int32)[None, :]
    src = jnp.clip(pick(start)[:, None] + local0[:, None] + within, 0, m - 1)
    valid = within < n_valid[:, None]
    buf_tok = jnp.where(valid, jnp.take(order, src) // 2, 0)
    buf_w = jnp.where(valid, jnp.take(sw, src), 0.0)
    n_used = (pend[-1] // tm).astype(jnp.int32).reshape(1)
    later = (counts[None, :] > 0) & (e_ids[None, :] > e_ids[:, None])
    nxt_tab = jnp.min(jnp.where(later, e_ids[None, :], N_EXPERTS), axis=1)
    nxt_tab = jnp.where(nxt_tab == N_EXPERTS, e_ids, nxt_tab)
    return blk_e.astype(jnp.int32), pick(nxt_tab).astype(jnp.int32), n_used, buf_tok, buf_w, pos


def _rope_tables(s):
    pos = jnp.arange(s, dtype=F32)[:, None]
    inv128 = jnp.exp(-math.log(ROPE_THETA) * jnp.arange(0, HEAD_DIM, 2, dtype=F32) / HEAD_DIM)
    a = pos * inv128[None, :]
    c128 = jnp.concatenate([jnp.cos(a), jnp.cos(a)], axis=1)
    s128 = jnp.concatenate([-jnp.sin(a), jnp.sin(a)], axis=1)
    inv64 = jnp.exp(-math.log(ROPE_THETA) * jnp.arange(0, DIFF_DIM, 2, dtype=F32) / DIFF_DIM)
    a = pos * inv64[None, :]
    co, si, z = jnp.cos(a), jnp.sin(a), jnp.zeros_like(a)
    c64 = jnp.concatenate([co, co, co, co], axis=1)
    sa64 = jnp.concatenate([-si, z, -si, z], axis=1)
    sb64 = jnp.concatenate([z, si, z, si], axis=1)
    return (c128, s128), (c64, sa64, sb64)


def _orig_offsets():
    offs, o = {}, 0
    for name, w in zip(_ORIG_NAMES, _ORIG_WIDTHS):
        offs[name] = (o, w)
        o += w
    return offs


def _w_in_prep_kernel(w_ref, p_ref, r_ref, d_ref, s_ref):
    offs = _orig_offsets()
    for pieces, dst_ref in ((_PLAIN, p_ref), (_ROPE128, r_ref), (_ROPE64, d_ref)):
        dst = 0
        for name, width in pieces:
            src = offs[name][0]
            dst_ref[:, dst:dst + width] = w_ref[0, :, src:src + width].astype(BF16)
            dst += width
    (f0, fw), (g0, gw) = offs["ff"], offs["ngt"]
    s_ref[...] = jnp.zeros(s_ref.shape, BF16)
    s_ref[:, 0:fw] = w_ref[0, :, f0:f0 + fw].astype(BF16)
    s_ref[:, fw:fw + gw] = w_ref[0, :, g0:g0 + gw].astype(BF16)


def split_w_in(w_in, layer, tr=256):
    _, k, c = w_in.shape
    row = lambda w: pl.BlockSpec((tr, w), lambda i: (i, 0))
    shape = lambda w: jax.ShapeDtypeStruct((k, w), BF16)
    return pl.pallas_call(
        _w_in_prep_kernel,
        grid=(k // tr,),
        in_specs=[pl.BlockSpec((1, tr, c), lambda i: (layer, i, 0))],
        out_specs=[row(C_PLAIN), row(C_ROPE128), row(C_ROPE64), row(LANES)],
        out_shape=[shape(C_PLAIN), shape(C_ROPE128), shape(C_ROPE64), shape(LANES)],
        compiler_params=_cparams(("parallel",), 48),
        name="w_in_prep",
    )(w_in)


def _tiles(s, n):
    return dict(t_attn=min(512, s), tq_nsa=min(256, s), tk_sb=min(256, s), tm_norm=min(512, n), tm_proj=min(512, n),
                tm_out=min(512, n), tm_comb=min(256, n), tm_moe=256)


def kernel(x, norm1_g, w_in, fox_bf, nsa_pe_k, nsa_pe_v, nsa_w_ck, nsa_w_cv, diff_lq1, diff_lk1, diff_lq2, diff_lk2, diff_norm_g, w_branch, w_mgate, w_out, norm2_g, w_rg, b_rg, w_re, b_re, w_eg, w_eu, w_ed, final_g):
    b, s, d = x.shape
    n = b * s
    depth = w_in.shape[0]
    t = _tiles(s, n)
    rope128_tabs, rope64_tabs = _rope_tables(s)
    xf = x.reshape(n, d)
    h = rms_norm_rows(xf, norm1_g[0], BF16, t["tm_norm"])
    out = None
    for l in range(depth):
        lam_init = 0.8 - 0.6 * math.exp(-0.3 * l)
        w_p, w_r, w_d, w_s = split_w_in(w_in, l)
        u_p = in_proj_plain(h, w_p, t["tm_proj"], C_PLAIN // 2).reshape(b, s, C_PLAIN)
        u_r, u_d, small = in_proj_rope(h, w_r, w_d, w_s, rope128_tabs, rope64_tabs, s, t["tm_proj"])
        u_r, u_d, small = u_r.reshape(b, s, C_ROPE128), u_d.reshape(b, s, C_ROPE64), small.reshape(b, s, LANES)
        cum = forget_cumsum(small[:, :, 0:N_HEADS].transpose(0, 2, 1), fox_bf[l])
        o_fox = fox_attention(u_p, cum, t["t_attn"])
        o_sb = sb_attention(u_p, t["t_attn"], t["tk_sb"])
        o_diff = diff_attention(u_d, u_p, diff_lq1[l], diff_lk1[l], diff_lq2[l], diff_lk2[l], diff_norm_g[l],
                                lam_init, t["t_attn"])
        kc, vc = nsa_compress(u_r, u_p, nsa_pe_k[l], nsa_pe_v[l], nsa_w_ck[l], nsa_w_cv[l])
        o_c, mt = nsa_cmp_attention(u_r, kc, vc, t["tq_nsa"])
        nvs0 = COL_P["nvs"] * LANES
        vs_t = u_p[:, :, nvs0:nvs0 + 2 * LANES].transpose(0, 2, 1)
        o_s = nsa_sel_attention(u_r, vs_t, mt, t["t_attn"])
        o_nsa = nsa_win_attention(u_r, u_p, o_c, o_s, small, t["t_attn"])
        branches = [o.reshape(n, BRANCH_WIDTH) for o in (o_fox, o_nsa, o_sb, o_diff)]
        merged = merge_branches(h, branches, w_mgate[l].astype(BF16), w_branch[l].astype(BF16), t["tm_proj"], 512)
        w_router = jnp.concatenate([w_rg[l], w_re[l], jnp.zeros((d, LANES - N_GROUPS - N_EXPERTS), F32)], axis=1).astype(BF16)
        b_router = jnp.concatenate([b_rg[l], b_re[l], jnp.zeros((LANES - N_GROUPS - N_EXPERTS,), F32)]).reshape(1, LANES)
        x1, h2_rows, route = out_proj_router(merged, xf, w_out[l].astype(BF16), norm2_g[l], w_router, b_router, t["tm_out"])
        blk_e, nxt_e, n_used, buf_tok, buf_w, pos = _dispatch_plan(route, t["tm_moe"])
        yb_rows = moe_experts(blk_e, nxt_e, n_used, buf_tok, buf_w, h2_rows, w_eg, w_eu, w_ed, l, t["tm_moe"])
        last = l == depth - 1
        g_next = final_g if last else norm1_g[l + 1]
        res = moe_combine(pos, x1, yb_rows, g_next, F32 if last else BF16, t["tm_comb"], emit_x2=not last)
        h = out = res[-1]
        xf = res[0]
    return out.reshape(b, s, d)
```
